```python
import jax
import jax.numpy as jnp
from jax import lax
import numpy as np

D_MODEL = 4096
BATCH = 4
SEQ = 2048
DEPTH = 1
DEC_BATCH = 32
DEC_SEQ = 4
PAST_LEN = 8192
PAGE_SIZE = 128

A_HEADS = 32
A_EXPAND = 128
A_HEAD_V = D_MODEL // A_HEADS
A_KDIM = A_HEADS * A_EXPAND
A_VDIM = A_HEADS * A_HEAD_V
A_CHUNK = 32
B_HEADS = 32
B_KV_HEADS = 8
B_HEAD_DIM = D_MODEL // B_HEADS
B_QDIM = B_HEADS * B_HEAD_DIM
B_KVDIM = B_KV_HEADS * B_HEAD_DIM
IDX_HEADS = 32
IDX_DIM = 128
TOPK_MAX = 256
Q_BLOCK = 128
N_GROUPS = 8
EXPERTS_PER_GROUP = 8
N_EXPERTS = N_GROUPS * EXPERTS_PER_GROUP
TOP_K_EXPERTS = 2
D_FF_EXPERT = 1024
MOE_BLOCK = 128
NORM_EPS = 1e-6

kernel_name = 'hgrn2_dsa_hmoe_decode_step'


def _in_sizes():
    return (A_KDIM, A_KDIM, A_VDIM, A_VDIM,
            B_QDIM, B_KVDIM, B_KVDIM, IDX_HEADS * IDX_DIM, IDX_DIM, IDX_HEADS,
            D_MODEL, D_MODEL)


def rmsnorm(x, g):
    xf = x.astype(jnp.float32)
    xf = xf * lax.rsqrt(jnp.mean(xf * xf, axis=-1, keepdims=True) + NORM_EPS)
    return (xf * g.astype(jnp.float32)).astype(x.dtype)


def split_projection(z):
    cuts = [int(c) for c in np.cumsum(_in_sizes())[:-1]]
    return jnp.split(z, cuts, axis=-1)


def heads_first(a, n_heads):
    b, t, w = a.shape
    return a.reshape(b, t, n_heads, w // n_heads).transpose(0, 2, 1, 3)


def hgrn2_inputs(zq, zf, zi, lb):
    f32 = jnp.float32
    f = lb + (1.0 - lb) * jax.nn.sigmoid(zf.astype(f32))
    q = heads_first(jax.nn.silu(zq.astype(f32)), A_HEADS)
    k = heads_first(1.0 - f, A_HEADS)
    logf = heads_first(jnp.log(f), A_HEADS)
    v = heads_first(zi.astype(f32), A_HEADS)
    return q, k, v, logf


def gla_chunk(S, xs):
    q, k, v, logf = xs
    c = q.shape[2]
    b_cum = jnp.cumsum(logf, axis=2)
    causal = jnp.tril(jnp.ones((c, c), dtype=bool))
    diff = b_cum[:, :, :, None, :] - b_cum[:, :, None, :, :]
    decay = jnp.exp(jnp.where(causal[None, None, :, :, None], diff, -jnp.inf))
    scores = jnp.einsum('bhtd,bhsd,bhtsd->bhts', q, k, decay)
    o = (jnp.einsum('bhts,bhsv->bhtv', scores, v)
         + jnp.einsum('bhtd,bhdv->bhtv', q * jnp.exp(b_cum), S))
    b_last = b_cum[:, :, -1:, :]
    S_new = (jnp.exp(b_last[:, :, 0, :, None]) * S
             + jnp.einsum('bhsd,bhsv->bhdv', k * jnp.exp(b_last - b_cum), v))
    return S_new, o


def hgrn2_prompt(q, k, v, logf):
    b, h, t, _ = q.shape
    nc = t // A_CHUNK

    def to_chunks(a):
        return a.reshape(b, h, nc, A_CHUNK, a.shape[-1]).transpose(2, 0, 1, 3, 4)

    S0 = jnp.zeros((b, h, A_EXPAND, A_HEAD_V), jnp.float32)
    S_fin, o = lax.scan(gla_chunk, S0, (to_chunks(q), to_chunks(k), to_chunks(v), to_chunks(logf)))
    o = o.transpose(1, 2, 0, 3, 4).reshape(b, h, t, A_HEAD_V)
    return S_fin, o


def hgrn2_readout(o, zg, gain):
    b, h, t, dv = o.shape
    o = rmsnorm(o.transpose(0, 2, 1, 3), gain.reshape(h, dv)).reshape(b, t, h * dv)
    return (o * jax.nn.silu(zg.astype(jnp.float32))).astype(zg.dtype)


def dsa_inputs(zq, zk, zv, zqi, zki, zw):
    b, t, _ = zq.shape
    q = zq.reshape(b, t, B_HEADS, B_HEAD_DIM)
    k = zk.reshape(b, t, B_KV_HEADS, B_HEAD_DIM)
    v = zv.reshape(b, t, B_KV_HEADS, B_HEAD_DIM)
    q_idx = zqi.reshape(b, t, IDX_HEADS, IDX_DIM)
    w_idx = zw.astype(jnp.float32) * IDX_HEADS ** -0.5
    return q, k, v, q_idx, zki, w_idx


def indexer_scores(q_idx, w_idx, k_idx):
    s = jnp.einsum('bqhd,bsd->bqhs', q_idx, k_idx, preferred_element_type=jnp.float32) * IDX_DIM ** -0.5
    return jnp.einsum('bqhs,bqh->bqs', jax.nn.relu(s), w_idx)


def select_keys(scores, q_pos, topk):
    n_keys = scores.shape[-1]
    admissible = jnp.arange(n_keys)[None, :] <= q_pos[:, None]
    _, idx = lax.top_k(jnp.where(admissible[None], scores, -jnp.inf), topk)
    valid = idx <= q_pos[None, :, None]
    return idx, valid


def take_rows(a, idx):
    return jax.vmap(lambda ab, ib: ab[ib])(a, idx)


def sparse_attend(q, k_sel, v_sel, valid):
    b, t = q.shape[:2]
    qg = q.reshape(b, t, B_KV_HEADS, B_HEADS // B_KV_HEADS, B_HEAD_DIM)
    s = jnp.einsum('bqhgd,bqnhd->bqhgn', qg, k_sel, preferred_element_type=jnp.float32) * B_HEAD_DIM ** -0.5
    p = jax.nn.softmax(jnp.where(valid[:, :, None, None, :], s, -jnp.inf), axis=-1)
    o = jnp.einsum('bqhgn,bqnhd->bqhgd', p.astype(v_sel.dtype), v_sel)
    return o.reshape(b, t, B_QDIM)


def dsa_prompt(q, k, v, q_idx, w_idx, k_idx):
    b, t = q.shape[:2]
    topk = min(TOPK_MAX, t // 4)

    def block(j):
        s0 = j * Q_BLOCK
        sl = lambda a: lax.dynamic_slice_in_dim(a, s0, Q_BLOCK, axis=1)
        pos = s0 + jnp.arange(Q_BLOCK)
        idx, valid = select_keys(indexer_scores(sl(q_idx), sl(w_idx), k_idx), pos, topk)
        return sparse_attend(sl(q), take_rows(k, idx), take_rows(v, idx), valid)

    o = lax.map(block, jnp.arange(t // Q_BLOCK))
    return o.transpose(1, 0, 2, 3).reshape(b, t, B_QDIM)


def dsa_sample(q, k, v, q_idx, w_idx, k_idx, cache_k, cache_v, cache_idx_k, page_table):
    b, t = q.shape[:2]
    past = page_table.shape[1] * PAGE_SIZE
    topk = min(TOPK_MAX, (past + t) // 4)
    k_idx_past = cache_idx_k[page_table].reshape(b, past, IDX_DIM)
    k_idx_all = jnp.concatenate([k_idx_past.astype(k_idx.dtype), k_idx], axis=1)
    pos = past + jnp.arange(t)
    idx, valid = select_keys(indexer_scores(q_idx, w_idx, k_idx_all), pos, topk)
    is_past = (idx < past)[..., None, None]
    p_idx = jnp.minimum(idx, past - 1)
    phys_page = jnp.take_along_axis(page_table, (p_idx // PAGE_SIZE).reshape(b, -1), axis=1).reshape(idx.shape)
    rows = phys_page * PAGE_SIZE + p_idx % PAGE_SIZE
    pool_k = cache_k.reshape(-1, B_KV_HEADS, B_HEAD_DIM)
    pool_v = cache_v.reshape(-1, B_KV_HEADS, B_HEAD_DIM)
    n_idx = jnp.clip(idx - past, 0, t - 1)
    k_sel = jnp.where(is_past, pool_k[rows].astype(k.dtype), take_rows(k, n_idx))
    v_sel = jnp.where(is_past, pool_v[rows].astype(v.dtype), take_rows(v, n_idx))
    return sparse_attend(q, k_sel, v_sel, valid)


def merge_branches(oa, ob, za, zb, w_out):
    m = jax.nn.sigmoid(za) * oa + jax.nn.sigmoid(zb) * ob
    return m @ w_out


def token_mixer_prompt(h, w_in, lb, g_hout, w_out):
    zq, zf, zi, zg, bq, bk, bv, bqi, bki, bw, ga, gb = split_projection(h @ w_in)
    S_fin, o = hgrn2_prompt(*hgrn2_inputs(zq, zf, zi, lb))
    oa = hgrn2_readout(o, zg, g_hout)
    q, k, v, q_idx, k_idx, w_idx = dsa_inputs(bq, bk, bv, bqi, bki, bw)
    ob = dsa_prompt(q, k, v, q_idx, w_idx, k_idx)
    return merge_branches(oa, ob, ga, gb, w_out), (k, v, k_idx, S_fin)


def token_mixer_sample(h, cache_k, cache_v, cache_idx_k, state, page_table, w_in, lb, g_hout, w_out):
    zq, zf, zi, zg, bq, bk, bv, bqi, bki, bw, ga, gb = split_projection(h @ w_in)
    S_new, o = gla_chunk(state.astype(jnp.float32), hgrn2_inputs(zq, zf, zi, lb))
    oa = hgrn2_readout(o, zg, g_hout)
    q, k, v, q_idx, k_idx, w_idx = dsa_inputs(bq, bk, bv, bqi, bki, bw)
    ob = dsa_sample(q, k, v, q_idx, w_idx, k_idx, cache_k, cache_v, cache_idx_k, page_table)
    return merge_branches(oa, ob, ga, gb, w_out), (k, v, k_idx, S_new)


def routed_experts(x, experts, gates, w_gate, w_up, w_down):
    n, d = x.shape
    nk = n * TOP_K_EXPERTS
    n_blocks = -(-nk // MOE_BLOCK) + N_EXPERTS
    flat_e = experts.reshape(-1)
    flat_tok = jnp.arange(nk, dtype=jnp.int32) // TOP_K_EXPERTS
    flat_w = gates.reshape(-1)
    order = jnp.argsort(flat_e)
    e_sorted = flat_e[order]
    counts = jnp.zeros((N_EXPERTS,), jnp.int32).at[flat_e].add(1)
    padded = (counts + MOE_BLOCK - 1) // MOE_BLOCK * MOE_BLOCK
    pad_end = jnp.cumsum(padded)
    pad_start = pad_end - padded
    start = jnp.cumsum(counts) - counts
    dest = pad_start[e_sorted] + jnp.arange(nk, dtype=jnp.int32) - start[e_sorted]
    row_tok = jnp.full((n_blocks * MOE_BLOCK,), n, jnp.int32).at[dest].set(flat_tok[order])
    row_w = jnp.zeros((n_blocks * MOE_BLOCK,), flat_w.dtype).at[dest].set(flat_w[order])
    block_e = jnp.minimum(jnp.searchsorted(pad_end, jnp.arange(n_blocks, dtype=jnp.int32) * MOE_BLOCK, side='right'), N_EXPERTS - 1)
    x_pad = jnp.concatenate([x, jnp.zeros((1, d), x.dtype)], axis=0)
    xb = x_pad[row_tok].reshape(n_blocks, MOE_BLOCK, d)

    def run(args):
        xe, e = args
        a = jax.nn.silu(xe @ w_gate[e]) * (xe @ w_up[e])
        return a @ w_down[e]

    yb = lax.map(run, (xb, block_e)).reshape(-1, d)
    y = jax.ops.segment_sum(yb * row_w[:, None].astype(yb.dtype), row_tok, num_segments=n + 1)
    return y[:n]


def hierarchical_moe(h, w_rg, b_rg, w_re, b_re, w_gate, w_up, w_down):
    shape = h.shape
    x = h.reshape(-1, shape[-1])
    n = x.shape[0]
    f32 = jnp.float32
    p_group = jax.nn.softmax((x @ w_rg).astype(f32) + b_rg.astype(f32), axis=-1)
    p_g, g_sel = lax.top_k(p_group, 1)
    e_logits = ((x @ w_re).astype(f32) + b_re.astype(f32)).reshape(n, N_GROUPS, EXPERTS_PER_GROUP)
    e_logits = jnp.take_along_axis(e_logits, g_sel[:, :, None], axis=1)[:, 0]
    p_e, e_loc = lax.top_k(jax.nn.softmax(e_logits, axis=-1), TOP_K_EXPERTS)
    gates = p_g * p_e / jnp.sum(p_e, axis=-1, keepdims=True)
    experts = g_sel * EXPERTS_PER_GROUP + e_loc
    return routed_experts(x, experts, gates, w_gate, w_up, w_down).reshape(shape)


def setup_inputs(seed: int = 0) -> dict:
    key = jax.random.key(seed)
    ks = jax.random.split(key, 21)
    f32 = jnp.float32
    nrm = jax.random.normal
    n_pages = PAST_LEN // PAGE_SIZE
    n_used = DEC_BATCH * n_pages
    n_pool = n_used + n_used // 4
    n_in = sum(_in_sizes())

    def gain(k, shape):
        return 1.0 + 0.02 * nrm(k, shape, f32)

    return {
        'x_prompt': nrm(ks[0], (BATCH, SEQ, D_MODEL), f32),
        'x_sample': nrm(ks[1], (DEC_BATCH, DEC_SEQ, D_MODEL), f32),
        'cache_k': nrm(ks[2], (DEPTH, n_pool, PAGE_SIZE, B_KV_HEADS, B_HEAD_DIM), f32),
        'cache_v': nrm(ks[3], (DEPTH, n_pool, PAGE_SIZE, B_KV_HEADS, B_HEAD_DIM), f32),
        'cache_idx_k': nrm(ks[4], (DEPTH, n_pool, PAGE_SIZE, IDX_DIM), f32),
        'state_hgrn': 0.5 * nrm(ks[5], (DEPTH, DEC_BATCH, A_HEADS, A_EXPAND, A_HEAD_V), f32),
        'page_table': jax.random.permutation(ks[6], n_pool)[:n_used].reshape(DEC_BATCH, n_pages).astype(jnp.int32),
        'norm_mix': gain(ks[7], (DEPTH, D_MODEL)),
        'w_in': nrm(ks[8], (DEPTH, D_MODEL, n_in), f32) * D_MODEL ** -0.5,
        'lb_logits': 0.5 * nrm(ks[9], (DEPTH + 1, A_KDIM), f32),
        'norm_hgrn_out': gain(ks[10], (DEPTH, A_VDIM)),
        'w_out': nrm(ks[11], (DEPTH, D_MODEL, D_MODEL), f32) * D_MODEL ** -0.5,
        'norm_ffn': gain(ks[12], (DEPTH, D_MODEL)),
        'router_group': nrm(ks[13], (DEPTH, D_MODEL, N_GROUPS), f32) * D_MODEL ** -0.5,
        'router_group_bias': 0.01 * nrm(ks[14], (DEPTH, N_GROUPS), f32),
        'router_expert': nrm(ks[15], (DEPTH, D_MODEL, N_EXPERTS), f32) * D_MODEL ** -0.5,
        'router_expert_bias': 0.01 * nrm(ks[16], (DEPTH, N_EXPERTS), f32),
        'expert_w_gate': nrm(ks[17], (DEPTH, N_EXPERTS, D_MODEL, D_FF_EXPERT), f32) * D_MODEL ** -0.5,
        'expert_w_up': nrm(ks[18], (DEPTH, N_EXPERTS, D_MODEL, D_FF_EXPERT), f32) * D_MODEL ** -0.5,
        'expert_w_down': nrm(ks[19], (DEPTH, N_EXPERTS, D_FF_EXPERT, D_MODEL), f32) * D_FF_EXPERT ** -0.5,
        'norm_final': gain(ks[20], (D_MODEL,)),
    }


def reference(x_prompt, x_sample, cache_k, cache_v, cache_idx_k, state_hgrn, page_table,
              norm_mix, w_in, lb_logits, norm_hgrn_out, w_out, norm_ffn,
              router_group, router_group_bias, router_expert, router_expert_bias,
              expert_w_gate, expert_w_up, expert_w_down, norm_final):
    lower_bounds = jnp.cumsum(jax.nn.softmax(lb_logits.astype(jnp.float32), axis=0), axis=0)
    xp, xs = x_prompt, x_sample
    kp, vp, ikp, sp = [], [], [], []
    ksm, vsm, iks, ss = [], [], [], []
    for l in range(DEPTH):
        yp, st_p = token_mixer_prompt(rmsnorm(xp, norm_mix[l]), w_in[l], lower_bounds[l], norm_hgrn_out[l], w_out[l])
        ys, st_s = token_mixer_sample(rmsnorm(xs, norm_mix[l]), cache_k[l], cache_v[l], cache_idx_k[l], state_hgrn[l],
                                      page_table, w_in[l], lower_bounds[l], norm_hgrn_out[l], w_out[l])
        xp = xp + yp
        xs = xs + ys
        moe_w = (router_group[l], router_group_bias[l], router_expert[l], router_expert_bias[l],
                 expert_w_gate[l], expert_w_up[l], expert_w_down[l])
        xp = xp + hierarchical_moe(rmsnorm(xp, norm_ffn[l]), *moe_w)
        xs = xs + hierarchical_moe(rmsnorm(xs, norm_ffn[l]), *moe_w)
        kp.append(st_p[0]); vp.append(st_p[1]); ikp.append(st_p[2]); sp.append(st_p[3])
        ksm.append(st_s[0]); vsm.append(st_s[1]); iks.append(st_s[2]); ss.append(st_s[3])
    y_prompt = rmsnorm(xp, norm_final)
    y_sample = rmsnorm(xs, norm_final)
    return (y_prompt, y_sample,
            jnp.stack(kp), jnp.stack(vp), jnp.stack(ikp), jnp.stack(sp),
            jnp.stack(ksm), jnp.stack(vsm), jnp.stack(iks), jnp.stack(ss))
```

```python
import functools
import math

import jax
import jax.numpy as jnp
import numpy as np
from jax import lax
from jax.experimental import pallas as pl
from jax.experimental.pallas import tpu as pltpu

A_HEADS = 32
A_EXPAND = 128
B_HEADS = 32
B_KV_HEADS = 8
IDX_HEADS = 32
IDX_DIM = 128
TOPK_MAX = 256
N_GROUPS = 8
EXPERTS_PER_GROUP = 8
NORM_EPS = 1e-6

LANE = 128
ROW_BLOCK = 128
MOE_BLOCK = 128
MOE_ITEM_BLOCKS = 4
MOE_FF_TILE = 256
HGRN_CHUNK = 128
HGRN_ROWS = 512
HGRN_SAMPLE_ROWS = 16
NEG_BIG = -1e30
VMEM_LIMIT = 56 * 1024 * 1024

f32 = jnp.float32
bf16 = jnp.bfloat16
i32 = jnp.int32
HIGHEST = lax.Precision.HIGHEST
NT_DIMS = (((1,), (1,)), ((), ()))
TN_DIMS = (((0,), (0,)), ((), ()))


def _pick_tile(n, cap, align):
    t = (min(cap, n) // align) * align
    while t > align and n % t:
        t -= align
    assert t > 0 and n % t == 0, (n, cap, align)
    return t


def _params(sem, limit=VMEM_LIMIT):
    return pltpu.CompilerParams(dimension_semantics=sem, vmem_limit_bytes=limit)


def _sigmoid(x):
    return 1.0 / (1.0 + jnp.exp(-x))


def _log2(n):
    s = int(math.log2(n))
    assert 1 << s == n, n
    return s


def _rmsnorm_kernel(x_ref, g_ref, o_ref):
    x = x_ref[...]
    ms = jnp.mean(x * x, axis=-1, keepdims=True)
    o_ref[...] = (x * lax.rsqrt(ms + NORM_EPS) * g_ref[...]).astype(o_ref.dtype)


def _rmsnorm_bf16(x, g):
    n, d = x.shape
    return pl.pallas_call(
        _rmsnorm_kernel,
        grid=(n // ROW_BLOCK,),
        in_specs=[pl.BlockSpec((ROW_BLOCK, d), lambda i: (i, 0)),
                  pl.BlockSpec((1, d), lambda i: (0, 0))],
        out_specs=pl.BlockSpec((ROW_BLOCK, d), lambda i: (i, 0)),
        out_shape=jax.ShapeDtypeStruct((n, d), bf16),
        compiler_params=_params(("parallel",)),
        name="rmsnorm_bf16",
    )(x, g.reshape(1, d))


def _mm_kernel(a_ref, b_ref, o_ref):
    o_ref[...] = jnp.dot(a_ref[...], b_ref[...], preferred_element_type=f32)


def _mm_res_kernel(a_ref, b_ref, r_ref, o_ref):
    o_ref[...] = r_ref[...] + jnp.dot(a_ref[...], b_ref[...], preferred_element_type=f32)


def _matmul(a, b, res=None, name="matmul"):
    m, k = a.shape
    n = b.shape[1]
    tm = _pick_tile(m, 640, 128)
    tn = _pick_tile(n, 1024, 128)
    in_specs = [pl.BlockSpec((tm, k), lambda j, i: (i, 0)),
                pl.BlockSpec((k, tn), lambda j, i: (0, j))]
    args = [a, b]
    kern = _mm_kernel
    if res is not None:
        in_specs.append(pl.BlockSpec((tm, tn), lambda j, i: (i, j)))
        args.append(res)
        kern = _mm_res_kernel
    return pl.pallas_call(
        kern,
        grid=(n // tn, m // tm),
        in_specs=in_specs,
        out_specs=pl.BlockSpec((tm, tn), lambda j, i: (i, j)),
        out_shape=jax.ShapeDtypeStruct((m, n), f32),
        compiler_params=_params(("parallel", "parallel")),
        name=name,
    )(*args)


def _hgrn_static(c):
    levels = int(math.log2(c))
    assert 1 << levels == c
    t = np.arange(c)
    mat = np.zeros(((levels + 1) * c, c), np.float32)
    mat[:c] = t[None, :] <= t[:, None]
    lv = np.full((c, c), -1, np.int32)
    for l in range(levels):
        blk = c >> l
        half = blk // 2
        mid = (t // blk) * blk + half
        mat[(l + 1) * c:(l + 2) * c] = t[None, :] <= (mid - 1)[:, None]
        same = (t[:, None] // blk) == (t[None, :] // blk)
        up = ((t // half) % 2 == 1)[:, None]
        lo = ((t // half) % 2 == 0)[None, :]
        lv[same & up & lo] = l
    lv[t, t] = levels
    return mat, lv, levels


def _hgrn_kernel(*refs, chunk, n_chunks, levels, n_valid, has_init):
    if has_init:
        (zq_ref, zf_ref, zi_ref, zg_ref, lb_ref, gain_ref, mat_ref, lv_ref, s0_ref,
         oa_ref, sfin_ref, st_ref) = refs
    else:
        (zq_ref, zf_ref, zi_ref, zg_ref, lb_ref, gain_ref, mat_ref, lv_ref,
         oa_ref, sfin_ref, st_ref) = refs
        s0_ref = None
    tb = pl.program_id(2)
    c = chunk

    @pl.when(tb == 0)
    def _():
        if has_init:
            st_ref[...] = s0_ref[0, 0].T
        else:
            st_ref[...] = jnp.zeros_like(st_ref)

    lb = lb_ref[...]
    gain = gain_ref[...]
    lv = lv_ref[...]
    row = lax.broadcasted_iota(i32, (c, A_EXPAND), 0)

    def one_chunk(ci, carry):
        r0 = pl.multiple_of(ci * c, c)
        zq = zq_ref[pl.ds(r0, c), :]
        zf = zf_ref[pl.ds(r0, c), :]
        v = zi_ref[pl.ds(r0, c), :]
        zg = zg_ref[pl.ds(r0, c), :]
        f = lb + (1.0 - lb) * _sigmoid(zf)
        logf = jnp.log(f)
        k = 1.0 - f
        q = zq * _sigmoid(zq)
        if n_valid < c:
            live = row < n_valid
            logf = jnp.where(live, logf, 0.0)
            k = jnp.where(live, k, 0.0)
        cum = jnp.dot(mat_ref[...], logf, precision=HIGHEST, preferred_element_type=f32)
        b = cum[0:c]
        a = jnp.where(lv == levels, jnp.sum(q * k, axis=-1, keepdims=True), 0.0)
        for l in range(levels):
            ref_b = cum[(l + 1) * c:(l + 2) * c]
            e = jnp.exp(-jnp.abs(b - ref_b))
            upper = ((row >> (levels - l - 1)) & 1) == 1
            qt = jnp.where(upper, q * e, 0.0).astype(bf16)
            kt = jnp.where(upper, 0.0, k * e).astype(bf16)
            al = lax.dot_general(qt, kt, NT_DIMS, preferred_element_type=f32)
            a = jnp.where(lv == l, al, a)
        st = st_ref[...]
        qh = (q * jnp.exp(b)).astype(bf16)
        o = (jnp.dot(a.astype(bf16), v.astype(bf16), preferred_element_type=f32)
             + lax.dot_general(qh, st.astype(bf16), NT_DIMS, preferred_element_type=f32))
        b_last = b[c - 1:c, :]
        kd = (k * jnp.exp(b_last - b)).astype(bf16)
        st_ref[...] = (st * jnp.exp(b_last)
                       + lax.dot_general(v.astype(bf16), kd, TN_DIMS, preferred_element_type=f32))
        ms = jnp.mean(o * o, axis=-1, keepdims=True)
        on = o * lax.rsqrt(ms + NORM_EPS) * gain
        oa_ref[pl.ds(r0, c), :] = on * (zg * _sigmoid(zg))
        return carry

    lax.fori_loop(0, n_chunks, one_chunk, 0)

    @pl.when(tb == pl.num_programs(2) - 1)
    def _():
        sfin_ref[0, 0] = st_ref[...].T


def _hgrn(z_a, lb, gain, n_seq, seq_rows, row_block, chunk, n_valid, s0=None):
    kd = A_HEADS * A_EXPAND
    dv = LANE
    assert z_a.shape[1] == 2 * kd + 2 * A_HEADS * dv and A_EXPAND == LANE
    mat, lv, levels = _hgrn_static(chunk)
    nrb = seq_rows // row_block
    hb = kd // LANE

    def zspec(seg):
        return pl.BlockSpec((row_block, LANE), lambda b, h, t, seg=seg: (b * nrb + t, seg * hb + h))

    in_specs = [zspec(0), zspec(1), zspec(2), zspec(3),
                pl.BlockSpec((1, LANE), lambda b, h, t: (0, h)),
                pl.BlockSpec((1, LANE), lambda b, h, t: (0, h)),
                pl.BlockSpec(mat.shape, lambda b, h, t: (0, 0)),
                pl.BlockSpec(lv.shape, lambda b, h, t: (0, 0))]
    args = [z_a, z_a, z_a, z_a, lb.reshape(1, kd), gain.reshape(1, -1), jnp.asarray(mat), jnp.asarray(lv)]
    if s0 is not None:
        in_specs.append(pl.BlockSpec((1, 1, A_EXPAND, dv), lambda b, h, t: (b, h, 0, 0)))
        args.append(s0)
    kern = functools.partial(_hgrn_kernel, chunk=chunk, n_chunks=row_block // chunk, levels=levels,
                             n_valid=n_valid, has_init=s0 is not None)
    return pl.pallas_call(
        kern,
        grid=(n_seq, A_HEADS, nrb),
        in_specs=in_specs,
        out_specs=[pl.BlockSpec((row_block, LANE), lambda b, h, t: (b * nrb + t, h)),
                   pl.BlockSpec((1, 1, A_EXPAND, dv), lambda b, h, t: (b, h, 0, 0))],
        out_shape=[jax.ShapeDtypeStruct((n_seq * seq_rows, A_HEADS * dv), f32),
                   jax.ShapeDtypeStruct((n_seq, A_HEADS, A_EXPAND, dv), f32)],
        scratch_shapes=[pltpu.VMEM((dv, A_EXPAND), f32)],
        compiler_params=_params(("parallel", "parallel", "arbitrary")),
        name="hgrn2_prompt" if s0 is None else "hgrn2_sample",
    )(*args)


def _sort_key(x):
    b = pltpu.bitcast(x + 0.0, i32)
    return b ^ ((b >> 31) & 0x7FFFFFFF)


def _kth_largest_key(key_ref, k, axis):
    shape = (1, key_ref.shape[1]) if axis == 0 else (key_ref.shape[0], 1)
    int_min = jnp.int32(-2 ** 31)

    def step(i, lo):
        cand = lo ^ lax.shift_left(jnp.int32(1), 31 - i)
        cnt = jnp.sum((key_ref[...] >= cand).astype(i32), axis=axis, keepdims=True)
        return jnp.where(cnt >= k, cand, lo)

    return lax.fori_loop(0, 32, step, jnp.full(shape, int_min, i32))


def _dsa_prompt_kernel(qi_ref, smq_ref, smk_ref, q_ref, k_ref, v_ref, o_ref, mask_ref, key_ref,
                       *, seq, topk, qblk, group):
    qb = pl.program_id(1)
    g = pl.program_id(2)

    @pl.when(g == 0)
    def _():
        kidx = smk_ref[:, 0:IDX_DIM].astype(bf16)
        w_t = smq_ref[...].T * (IDX_HEADS ** -0.5 * IDX_DIM ** -0.5)
        wrow = lax.broadcasted_iota(i32, w_t.shape, 0)

        mask_ref[...] = jnp.zeros_like(mask_ref)

        def head(h, carry):
            c0 = pl.multiple_of(h * IDX_DIM, IDX_DIM)
            qh = qi_ref[:, pl.ds(c0, IDX_DIM)].astype(bf16)
            s = lax.dot_general(kidx, qh, NT_DIMS, preferred_element_type=f32)
            wh = jnp.sum(jnp.where(wrow == IDX_DIM + h, w_t, 0.0), axis=0, keepdims=True)
            mask_ref[...] += jnp.maximum(s, 0.0) * wh
            return carry

        lax.fori_loop(0, IDX_HEADS, head, 0)
        acc = mask_ref[...]
        kpos = lax.broadcasted_iota(i32, (seq, qblk), 0)
        qpos = qb * qblk + lax.broadcasted_iota(i32, (seq, qblk), 1)
        adm = kpos <= qpos
        key_ref[...] = _sort_key(jnp.where(adm, acc, -jnp.inf))
        thr = _kth_largest_key(key_ref, topk, axis=0)
        keys = key_ref[...]
        mask_ref[...] = jnp.where(jnp.logical_and(adm, keys >= thr), 0.0, -jnp.inf)
        need = topk - jnp.sum((keys > thr).astype(i32), axis=0, keepdims=True)
        n_tie = jnp.sum((keys == thr).astype(i32), axis=0, keepdims=True)

        @pl.when(jnp.max(n_tie - need) > 0)
        def _():
            strict = (lax.broadcasted_iota(i32, (qblk, qblk), 0) > lax.broadcasted_iota(i32, (qblk, qblk), 1))
            strict = jnp.where(strict, 1.0, 0.0).astype(bf16)
            need_f = need.astype(f32)

            def tile(ti, seen):
                r0 = pl.multiple_of(ti * qblk, qblk)
                kt = key_ref[pl.ds(r0, qblk), :]
                tie = kt == thr
                tie_f = jnp.where(tie, 1.0, 0.0)
                before = seen + jnp.dot(strict, tie_f.astype(bf16), preferred_element_type=f32)
                keep = jnp.logical_or(kt > thr, jnp.logical_and(tie, before < need_f))
                rpos = r0 + lax.broadcasted_iota(i32, (qblk, qblk), 0)
                cpos = qb * qblk + lax.broadcasted_iota(i32, (qblk, qblk), 1)
                keep = jnp.logical_and(keep, rpos <= cpos)
                mask_ref[pl.ds(r0, qblk), :] = jnp.where(keep, 0.0, -jnp.inf)
                return seen + jnp.sum(tie_f, axis=0, keepdims=True)

            lax.fori_loop(0, seq // qblk, tile, jnp.zeros((1, qblk), f32))

    kk = k_ref[...].astype(bf16)
    vt = v_ref[...].T.astype(bf16)
    dh = kk.shape[1]
    scale = dh ** -0.5
    for hq in range(group):
        qh = q_ref[:, hq * dh:(hq + 1) * dh].astype(bf16)
        s = lax.dot_general(kk, qh, NT_DIMS, preferred_element_type=f32) * scale + mask_ref[...]
        m = jnp.max(s, axis=0, keepdims=True)
        p = jnp.exp(s - m)
        den = jnp.sum(p, axis=0, keepdims=True)
        o_t = jnp.dot(vt, p.astype(bf16), preferred_element_type=f32) / den
        o_ref[:, hq * dh:(hq + 1) * dh] = o_t.T


def _dsa_prompt(bqi, small, bq, bkv, n_batch, seq):
    dh = bq.shape[1] // B_HEADS
    group = B_HEADS // B_KV_HEADS
    qblk = LANE
    nqb = seq // qblk
    topk = min(TOPK_MAX, seq // 4)
    kern = functools.partial(_dsa_prompt_kernel, seq=seq, topk=topk, qblk=qblk, group=group)
    return pl.pallas_call(
        kern,
        grid=(n_batch, nqb, B_KV_HEADS),
        in_specs=[pl.BlockSpec((qblk, bqi.shape[1]), lambda b, j, g: (b * nqb + j, 0)),
                  pl.BlockSpec((qblk, small.shape[1]), lambda b, j, g: (b * nqb + j, 0)),
                  pl.BlockSpec((seq, small.shape[1]), lambda b, j, g: (b, 0)),
                  pl.BlockSpec((qblk, group * dh), lambda b, j, g: (b * nqb + j, g)),
                  pl.BlockSpec((seq, dh), lambda b, j, g: (b, g)),
                  pl.BlockSpec((seq, dh), lambda b, j, g: (b, B_KV_HEADS + g))],
        out_specs=pl.BlockSpec((qblk, group * dh), lambda b, j, g: (b * nqb + j, g)),
        out_shape=jax.ShapeDtypeStruct((n_batch * seq, B_HEADS * dh), f32),
        scratch_shapes=[pltpu.VMEM((seq, qblk), f32), pltpu.VMEM((seq, qblk), i32)],
        compiler_params=_params(("parallel", "parallel", "arbitrary")),
        name="dsa_prompt",
    )(bqi, small, small, bq, bkv, bkv)


def _dsa_sample_select_kernel(pt_ref, qi_ref, wsel_ref, kpage_ref, knew_ref, mask_ref, sc_ref, key_ref,
                              *, n_pages, page, topk, n_tok):
    p = pl.program_id(1)
    qi = qi_ref[0].astype(bf16)
    wsel = wsel_ref[0]

    def tile_scores(keys):
        s = lax.dot_general(qi, keys.astype(bf16), NT_DIMS, preferred_element_type=f32)
        return jnp.dot(wsel, jnp.maximum(s, 0.0), precision=HIGHEST, preferred_element_type=f32)

    c0 = pl.multiple_of(p * page, page)
    sc_ref[:, pl.ds(c0, page)] = tile_scores(kpage_ref[0, 0])

    @pl.when(p == n_pages - 1)
    def _():
        past = n_pages * page
        new = tile_scores(knew_ref[0])
        r = lax.broadcasted_iota(i32, new.shape, 0)
        cidx = lax.broadcasted_iota(i32, new.shape, 1)
        sc_ref[:, past:past + LANE] = jnp.where(jnp.logical_and(cidx <= r, cidx < n_tok), new, -jnp.inf)
        sc = sc_ref[...]
        key_ref[...] = _sort_key(sc)
        thr = _kth_largest_key(key_ref, topk, axis=1)
        keys = key_ref[...]
        sel = jnp.logical_and(keys >= thr, sc > -jnp.inf)
        mask_ref[0] = jnp.where(sel, 0.0, NEG_BIG)
        need = topk - jnp.sum((keys > thr).astype(i32), axis=1, keepdims=True)
        n_tie = jnp.sum((keys == thr).astype(i32), axis=1, keepdims=True)

        @pl.when(jnp.max(n_tie - need) > 0)
        def _():
            strict = (lax.broadcasted_iota(i32, (LANE, LANE), 0) < lax.broadcasted_iota(i32, (LANE, LANE), 1))
            strict = jnp.where(strict, 1.0, 0.0).astype(bf16)
            need_f = need.astype(f32)

            def tile(ti, seen):
                c1 = pl.multiple_of(ti * LANE, LANE)
                kt = key_ref[:, pl.ds(c1, LANE)]
                tie = kt == thr
                tie_f = jnp.where(tie, 1.0, 0.0)
                before = seen + jnp.dot(tie_f.astype(bf16), strict, preferred_element_type=f32)
                keep = jnp.logical_or(kt > thr, jnp.logical_and(tie, before < need_f))
                keep = jnp.logical_and(keep, sc_ref[:, pl.ds(c1, LANE)] > -jnp.inf)
                mask_ref[0, :, pl.ds(c1, LANE)] = jnp.where(keep, 0.0, NEG_BIG)
                return seen + jnp.sum(tie_f, axis=1, keepdims=True)

            lax.fori_loop(0, sc_ref.shape[1] // LANE, tile, jnp.zeros((8, 1), f32))


def _dsa_sample_attn_kernel(pt_ref, q_ref, mask_ref, kpage_ref, vpage_ref, knew_ref, vnew_ref, o_ref,
                            qbd_ref, m_ref, l_ref, acc_ref, *, n_pages, n_tok, heads, dh, kvh):
    p = pl.program_id(1)
    group = heads // kvh
    scale = dh ** -0.5

    @pl.when(p == 0)
    def _():
        q = q_ref[0]
        tiled = jnp.concatenate([q] * kvh, axis=1)
        r = lax.broadcasted_iota(i32, tiled.shape, 0)
        cidx = lax.broadcasted_iota(i32, tiled.shape, 1)
        own = (cidx >> _log2(dh)) == ((r & (heads - 1)) >> _log2(group))
        qbd_ref[...] = jnp.where(own, tiled, 0.0).astype(bf16)
        m_ref[...] = jnp.full_like(m_ref, NEG_BIG)
        l_ref[...] = jnp.zeros_like(l_ref)
        acc_ref[...] = jnp.zeros_like(acc_ref)

    def attend(keys, vals):
        mask8 = mask_ref[0]
        mask = jnp.concatenate([jnp.broadcast_to(mask8[t:t + 1], (heads, mask8.shape[1]))
                                for t in range(n_tok)], axis=0)
        s = lax.dot_general(qbd_ref[...], keys.astype(bf16), NT_DIMS, preferred_element_type=f32)
        s = s * scale + mask
        m_old = m_ref[...]
        m_new = jnp.maximum(m_old, jnp.max(s, axis=-1, keepdims=True))
        pr = jnp.where(mask == 0.0, jnp.exp(s - m_new), 0.0)
        alpha = jnp.exp(m_old - m_new)
        l_ref[...] = alpha * l_ref[...] + jnp.sum(pr, axis=-1, keepdims=True)
        acc_ref[...] = alpha * acc_ref[...] + jnp.dot(pr.astype(bf16), vals.astype(bf16),
                                                      preferred_element_type=f32)
        m_ref[...] = m_new

    @pl.when(p < n_pages)
    def _():
        attend(kpage_ref[0, 0], vpage_ref[0, 0])

    @pl.when(p == n_pages)
    def _():
        attend(knew_ref[0], vnew_ref[0])
        o = acc_ref[...] / l_ref[...]
        r = lax.broadcasted_iota(i32, (o.shape[0], dh), 0)
        out = jnp.zeros((o.shape[0], dh), f32)
        for gg in range(kvh):
            out = jnp.where(((r & (heads - 1)) >> _log2(group)) == gg, o[:, gg * dh:(gg + 1) * dh], out)
        o_ref[0] = out


def _dsa_sample(qi, wsel, q, knew, kv_new_k, kv_new_v, cache_k, cache_v, cache_idx_k, page_table, n_tok):
    n_seq, n_pages = page_table.shape
    n_pool, page = cache_k.shape[0], cache_k.shape[1]
    dh = cache_k.shape[3]
    kvd = cache_k.shape[2] * dh
    assert page == LANE
    rows_i = qi.shape[1]
    rows = q.shape[1]
    kp = n_pages * page + LANE
    topk = min(TOPK_MAX, (n_pages * page + n_tok) // 4)
    ck = cache_k.reshape(n_pool, 1, page, kvd)
    cv = cache_v.reshape(n_pool, 1, page, kvd)
    cik = cache_idx_k.reshape(n_pool, 1, page, IDX_DIM)

    mask = pl.pallas_call(
        functools.partial(_dsa_sample_select_kernel, n_pages=n_pages, page=page, topk=topk, n_tok=n_tok),
        grid_spec=pltpu.PrefetchScalarGridSpec(
            num_scalar_prefetch=1,
            grid=(n_seq, n_pages),
            in_specs=[pl.BlockSpec((1, rows_i, IDX_DIM), lambda b, p, pt: (b, 0, 0)),
                      pl.BlockSpec((1, 8, rows_i), lambda b, p, pt: (b, 0, 0)),
                      pl.BlockSpec((1, 1, page, IDX_DIM), lambda b, p, pt: (pt[b, p], 0, 0, 0)),
                      pl.BlockSpec((1, LANE, IDX_DIM), lambda b, p, pt: (b, 0, 0))],
            out_specs=pl.BlockSpec((1, 8, kp), lambda b, p, pt: (b, 0, 0)),
            scratch_shapes=[pltpu.VMEM((8, kp), f32), pltpu.VMEM((8, kp), i32)]),
        out_shape=jax.ShapeDtypeStruct((n_seq, 8, kp), f32),
        compiler_params=_params(("parallel", "arbitrary")),
        name="dsa_sample_select",
    )(page_table, qi, wsel, cik, knew)

    def page_idx(b, p, pt):
        return (pt[b, jnp.minimum(p, n_pages - 1)], 0, 0, 0)

    return pl.pallas_call(
        functools.partial(_dsa_sample_attn_kernel, n_pages=n_pages, n_tok=n_tok, heads=B_HEADS, dh=dh,
                          kvh=B_KV_HEADS),
        grid_spec=pltpu.PrefetchScalarGridSpec(
            num_scalar_prefetch=1,
            grid=(n_seq, n_pages + 1),
            in_specs=[pl.BlockSpec((1, rows, dh), lambda b, p, pt: (b, 0, 0)),
                      pl.BlockSpec((1, 8, page), lambda b, p, pt: (b, 0, p)),
                      pl.BlockSpec((1, 1, page, kvd), page_idx),
                      pl.BlockSpec((1, 1, page, kvd), page_idx),
                      pl.BlockSpec((1, LANE, kvd), lambda b, p, pt: (b, 0, 0)),
                      pl.BlockSpec((1, LANE, kvd), lambda b, p, pt: (b, 0, 0))],
            out_specs=pl.BlockSpec((1, rows, dh), lambda b, p, pt: (b, 0, 0)),
            scratch_shapes=[pltpu.VMEM((rows, kvd), bf16), pltpu.VMEM((rows, 1), f32),
                            pltpu.VMEM((rows, 1), f32), pltpu.VMEM((rows, kvd), f32)]),
        out_shape=jax.ShapeDtypeStruct((n_seq, rows, dh), f32),
        compiler_params=_params(("parallel", "arbitrary")),
        name="dsa_sample_attn",
    )(page_table, q, mask, ck, cv, kv_new_k, kv_new_v)


def _merge_kernel(ga_ref, gb_ref, oa_ref, ob_ref, o_ref):
    o_ref[...] = (_sigmoid(ga_ref[...]) * oa_ref[...] + _sigmoid(gb_ref[...]) * ob_ref[...]).astype(o_ref.dtype)


def _merge(gates, oa, ob):
    n, d = oa.shape
    tn = _pick_tile(d, 1024, 128)
    nj = d // tn
    return pl.pallas_call(
        _merge_kernel,
        grid=(n // ROW_BLOCK, nj),
        in_specs=[pl.BlockSpec((ROW_BLOCK, tn), lambda i, j: (i, j)),
                  pl.BlockSpec((ROW_BLOCK, tn), lambda i, j: (i, nj + j)),
                  pl.BlockSpec((ROW_BLOCK, tn), lambda i, j: (i, j)),
                  pl.BlockSpec((ROW_BLOCK, tn), lambda i, j: (i, j))],
        out_specs=pl.BlockSpec((ROW_BLOCK, tn), lambda i, j: (i, j)),
        out_shape=jax.ShapeDtypeStruct((n, d), bf16),
        compiler_params=_params(("parallel", "parallel")),
        name="merge_branches",
    )(gates, gates, oa, ob)


def _router_kernel(x_ref, g_ref, w_ref, b_ref, route_ref, cnt_ref, carry_ref):
    i = pl.program_id(0)

    @pl.when(i == 0)
    def _():
        carry_ref[...] = jnp.zeros_like(carry_ref)

    x = x_ref[...]
    ms = jnp.mean(x * x, axis=-1, keepdims=True)
    h = x * lax.rsqrt(ms + NORM_EPS) * g_ref[...]
    logits = jnp.dot(h, w_ref[...], precision=HIGHEST, preferred_element_type=f32) + b_ref[...]
    lane = lax.broadcasted_iota(i32, logits.shape, 1)
    big = jnp.int32(1 << 20)
    ng, epg = N_GROUPS, EXPERTS_PER_GROUP

    def first_max(vals):
        mx = jnp.max(vals, axis=-1, keepdims=True)
        idx = jnp.min(jnp.where(vals == mx, lane, big), axis=-1, keepdims=True)
        return mx, idx

    lg = jnp.where(lane < ng, logits, -jnp.inf)
    mg, g_sel = first_max(lg)
    p_g = 1.0 / jnp.sum(jnp.exp(lg - mg), axis=-1, keepdims=True)
    lo = ng + g_sel * epg
    le = jnp.where(jnp.logical_and(lane >= lo, lane < lo + epg), logits, -jnp.inf)
    m1, i1 = first_max(le)
    se = jnp.sum(jnp.exp(le - m1), axis=-1, keepdims=True)
    m2, i2 = first_max(jnp.where(lane == i1, -jnp.inf, le))
    p1 = 1.0 / se
    p2 = jnp.exp(m2 - m1) / se
    w1 = p_g * p1 / (p1 + p2)
    w2 = p_g * p2 / (p1 + p2)

    hot = jnp.logical_or(lane == i1, lane == i2)
    hot_b = jnp.where(hot, 1.0, 0.0).astype(bf16)
    n = x.shape[0]
    strict = (lax.broadcasted_iota(i32, (n, n), 0) > lax.broadcasted_iota(i32, (n, n), 1))
    before = jnp.dot(jnp.where(strict, 1.0, 0.0).astype(bf16), hot_b, preferred_element_type=f32)
    before = before + carry_ref[...]
    r1 = jnp.sum(jnp.where(lane == i1, before, 0.0), axis=-1, keepdims=True)
    r2 = jnp.sum(jnp.where(lane == i2, before, 0.0), axis=-1, keepdims=True)
    carry_ref[...] = carry_ref[...] + jnp.sum(jnp.where(hot, 1.0, 0.0), axis=0, keepdims=True)

    out = jnp.where(lane == 0, (i1 - ng).astype(f32), 0.0)
    out = jnp.where(lane == 1, (i2 - ng).astype(f32), out)
    out = jnp.where(lane == 2, w1, out)
    out = jnp.where(lane == 3, w2, out)
    out = jnp.where(lane == 4, r1, out)
    out = jnp.where(lane == 5, r2, out)
    route_ref[...] = out
    cnt_ref[...] = carry_ref[...]


def _router(x1, g, w_r, b_r):
    n, d = x1.shape
    return pl.pallas_call(
        _router_kernel,
        grid=(n // ROW_BLOCK,),
        in_specs=[pl.BlockSpec((ROW_BLOCK, d), lambda i: (i, 0)),
                  pl.BlockSpec((1, d), lambda i: (0, 0)),
                  pl.BlockSpec((d, LANE), lambda i: (0, 0)),
                  pl.BlockSpec((1, LANE), lambda i: (0, 0))],
        out_specs=[pl.BlockSpec((ROW_BLOCK, LANE), lambda i: (i, 0)),
                   pl.BlockSpec((1, LANE), lambda i: (0, 0))],
        out_shape=[jax.ShapeDtypeStruct((n, LANE), f32), jax.ShapeDtypeStruct((1, LANE), f32)],
        scratch_shapes=[pltpu.VMEM((1, LANE), f32)],
        compiler_params=_params(("arbitrary",)),
        name="moe_router",
    )(x1, g.reshape(1, d), w_r, b_r)


def _row_copy(src_hbm, dst_vmem, sem, src_row, dst_row):
    return pltpu.make_async_copy(src_hbm.at[pl.ds(src_row, 1), :], dst_vmem.at[pl.ds(dst_row, 1), :], sem)


def _dispatch_kernel(tok_ref, x_hbm, g_ref, o_ref, buf_ref, sem):
    base = pl.program_id(0) * MOE_BLOCK

    def start(r, c):
        _row_copy(x_hbm, buf_ref, sem, tok_ref[base + r], r).start()
        return c

    def wait(r, c):
        _row_copy(x_hbm, buf_ref, sem, 0, r).wait()
        return c

    lax.fori_loop(0, MOE_BLOCK, start, 0)
    lax.fori_loop(0, MOE_BLOCK, wait, 0)
    x = buf_ref[...]
    ms = jnp.mean(x * x, axis=-1, keepdims=True)
    o_ref[...] = (x * lax.rsqrt(ms + NORM_EPS) * g_ref[...]).astype(o_ref.dtype)


def _dispatch(row_tok, x1, g):
    n, d = x1.shape
    rows = row_tok.shape[0]
    return pl.pallas_call(
        _dispatch_kernel,
        grid_spec=pltpu.PrefetchScalarGridSpec(
            num_scalar_prefetch=1,
            grid=(rows // MOE_BLOCK,),
            in_specs=[pl.BlockSpec(memory_space=pl.ANY),
                      pl.BlockSpec((1, d), lambda i, tok: (0, 0))],
            out_specs=pl.BlockSpec((MOE_BLOCK, d), lambda i, tok: (i, 0)),
            scratch_shapes=[pltpu.VMEM((MOE_BLOCK, d), f32), pltpu.SemaphoreType.DMA(())]),
        out_shape=jax.ShapeDtypeStruct((rows, d), bf16),
        compiler_params=_params(("arbitrary",)),
        name="moe_dispatch",
    )(row_tok, x1, g.reshape(1, d))


def _expert_ffn_kernel(ie_ref, ir_ref, in_ref, nit_ref, xs_hbm, wg_ref, wu_ref, wd_ref, y_hbm,
                       x_ref, acc_ref, sem_in, sem_out, *, n_ff):
    it = pl.program_id(0)
    ff = pl.program_id(1)
    nblk = in_ref[it]
    row0 = pl.multiple_of(ir_ref[it] * MOE_BLOCK, MOE_BLOCK)
    span = MOE_ITEM_BLOCKS * MOE_BLOCK

    @pl.when(nblk > 0)
    def _():
        @pl.when(ff == 0)
        def _():
            cp = pltpu.make_async_copy(xs_hbm.at[pl.ds(row0, span), :], x_ref, sem_in)
            cp.start()
            cp.wait()

        x = x_ref[...]
        gate = jnp.dot(x, wg_ref[0].astype(bf16), preferred_element_type=f32)
        up = jnp.dot(x, wu_ref[0].astype(bf16), preferred_element_type=f32)
        act = (gate * _sigmoid(gate) * up).astype(bf16)
        y = jnp.dot(act, wd_ref[0].astype(bf16), preferred_element_type=f32)

        @pl.when(ff == 0)
        def _():
            acc_ref[...] = y

        @pl.when(ff > 0)
        def _():
            acc_ref[...] += y

        @pl.when(ff == n_ff - 1)
        def _():
            for sb in range(MOE_ITEM_BLOCKS):
                @pl.when(sb < nblk)
                def _(sb=sb):
                    rs = pl.ds(sb * MOE_BLOCK, MOE_BLOCK)
                    cp = pltpu.make_async_copy(acc_ref.at[rs, :],
                                               y_hbm.at[pl.ds(row0 + sb * MOE_BLOCK, MOE_BLOCK), :], sem_out)
                    cp.start()
                    cp.wait()

    @pl.when(jnp.logical_and(it == pl.num_programs(0) - 1, ff == n_ff - 1))
    def _():
        acc_ref[0:MOE_BLOCK, :] = jnp.zeros((MOE_BLOCK, acc_ref.shape[1]), f32)

        def fill(blk, c):
            cp = pltpu.make_async_copy(acc_ref.at[pl.ds(0, MOE_BLOCK), :],
                                       y_hbm.at[pl.ds(pl.multiple_of(blk * MOE_BLOCK, MOE_BLOCK), MOE_BLOCK), :],
                                       sem_out)
            cp.start()
            cp.wait()
            return c

        lax.fori_loop(nit_ref[1], y_hbm.shape[0] // MOE_BLOCK, fill, 0)


def _expert_ffn(item_e, item_r, item_n, n_items, xs, w_gate, w_up, w_down, max_items):
    rows, d = xs.shape
    n_exp, _, dff = w_gate.shape
    tf = _pick_tile(dff, MOE_FF_TILE, 128)
    n_ff = dff // tf
    span = MOE_ITEM_BLOCKS * MOE_BLOCK

    def f_eff(i, f, nit):
        return jnp.where(i < nit[0], f, n_ff - 1)

    return pl.pallas_call(
        functools.partial(_expert_ffn_kernel, n_ff=n_ff),
        grid_spec=pltpu.PrefetchScalarGridSpec(
            num_scalar_prefetch=4,
            grid=(max_items, n_ff),
            in_specs=[pl.BlockSpec(memory_space=pl.ANY),
                      pl.BlockSpec((1, d, tf), lambda i, f, ie, ir, inn, nit: (ie[i], 0, f_eff(i, f, nit))),
                      pl.BlockSpec((1, d, tf), lambda i, f, ie, ir, inn, nit: (ie[i], 0, f_eff(i, f, nit))),
                      pl.BlockSpec((1, tf, d), lambda i, f, ie, ir, inn, nit: (ie[i], f_eff(i, f, nit), 0))],
            out_specs=pl.BlockSpec(memory_space=pl.ANY),
            scratch_shapes=[pltpu.VMEM((span, d), bf16), pltpu.VMEM((span, d), f32),
                            pltpu.SemaphoreType.DMA(()), pltpu.SemaphoreType.DMA(())]),
        out_shape=jax.ShapeDtypeStruct((rows, d), f32),
        compiler_params=_params(("arbitrary", "arbitrary")),
        name="moe_expert_ffn",
    )(item_e, item_r, item_n, n_items, xs, w_gate, w_up, w_down)


def _combine_kernel(d1_ref, d2_ref, y_hbm, x_ref, route_ref, g_ref, o_ref, buf_ref, sem):
    base = pl.program_id(0) * ROW_BLOCK

    def start(r, c):
        _row_copy(y_hbm, buf_ref.at[0], sem, d1_ref[base + r], r).start()
        _row_copy(y_hbm, buf_ref.at[1], sem, d2_ref[base + r], r).start()
        return c

    def wait(r, c):
        _row_copy(y_hbm, buf_ref.at[0], sem, 0, r).wait()
        _row_copy(y_hbm, buf_ref.at[1], sem, 0, r).wait()
        return c

    lax.fori_loop(0, ROW_BLOCK, start, 0)
    lax.fori_loop(0, ROW_BLOCK, wait, 0)
    route = route_ref[...]
    x = x_ref[...] + route[:, 2:3] * buf_ref[0] + route[:, 3:4] * buf_ref[1]
    ms = jnp.mean(x * x, axis=-1, keepdims=True)
    o_ref[...] = x * lax.rsqrt(ms + NORM_EPS) * g_ref[...]


def _combine(d1, d2, yb, x1, route, g):
    n, d = x1.shape
    return pl.pallas_call(
        _combine_kernel,
        grid_spec=pltpu.PrefetchScalarGridSpec(
            num_scalar_prefetch=2,
            grid=(n // ROW_BLOCK,),
            in_specs=[pl.BlockSpec(memory_space=pl.ANY),
                      pl.BlockSpec((ROW_BLOCK, d), lambda i, a, b: (i, 0)),
                      pl.BlockSpec((ROW_BLOCK, LANE), lambda i, a, b: (i, 0)),
                      pl.BlockSpec((1, d), lambda i, a, b: (0, 0))],
            out_specs=pl.BlockSpec((ROW_BLOCK, d), lambda i, a, b: (i, 0)),
            scratch_shapes=[pltpu.VMEM((2, ROW_BLOCK, d), f32), pltpu.SemaphoreType.DMA(())]),
        out_shape=jax.ShapeDtypeStruct((n, d), f32),
        compiler_params=_params(("arbitrary",)),
        name="moe_combine",
    )(d1, d2, yb, x1, route, g.reshape(1, d))


def _moe(x1, norm_ffn, w_rg, b_rg, w_re, b_re, w_gate, w_up, w_down, norm_final):
    n, d = x1.shape
    n_exp = w_gate.shape[0]
    ng = w_rg.shape[1]
    w_r = jnp.zeros((d, LANE), f32).at[:, :ng].set(w_rg).at[:, ng:ng + n_exp].set(w_re)
    b_r = jnp.zeros((1, LANE), f32).at[0, :ng].set(b_rg).at[0, ng:ng + n_exp].set(b_re)
    route, cnt = _router(x1, norm_ffn, w_r, b_r)

    e1 = route[:, 0].astype(i32)
    e2 = route[:, 1].astype(i32)
    counts = cnt[0, ng:ng + n_exp].astype(i32)
    nblk_e = (counts + MOE_BLOCK - 1) // MOE_BLOCK
    blk_start = jnp.cumsum(nblk_e) - nblk_e
    d1 = blk_start[e1] * MOE_BLOCK + route[:, 4].astype(i32)
    d2 = blk_start[e2] * MOE_BLOCK + route[:, 5].astype(i32)
    max_blocks = -(-2 * n // MOE_BLOCK) + n_exp
    rows = (max_blocks + MOE_ITEM_BLOCKS) * MOE_BLOCK
    tok = jnp.arange(n, dtype=i32)
    row_tok = jnp.zeros((rows,), i32).at[d1].set(tok).at[d2].set(tok)
    items_e = (nblk_e + MOE_ITEM_BLOCKS - 1) // MOE_ITEM_BLOCKS
    item_end = jnp.cumsum(items_e)
    n_items = item_end[-1]
    max_items = -(-max_blocks // MOE_ITEM_BLOCKS) + n_exp
    it = jnp.arange(max_items, dtype=i32)
    it_c = jnp.minimum(it, n_items - 1)
    item_e = jnp.minimum(jnp.searchsorted(item_end, it_c, side="right"), n_exp - 1).astype(i32)
    local = it_c - (item_end - items_e)[item_e]
    item_r = (blk_start[item_e] + local * MOE_ITEM_BLOCKS).astype(i32)
    item_n = jnp.where(it < n_items, jnp.minimum(MOE_ITEM_BLOCKS, nblk_e[item_e] - local * MOE_ITEM_BLOCKS), 0)

    xs = _dispatch(row_tok, x1, norm_ffn)
    meta = jnp.stack([n_items, jnp.sum(nblk_e)]).astype(i32)
    yb = _expert_ffn(item_e, item_r, item_n.astype(i32), meta, xs, w_gate, w_up, w_down, max_items)
    return _combine(d1, d2, yb, x1, route, norm_final)


def kernel(x_prompt, x_sample, cache_k, cache_v, cache_idx_k, state_hgrn, page_table, norm_mix, w_in,
           lb_logits, norm_hgrn_out, w_out, norm_ffn, router_group, router_group_bias, router_expert,
           router_expert_bias, expert_w_gate, expert_w_up, expert_w_down, norm_final):
    n_b, seq, d = x_prompt.shape
    n_s, n_tok, _ = x_sample.shape
    depth = norm_mix.shape[0]
    assert depth == 1
    kd = A_HEADS * A_EXPAND
    dh = d // B_HEADS
    kvd = B_KV_HEADS * dh
    np_rows = n_b * seq
    ns_rows = n_s * n_tok
    n_all = np_rows + ns_rows
    n_pad = -(-n_all // ROW_BLOCK) * ROW_BLOCK
    assert np_rows % ROW_BLOCK == 0

    lower = jnp.cumsum(jax.nn.softmax(lb_logits.astype(f32), axis=0), axis=0)[0]
    x_all = jnp.concatenate([x_prompt.reshape(np_rows, d), x_sample.reshape(ns_rows, d),
                             jnp.zeros((n_pad - n_all, d), f32)], axis=0)

    w = w_in[0]
    o_bq = 4 * kd
    o_bk = o_bq + d
    o_bqi = o_bk + 2 * kvd
    o_sm = o_bqi + IDX_HEADS * IDX_DIM
    o_g = o_sm + IDX_DIM + IDX_HEADS
    small_w = 2 * LANE
    h = _rmsnorm_bf16(x_all, norm_mix[0])
    z_a = _matmul(h, w[:, :o_bq].astype(bf16), name="proj_hgrn")
    z_bq = _matmul(h, w[:, o_bq:o_bk].astype(bf16), name="proj_q")
    z_kv = _matmul(h, w[:, o_bk:o_bqi].astype(bf16), name="proj_kv")
    z_qi = _matmul(h, w[:, o_bqi:o_sm].astype(bf16), name="proj_qidx")
    w_small = jnp.zeros((d, small_w), bf16).at[:, :o_g - o_sm].set(w[:, o_sm:o_g].astype(bf16))
    z_sm = _matmul(h, w_small, name="proj_small")
    z_g = _matmul(h, w[:, o_g:].astype(bf16), name="proj_gates")

    oa_p, s_p = _hgrn(z_a, lower, norm_hgrn_out[0], n_b, seq, min(HGRN_ROWS, seq), min(HGRN_CHUNK, seq), HGRN_CHUNK)
    zs = z_a[np_rows:n_all].reshape(n_s, n_tok, 4 * kd)
    sp = HGRN_SAMPLE_ROWS
    zs = jnp.concatenate([zs, jnp.zeros((n_s, sp - n_tok, 4 * kd), f32)], axis=1).reshape(n_s * sp, 4 * kd)
    oa_sp, s_s = _hgrn(zs, lower, norm_hgrn_out[0], n_s, sp, sp, sp, n_tok, s0=state_hgrn[0])
    oa_s = oa_sp.reshape(n_s, sp, d)[:, :n_tok].reshape(ns_rows, d)

    ob_p = _dsa_prompt(z_qi, z_sm, z_bq, z_kv, n_b, seq)
    sm_s = z_sm[np_rows:n_all]
    w_s = sm_s[:, IDX_DIM:IDX_DIM + IDX_HEADS].reshape(n_s, n_tok, IDX_HEADS) * (IDX_HEADS ** -0.5 * IDX_DIM ** -0.5)
    eye = jnp.eye(8, n_tok, dtype=f32)
    wsel = (eye[None, :, :, None] * w_s[:, None, :, :]).reshape(n_s, 8, n_tok * IDX_HEADS)
    qi_s = z_qi[np_rows:n_all].reshape(n_s, n_tok * IDX_HEADS, IDX_DIM)
    q_s = z_bq[np_rows:n_all].reshape(n_s, n_tok * B_HEADS, dh)

    def pad_new(a):
        a = a.reshape(n_s, n_tok, -1)
        return jnp.concatenate([a, jnp.zeros((n_s, LANE - n_tok, a.shape[2]), f32)], axis=1)

    kv_s = z_kv[np_rows:n_all]
    ob_s = _dsa_sample(qi_s, wsel, q_s, pad_new(sm_s[:, :IDX_DIM]), pad_new(kv_s[:, :kvd]), pad_new(kv_s[:, kvd:]),
                       cache_k[0], cache_v[0], cache_idx_k[0], page_table, n_tok)
    ob_s = ob_s.reshape(ns_rows, d)

    pad_rows = jnp.zeros((n_pad - n_all, d), f32)
    oa = jnp.concatenate([oa_p, oa_s, pad_rows], axis=0)
    ob = jnp.concatenate([ob_p, ob_s, pad_rows], axis=0)
    m = _merge(z_g, oa, ob)
    x1 = _matmul(m, w_out[0].astype(bf16), res=x_all, name="out_proj")

    y = _moe(x1, norm_ffn[0], router_group[0], router_group_bias[0], router_expert[0], router_expert_bias[0],
             expert_w_gate[0], expert_w_up[0], expert_w_down[0], norm_final)

    kp = z_kv[:np_rows]
    ks = z_kv[np_rows:n_all]
    return (y[:np_rows].reshape(n_b, seq, d),
            y[np_rows:n_all].reshape(n_s, n_tok, d),
            kp[:, :kvd].reshape(1, n_b, seq, B_KV_HEADS, dh),
            kp[:, kvd:].reshape(1, n_b, seq, B_KV_HEADS, dh),
            z_sm[:np_rows, :IDX_DIM].reshape(1, n_b, seq, IDX_DIM),
            s_p[None],
            ks[:, :kvd].reshape(1, n_s, n_tok, B_KV_HEADS, dh),
            ks[:, kvd:].reshape(1, n_s, n_tok, B_KV_HEADS, dh),
            z_sm[np_rows:n_all, :IDX_DIM].reshape(1, n_s, n_tok, IDX_DIM),
            s_s[None])
```

```python
import functools
import math

import jax
import jax.numpy as jnp
import numpy as np
from jax import lax
from jax.experimental import pallas as pl
from jax.experimental.pallas import tpu as pltpu

A_HEADS = 32
A_EXPAND = 128
B_HEADS = 32
B_KV_HEADS = 8
IDX_HEADS = 32
IDX_DIM = 128
TOPK_MAX = 256
N_GROUPS = 8
EXPERTS_PER_GROUP = 8
NORM_EPS = 1e-6

LANE = 128
ROW_BLOCK = 128
MOE_BLOCK = 128
MOE_ITEM_BLOCKS = 4
MOE_FF_TILE = 256
HGRN_CHUNK = 64
HGRN_ROWS = 512
HGRN_SAMPLE_ROWS = 16
HGRN_SAMPLE_HEADS = 8
HGRN_PROMPT_HEADS = 2
HGRN_SAFE_DECAY = 60.0
SAMPLE_PAGES_PER_STEP = 8
NEG_BIG = -1e30
LOG2E = 1.4426950408889634
VMEM_LIMIT = 56 * 1024 * 1024

f32 = jnp.float32
bf16 = jnp.bfloat16
i32 = jnp.int32
HIGHEST = lax.Precision.HIGHEST
NT_DIMS = (((1,), (1,)), ((), ()))
TN_DIMS = (((0,), (0,)), ((), ()))


def _pick_tile(n, cap, align):
    t = (min(cap, n) // align) * align
    while t > align and n % t:
        t -= align
    assert t > 0 and n % t == 0, (n, cap, align)
    return t


def _params(sem, limit=VMEM_LIMIT):
    return pltpu.CompilerParams(dimension_semantics=sem, vmem_limit_bytes=limit)


def _sigmoid(x):
    return 1.0 / (1.0 + jnp.exp(-x))


def _log2(n):
    s = int(math.log2(n))
    assert 1 << s == n, n
    return s


def _rmsnorm_kernel(x_ref, g_ref, o_ref):
    x = x_ref[...]
    ms = jnp.mean(x * x, axis=-1, keepdims=True)
    o_ref[...] = (x * lax.rsqrt(ms + NORM_EPS) * g_ref[...]).astype(o_ref.dtype)


def _rmsnorm_bf16(x, g):
    n, d = x.shape
    return pl.pallas_call(
        _rmsnorm_kernel,
        grid=(n // ROW_BLOCK,),
        in_specs=[pl.BlockSpec((ROW_BLOCK, d), lambda i: (i, 0)),
                  pl.BlockSpec((1, d), lambda i: (0, 0))],
        out_specs=pl.BlockSpec((ROW_BLOCK, d), lambda i: (i, 0)),
        out_shape=jax.ShapeDtypeStruct((n, d), bf16),
        compiler_params=_params(("parallel",)),
        name="rmsnorm_bf16",
    )(x, g.reshape(1, d))


def _mm_kernel(a_ref, b_ref, o_ref):
    o_ref[...] = jnp.dot(a_ref[...], b_ref[...], preferred_element_type=f32)


def _mm_res_kernel(a_ref, b_ref, r_ref, o_ref):
    o_ref[...] = r_ref[...] + jnp.dot(a_ref[...], b_ref[...], preferred_element_type=f32)


def _matmul(a, b, res=None, name="matmul"):
    m, k = a.shape
    n = b.shape[1]
    tm = _pick_tile(m, 640, 128)
    tn = _pick_tile(n, 1024, 128)
    in_specs = [pl.BlockSpec((tm, k), lambda j, i: (i, 0)),
                pl.BlockSpec((k, tn), lambda j, i: (0, j))]
    args = [a, b]
    kern = _mm_kernel
    if res is not None:
        in_specs.append(pl.BlockSpec((tm, tn), lambda j, i: (i, j)))
        args.append(res)
        kern = _mm_res_kernel
    return pl.pallas_call(
        kern,
        grid=(n // tn, m // tm),
        in_specs=in_specs,
        out_specs=pl.BlockSpec((tm, tn), lambda j, i: (i, j)),
        out_shape=jax.ShapeDtypeStruct((m, n), f32),
        compiler_params=_params(("parallel", "parallel")),
        name=name,
    )(*args)


def _hgrn_static(c):
    levels = _log2(c)
    t = np.arange(c)
    cmat = np.zeros((2 * c, c), np.float32)
    cmat[:c] = t[None, :] <= t[:, None]
    cmat[c:] = 1.0
    lmat = np.zeros((levels * c, c), np.float32)
    lv = np.full((c, c), -1, np.int32)
    for l in range(levels):
        blk = c >> l
        half = blk // 2
        mid = (t // blk) * blk + half
        lmat[l * c:(l + 1) * c] = t[None, :] <= (mid - 1)[:, None]
        same = (t[:, None] // blk) == (t[None, :] // blk)
        up = ((t // half) % 2 == 1)[:, None]
        lo = ((t // half) % 2 == 0)[None, :]
        lv[same & up & lo] = l
    lv[t, t] = levels
    return cmat, lmat, lv, levels


def _split3(x):
    hi = x.astype(bf16)
    r = x - hi.astype(f32)
    mid = r.astype(bf16)
    lo = (r - mid.astype(f32)).astype(bf16)
    return jnp.concatenate([hi, mid, lo], axis=1)


def _sum3(y):
    w = y.shape[1] // 3
    return y[:, :w] + y[:, w:2 * w] + y[:, 2 * w:]


def _hgrn_kernel(*refs, rows, chunk, levels, n_valid, heads, has_init):
    if has_init:
        (zq_ref, zf_ref, zi_ref, zg_ref, lb_ref, gain_ref, cmat_ref, lmat_ref, lv_ref, s0_ref,
         oa_ref, sfin_ref, st_ref, a_ref, q_sc, k_sc, b_sc, p_sc) = refs
    else:
        (zq_ref, zf_ref, zi_ref, zg_ref, lb_ref, gain_ref, cmat_ref, lmat_ref, lv_ref,
         oa_ref, sfin_ref, st_ref, a_ref, q_sc, k_sc, b_sc, p_sc) = refs
        s0_ref = None
    tb = pl.program_id(2)
    c = chunk
    nc = rows // c
    dk = A_EXPAND

    @pl.when(tb == 0)
    def _():
        for hh in range(heads):
            if has_init:
                st_ref[hh] = s0_ref[0, hh].T
            else:
                st_ref[hh] = jnp.zeros(st_ref.shape[1:], f32)

    lv = lv_ref[...]
    cmat = cmat_ref[...]
    row_c = lax.broadcasted_iota(i32, (c, dk), 0)

    pre = []
    unsafe = None
    for hh in range(heads):
        ls = slice(hh * dk, (hh + 1) * dk)
        zq = zq_ref[:, ls]
        lb = lb_ref[:, ls]
        f = lb + (1.0 - lb) * _sigmoid(zf_ref[:, ls])
        logf = jnp.log(f)
        k = 1.0 - f
        q = zq * _sigmoid(zq)
        if n_valid < rows:
            live = lax.broadcasted_iota(i32, (rows, dk), 0) < n_valid
            logf = jnp.where(live, logf, 0.0)
            k = jnp.where(live, k, 0.0)
        parts = _split3(logf)
        bs, bls = [], []
        for ci in range(nc):
            y = _sum3(jnp.dot(cmat, parts[ci * c:(ci + 1) * c], preferred_element_type=f32))
            bs.append(y[:c])
            bls.append(y[c:])
        b = jnp.concatenate(bs, axis=0) if nc > 1 else bs[0]
        bl = jnp.concatenate(bls, axis=0) if nc > 1 else bls[0]
        head_unsafe = jnp.min(bl) < -HGRN_SAFE_DECAY
        unsafe = head_unsafe if unsafe is None else jnp.logical_or(unsafe, head_unsafe)
        pre.append(dict(q=q, k=k, b=b, bl=bl, parts=parts, qh_b=(q * jnp.exp(b)).astype(bf16),
                        kd=(k * jnp.exp(bl - b)).astype(bf16), v_b=zi_ref[:, ls].astype(bf16)))

    @pl.when(jnp.logical_not(unsafe))
    def _():
        for hh, hv in enumerate(pre):
            kt = (hv["k"] * jnp.exp(-hv["b"])).astype(bf16)
            for ci in range(nc):
                rs = slice(ci * c, (ci + 1) * c)
                al = lax.dot_general(hv["qh_b"][rs], kt[rs], NT_DIMS, preferred_element_type=f32)
                a_ref[hh * nc + ci] = jnp.where(lv >= 0, al, 0.0)

    @pl.when(unsafe)
    def _():
        for hh, hv in enumerate(pre):
            hr = slice(hh * rows, (hh + 1) * rows)
            q_sc[hr, :] = hv["q"]
            k_sc[hr, :] = hv["k"]
            b_sc[hr, :] = hv["b"]
            p_sc[hr, :] = hv["parts"]

        def safe_chunk(ci, carry):
            r0 = pl.multiple_of(ci * c, c)
            q_c = q_sc[pl.ds(r0, c), :]
            k_c = k_sc[pl.ds(r0, c), :]
            b_c = b_sc[pl.ds(r0, c), :]
            refs_b = _sum3(jnp.dot(lmat_ref[...], p_sc[pl.ds(r0, c), :], preferred_element_type=f32))
            a = jnp.where(lv == levels, jnp.sum(q_c * k_c, axis=-1, keepdims=True), 0.0)
            for l in range(levels):
                e = jnp.exp(-jnp.abs(b_c - refs_b[l * c:(l + 1) * c]))
                upper = ((row_c >> (levels - l - 1)) & 1) == 1
                qt = jnp.where(upper, q_c * e, 0.0).astype(bf16)
                kt = jnp.where(upper, 0.0, k_c * e).astype(bf16)
                al = lax.dot_general(qt, kt, NT_DIMS, preferred_element_type=f32)
                a = jnp.where(lv == l, al, a)
            a_ref[ci] = a
            return carry

        lax.fori_loop(0, heads * nc, safe_chunk, 0)

    for hh, hv in enumerate(pre):
        ls = slice(hh * dk, (hh + 1) * dk)
        gain = gain_ref[:, ls]
        zg = zg_ref[:, ls]
        st = st_ref[hh]
        for ci in range(nc):
            rs = slice(ci * c, (ci + 1) * c)
            o = (jnp.dot(a_ref[hh * nc + ci].astype(bf16), hv["v_b"][rs], preferred_element_type=f32)
                 + lax.dot_general(hv["qh_b"][rs], st.astype(bf16), NT_DIMS, preferred_element_type=f32))
            ms = jnp.mean(o * o, axis=-1, keepdims=True)
            zg_c = zg[rs]
            oa_ref[rs, ls] = o * lax.rsqrt(ms + NORM_EPS) * gain * (zg_c * _sigmoid(zg_c))
            st = (st * jnp.exp(hv["bl"][ci * c:ci * c + 1])
                  + lax.dot_general(hv["v_b"][rs], hv["kd"][rs], TN_DIMS, preferred_element_type=f32))
        st_ref[hh] = st

    @pl.when(tb == pl.num_programs(2) - 1)
    def _():
        for hh in range(heads):
            sfin_ref[0, hh] = st_ref[hh].T


def _hgrn(z_a, lb, gain, n_seq, seq_rows, row_block, chunk, n_valid, heads, s0=None):
    kd = A_HEADS * A_EXPAND
    dv = LANE
    assert z_a.shape[1] == 2 * kd + 2 * A_HEADS * dv and A_EXPAND == LANE and A_HEADS % heads == 0
    cmat, lmat, lv, levels = _hgrn_static(chunk)
    nrb = seq_rows // row_block
    hb = A_HEADS // heads
    wd = heads * LANE

    def zspec(seg):
        return pl.BlockSpec((row_block, wd), lambda b, h, t, seg=seg: (b * nrb + t, seg * hb + h))

    const = lambda b, h, t: (0, 0)
    in_specs = [zspec(0), zspec(1), zspec(2), zspec(3),
                pl.BlockSpec((1, wd), lambda b, h, t: (0, h)),
                pl.BlockSpec((1, wd), lambda b, h, t: (0, h)),
                pl.BlockSpec(cmat.shape, const), pl.BlockSpec(lmat.shape, const), pl.BlockSpec(lv.shape, const)]
    args = [z_a, z_a, z_a, z_a, lb.reshape(1, kd), gain.reshape(1, -1),
            jnp.asarray(cmat, bf16), jnp.asarray(lmat, bf16), jnp.asarray(lv)]
    if s0 is not None:
        in_specs.append(pl.BlockSpec((1, heads, A_EXPAND, dv), lambda b, h, t: (b, h, 0, 0)))
        args.append(s0)
    kern = functools.partial(_hgrn_kernel, rows=row_block, chunk=chunk, levels=levels, n_valid=n_valid,
                             heads=heads, has_init=s0 is not None)
    return pl.pallas_call(
        kern,
        grid=(n_seq, hb, nrb),
        in_specs=in_specs,
        out_specs=[pl.BlockSpec((row_block, wd), lambda b, h, t: (b * nrb + t, h)),
                   pl.BlockSpec((1, heads, A_EXPAND, dv), lambda b, h, t: (b, h, 0, 0))],
        out_shape=[jax.ShapeDtypeStruct((n_seq * seq_rows, A_HEADS * dv), f32),
                   jax.ShapeDtypeStruct((n_seq, A_HEADS, A_EXPAND, dv), f32)],
        scratch_shapes=[pltpu.VMEM((heads, dv, A_EXPAND), f32),
                        pltpu.VMEM((heads * row_block // chunk, chunk, chunk), f32),
                        pltpu.VMEM((heads * row_block, LANE), f32), pltpu.VMEM((heads * row_block, LANE), f32),
                        pltpu.VMEM((heads * row_block, LANE), f32),
                        pltpu.VMEM((heads * row_block, 3 * LANE), bf16)],
        compiler_params=_params(("parallel", "parallel", "arbitrary")),
        name="hgrn2_prompt" if s0 is None else "hgrn2_sample",
    )(*args)


def _sort_key(x):
    b = pltpu.bitcast(x + 0.0, i32)
    return b ^ ((b >> 31) & 0x7FFFFFFF)


def _kth_largest_key(key_ref, k, axis):
    shape = (1, key_ref.shape[1]) if axis == 0 else (key_ref.shape[0], 1)
    int_min = jnp.int32(-2 ** 31)

    def step(i, lo):
        cand = lo ^ lax.shift_left(jnp.int32(1), 31 - i)
        cnt = jnp.sum((key_ref[...] >= cand).astype(i32), axis=axis, keepdims=True)
        return jnp.where(cnt >= k, cand, lo)

    return lax.fori_loop(0, 32, step, jnp.full(shape, int_min, i32))


def _dsa_prompt_kernel(qi_ref, smq_ref, smk_ref, q_ref, k_ref, v_ref, o_ref, mask_ref, key_ref, qs_ref,
                       *, kext, topk, qblk, group, qb0):
    qb = qb0 + pl.program_id(1)
    g = pl.program_id(2)
    n_tiles = kext // qblk

    def causal(r0):
        kpos = r0 + lax.broadcasted_iota(i32, (qblk, qblk), 0)
        qpos = qb * qblk + lax.broadcasted_iota(i32, (qblk, qblk), 1)
        return kpos <= qpos

    @pl.when(g == 0)
    def _():
        qs_ref[...] = qi_ref[...].astype(bf16)
        w_t = smq_ref[...].T * (IDX_HEADS ** -0.5 * IDX_DIM ** -0.5)

        def score_tile(ti, carry):
            r0 = pl.multiple_of(ti * qblk, qblk)
            kid = smk_ref[pl.ds(r0, qblk), 0:IDX_DIM].astype(bf16)
            acc = jnp.zeros((qblk, qblk), f32)
            for h in range(IDX_HEADS):
                s = lax.dot_general(kid, qs_ref[:, h * IDX_DIM:(h + 1) * IDX_DIM], NT_DIMS,
                                    preferred_element_type=f32)
                acc = acc + jnp.maximum(s, 0.0) * w_t[IDX_DIM + h:IDX_DIM + h + 1, :]
            key_ref[pl.ds(r0, qblk), :] = _sort_key(jnp.where(causal(r0), acc, -jnp.inf))
            return carry

        lax.fori_loop(0, n_tiles, score_tile, 0)
        thr = _kth_largest_key(key_ref, topk, axis=0)
        keys = key_ref[...]
        kpos = lax.broadcasted_iota(i32, (kext, qblk), 0)
        qpos = qb * qblk + lax.broadcasted_iota(i32, (kext, qblk), 1)
        mask_ref[...] = jnp.where(jnp.logical_and(kpos <= qpos, keys >= thr), 0.0, -jnp.inf)
        need = topk - jnp.sum((keys > thr).astype(i32), axis=0, keepdims=True)
        n_tie = jnp.sum((keys == thr).astype(i32), axis=0, keepdims=True)

        @pl.when(jnp.max(n_tie - need) > 0)
        def _():
            strict = (lax.broadcasted_iota(i32, (qblk, qblk), 0) > lax.broadcasted_iota(i32, (qblk, qblk), 1))
            strict = jnp.where(strict, 1.0, 0.0).astype(bf16)
            need_f = need.astype(f32)

            def tile(ti, seen):
                r0 = pl.multiple_of(ti * qblk, qblk)
                kt = key_ref[pl.ds(r0, qblk), :]
                tie = kt == thr
                tie_f = jnp.where(tie, 1.0, 0.0)
                before = seen + jnp.dot(strict, tie_f.astype(bf16), preferred_element_type=f32)
                keep = jnp.logical_or(kt > thr, jnp.logical_and(tie, before < need_f))
                keep = jnp.logical_and(keep, causal(r0))
                mask_ref[pl.ds(r0, qblk), :] = jnp.where(keep, 0.0, -jnp.inf)
                return seen + jnp.sum(tie_f, axis=0, keepdims=True)

            lax.fori_loop(0, n_tiles, tile, jnp.zeros((1, qblk), f32))

    kk = k_ref[...].astype(bf16)
    vt = v_ref[...].T.astype(bf16)
    dh = kk.shape[1]
    for hq in range(group):
        qh = (q_ref[:, hq * dh:(hq + 1) * dh] * (dh ** -0.5 * LOG2E)).astype(bf16)
        s = lax.dot_general(kk, qh, NT_DIMS, preferred_element_type=f32) + mask_ref[...]
        m = jnp.max(s, axis=0, keepdims=True)
        p = jnp.exp2(s - m)
        den = jnp.sum(p, axis=0, keepdims=True)
        o_t = jnp.dot(vt, p.astype(bf16), preferred_element_type=f32) / den
        o_ref[:, hq * dh:(hq + 1) * dh] = o_t.T


def _dsa_prompt(bqi, small, bq, bkv, n_batch, seq):
    dh = bq.shape[1] // B_HEADS
    group = B_HEADS // B_KV_HEADS
    qblk = LANE
    nqb = seq // qblk
    topk = min(TOPK_MAX, seq // 4)
    parts = []
    qb0 = 0
    kext = 2 * qblk
    while qb0 < nqb:
        kext = min(kext, seq)
        assert seq % kext == 0
        count = kext // qblk - qb0
        nkb = seq // kext
        kern = functools.partial(_dsa_prompt_kernel, kext=kext, topk=topk, qblk=qblk, group=group, qb0=qb0)
        part = pl.pallas_call(
            kern,
            grid=(n_batch, count, B_KV_HEADS),
            in_specs=[pl.BlockSpec((qblk, bqi.shape[1]), lambda b, j, g, q0=qb0: (b * nqb + q0 + j, 0)),
                      pl.BlockSpec((qblk, small.shape[1]), lambda b, j, g, q0=qb0: (b * nqb + q0 + j, 0)),
                      pl.BlockSpec((kext, small.shape[1]), lambda b, j, g, nk=nkb: (b * nk, 0)),
                      pl.BlockSpec((qblk, group * dh), lambda b, j, g, q0=qb0: (b * nqb + q0 + j, g)),
                      pl.BlockSpec((kext, dh), lambda b, j, g, nk=nkb: (b * nk, g)),
                      pl.BlockSpec((kext, dh), lambda b, j, g, nk=nkb: (b * nk, B_KV_HEADS + g))],
            out_specs=pl.BlockSpec((qblk, group * dh), lambda b, j, g, cnt=count: (b * cnt + j, g)),
            out_shape=jax.ShapeDtypeStruct((n_batch * count * qblk, B_HEADS * dh), f32),
            scratch_shapes=[pltpu.VMEM((kext, qblk), f32), pltpu.VMEM((kext, qblk), i32),
                            pltpu.VMEM((qblk, bqi.shape[1]), bf16)],
            compiler_params=_params(("parallel", "parallel", "arbitrary")),
            name=f"dsa_prompt_k{kext}",
        )(bqi, small, small, bq, bkv, bkv)
        parts.append(part.reshape(n_batch, count * qblk, B_HEADS * dh))
        qb0 += count
        kext *= 2
    return jnp.concatenate(parts, axis=1).reshape(n_batch * seq, B_HEADS * dh)


def _dsa_sample_select_kernel(pt_ref, qi_ref, wsel_ref, knew_ref, cik_hbm, mask_ref,
                              slab_ref, sc_ref, key_ref, sem, *, n_pages, page, topk, n_tok):
    b = pl.program_id(0)
    n_seq = pl.num_programs(0)
    past = n_pages * page

    def page_copy(seq, slot, j):
        return pltpu.make_async_copy(cik_hbm.at[0, pt_ref[seq, j]],
                                     slab_ref.at[slot, pl.ds(pl.multiple_of(j * page, page), page), :],
                                     sem.at[slot])

    def start(seq, slot):
        lax.fori_loop(0, n_pages, lambda j, c: (page_copy(seq, slot, j).start(), c)[1], 0)

    def wait(seq, slot):
        lax.fori_loop(0, n_pages, lambda j, c: (page_copy(seq, slot, j).wait(), c)[1], 0)

    @pl.when(b == 0)
    def _():
        start(0, 0)

    slot = b % 2

    @pl.when(b + 1 < n_seq)
    def _():
        start(b + 1, 1 - slot)

    wait(b, slot)
    qi = qi_ref[0].astype(bf16)
    wsel = wsel_ref[0]

    def tile_scores(keys):
        s = lax.dot_general(qi, keys.astype(bf16), NT_DIMS, preferred_element_type=f32)
        return jnp.dot(wsel, jnp.maximum(s, 0.0), precision=HIGHEST, preferred_element_type=f32)

    sc_ref[:, 0:past] = tile_scores(slab_ref[slot])
    new = tile_scores(knew_ref[0])
    r = lax.broadcasted_iota(i32, new.shape, 0)
    cidx = lax.broadcasted_iota(i32, new.shape, 1)
    sc_ref[:, past:past + LANE] = jnp.where(jnp.logical_and(cidx <= r, cidx < n_tok), new, -jnp.inf)
    sc = sc_ref[...]
    key_ref[...] = _sort_key(sc)
    thr = _kth_largest_key(key_ref, topk, axis=1)
    keys = key_ref[...]
    sel = jnp.logical_and(keys >= thr, sc > -jnp.inf)
    mask_ref[0] = jnp.where(sel, 0.0, NEG_BIG)
    need = topk - jnp.sum((keys > thr).astype(i32), axis=1, keepdims=True)
    n_tie = jnp.sum((keys == thr).astype(i32), axis=1, keepdims=True)

    @pl.when(jnp.max(n_tie - need) > 0)
    def _():
        strict = (lax.broadcasted_iota(i32, (LANE, LANE), 0) < lax.broadcasted_iota(i32, (LANE, LANE), 1))
        strict = jnp.where(strict, 1.0, 0.0).astype(bf16)
        need_f = need.astype(f32)

        def tile(ti, seen):
            c1 = pl.multiple_of(ti * LANE, LANE)
            kt = key_ref[:, pl.ds(c1, LANE)]
            tie = kt == thr
            tie_f = jnp.where(tie, 1.0, 0.0)
            before = seen + jnp.dot(tie_f.astype(bf16), strict, preferred_element_type=f32)
            keep = jnp.logical_or(kt > thr, jnp.logical_and(tie, before < need_f))
            keep = jnp.logical_and(keep, sc_ref[:, pl.ds(c1, LANE)] > -jnp.inf)
            mask_ref[0, :, pl.ds(c1, LANE)] = jnp.where(keep, 0.0, NEG_BIG)
            return seen + jnp.sum(tie_f, axis=1, keepdims=True)

        lax.fori_loop(0, sc_ref.shape[1] // LANE, tile, jnp.zeros((8, 1), f32))


def _dsa_sample_attn_kernel(pt_ref, q_ref, mask_ref, masknew_ref, knew_ref, vnew_ref, ck_hbm, cv_hbm, o_ref,
                            kslab, vslab, m_ref, l_ref, acc_ref, sem,
                            *, n_chunks, ch, n_tok, kvh, group, dh, page):
    b = pl.program_id(0)
    c = pl.program_id(1)
    t = b * n_chunks + c
    total = pl.num_programs(0) * n_chunks
    scale = dh ** -0.5

    def copies(seq, chunk, slot, j):
        pid = pt_ref[seq, chunk * ch + j]
        return (pltpu.make_async_copy(ck_hbm.at[0, pid], kslab.at[slot, j], sem.at[slot, 0]),
                pltpu.make_async_copy(cv_hbm.at[0, pid], vslab.at[slot, j], sem.at[slot, 1]))

    def start(seq, chunk, slot):
        for j in range(ch):
            ck, cv = copies(seq, chunk, slot, j)
            ck.start()
            cv.start()

    def wait(seq, chunk, slot):
        for j in range(ch):
            ck, cv = copies(seq, chunk, slot, j)
            ck.wait()
            cv.wait()

    @pl.when(t == 0)
    def _():
        start(0, 0, 0)

    slot = t % 2

    @pl.when(t + 1 < total)
    def _():
        nxt = t + 1
        start(nxt // n_chunks, nxt % n_chunks, 1 - slot)

    wait(b, c, slot)

    @pl.when(c == 0)
    def _():
        m_ref[...] = jnp.full_like(m_ref, NEG_BIG)
        l_ref[...] = jnp.zeros_like(l_ref)
        acc_ref[...] = jnp.zeros_like(acc_ref)

    def attend(g, keys, vals, mask8):
        mask = jnp.concatenate([jnp.broadcast_to(mask8[tk:tk + 1], (group, mask8.shape[1]))
                                for tk in range(n_tok)], axis=0)
        qg = (q_ref[0, g] * scale).astype(bf16)
        s = lax.dot_general(qg, keys, NT_DIMS, preferred_element_type=f32) + mask
        m_old = m_ref[g]
        m_new = jnp.maximum(m_old, jnp.max(s, axis=-1, keepdims=True))
        pr = jnp.where(mask == 0.0, jnp.exp(s - m_new), 0.0)
        alpha = jnp.exp(m_old - m_new)
        l_ref[g] = alpha * l_ref[g] + jnp.sum(pr, axis=-1, keepdims=True)
        acc_ref[g] = alpha * acc_ref[g] + jnp.dot(pr.astype(bf16), vals, preferred_element_type=f32)
        m_ref[g] = m_new

    mask8 = mask_ref[0]
    for g in range(kvh):
        keys = kslab[slot, :, :, g, :].reshape(ch * page, dh).astype(bf16)
        vals = vslab[slot, :, :, g, :].reshape(ch * page, dh).astype(bf16)
        attend(g, keys, vals, mask8)

    @pl.when(c == n_chunks - 1)
    def _():
        for g in range(kvh):
            attend(g, knew_ref[0, g].astype(bf16), vnew_ref[0, g].astype(bf16), masknew_ref[0])
            o_ref[0, g] = acc_ref[g] / l_ref[g]


def _dsa_sample(qi, wsel, q, knew_idx, k_new, v_new, cache_k, cache_v, cache_idx_k, page_table, n_tok):
    n_seq, n_pages = page_table.shape
    page, kvh, dh = cache_k.shape[2], cache_k.shape[3], cache_k.shape[4]
    assert page == LANE
    rows_i = qi.shape[1]
    rows_g = q.shape[2]
    group = rows_g // n_tok
    past = n_pages * page
    kp = past + LANE
    topk = min(TOPK_MAX, (past + n_tok) // 4)
    ch = math.gcd(SAMPLE_PAGES_PER_STEP, n_pages)
    n_chunks = n_pages // ch

    mask = pl.pallas_call(
        functools.partial(_dsa_sample_select_kernel, n_pages=n_pages, page=page, topk=topk, n_tok=n_tok),
        grid_spec=pltpu.PrefetchScalarGridSpec(
            num_scalar_prefetch=1,
            grid=(n_seq,),
            in_specs=[pl.BlockSpec((1, rows_i, IDX_DIM), lambda b, pt: (b, 0, 0)),
                      pl.BlockSpec((1, 8, rows_i), lambda b, pt: (b, 0, 0)),
                      pl.BlockSpec((1, LANE, IDX_DIM), lambda b, pt: (b, 0, 0)),
                      pl.BlockSpec(memory_space=pl.ANY)],
            out_specs=pl.BlockSpec((1, 8, kp), lambda b, pt: (b, 0, 0)),
            scratch_shapes=[pltpu.VMEM((2, past, IDX_DIM), f32), pltpu.VMEM((8, kp), f32),
                            pltpu.VMEM((8, kp), i32), pltpu.SemaphoreType.DMA((2,))]),
        out_shape=jax.ShapeDtypeStruct((n_seq, 8, kp), f32),
        compiler_params=_params(("arbitrary",)),
        name="dsa_sample_select",
    )(page_table, qi, wsel, knew_idx, cache_idx_k)

    return pl.pallas_call(
        functools.partial(_dsa_sample_attn_kernel, n_chunks=n_chunks, ch=ch, n_tok=n_tok, kvh=kvh, group=group,
                          dh=dh, page=page),
        grid_spec=pltpu.PrefetchScalarGridSpec(
            num_scalar_prefetch=1,
            grid=(n_seq, n_chunks),
            in_specs=[pl.BlockSpec((1, kvh, rows_g, dh), lambda b, c, pt: (b, 0, 0, 0)),
                      pl.BlockSpec((1, 8, ch * page), lambda b, c, pt: (b, 0, c)),
                      pl.BlockSpec((1, 8, LANE), lambda b, c, pt: (b, 0, past // LANE)),
                      pl.BlockSpec((1, kvh, LANE, dh), lambda b, c, pt: (b, 0, 0, 0)),
                      pl.BlockSpec((1, kvh, LANE, dh), lambda b, c, pt: (b, 0, 0, 0)),
                      pl.BlockSpec(memory_space=pl.ANY),
                      pl.BlockSpec(memory_space=pl.ANY)],
            out_specs=pl.BlockSpec((1, kvh, rows_g, dh), lambda b, c, pt: (b, 0, 0, 0)),
            scratch_shapes=[pltpu.VMEM((2, ch, page, kvh, dh), f32), pltpu.VMEM((2, ch, page, kvh, dh), f32),
                            pltpu.VMEM((kvh, rows_g, 1), f32), pltpu.VMEM((kvh, rows_g, 1), f32),
                            pltpu.VMEM((kvh, rows_g, dh), f32), pltpu.SemaphoreType.DMA((2, 2))]),
        out_shape=jax.ShapeDtypeStruct((n_seq, kvh, rows_g, dh), f32),
        compiler_params=_params(("arbitrary", "arbitrary")),
        name="dsa_sample_attn",
    )(page_table, q, mask, mask, k_new, v_new, cache_k, cache_v)


def _merge_kernel(ga_ref, gb_ref, oa_ref, ob_ref, o_ref):
    o_ref[...] = (_sigmoid(ga_ref[...]) * oa_ref[...] + _sigmoid(gb_ref[...]) * ob_ref[...]).astype(o_ref.dtype)


def _merge(gates, oa, ob):
    n, d = oa.shape
    tn = _pick_tile(d, 1024, 128)
    nj = d // tn
    return pl.pallas_call(
        _merge_kernel,
        grid=(n // ROW_BLOCK, nj),
        in_specs=[pl.BlockSpec((ROW_BLOCK, tn), lambda i, j: (i, j)),
                  pl.BlockSpec((ROW_BLOCK, tn), lambda i, j: (i, nj + j)),
                  pl.BlockSpec((ROW_BLOCK, tn), lambda i, j: (i, j)),
                  pl.BlockSpec((ROW_BLOCK, tn), lambda i, j: (i, j))],
        out_specs=pl.BlockSpec((ROW_BLOCK, tn), lambda i, j: (i, j)),
        out_shape=jax.ShapeDtypeStruct((n, d), bf16),
        compiler_params=_params(("parallel", "parallel")),
        name="merge_branches",
    )(gates, gates, oa, ob)


def _router_kernel(x_ref, g_ref, w_ref, b_ref, route_ref, cnt_ref, carry_ref):
    i = pl.program_id(0)

    @pl.when(i == 0)
    def _():
        carry_ref[...] = jnp.zeros_like(carry_ref)

    x = x_ref[...]
    ms = jnp.mean(x * x, axis=-1, keepdims=True)
    h = x * lax.rsqrt(ms + NORM_EPS) * g_ref[...]
    logits = jnp.dot(h, w_ref[...], precision=HIGHEST, preferred_element_type=f32) + b_ref[...]
    lane = lax.broadcasted_iota(i32, logits.shape, 1)
    big = jnp.int32(1 << 20)
    ng, epg = N_GROUPS, EXPERTS_PER_GROUP

    def first_max(vals):
        mx = jnp.max(vals, axis=-1, keepdims=True)
        idx = jnp.min(jnp.where(vals == mx, lane, big), axis=-1, keepdims=True)
        return mx, idx

    lg = jnp.where(lane < ng, logits, -jnp.inf)
    mg, g_sel = first_max(lg)
    p_g = 1.0 / jnp.sum(jnp.exp(lg - mg), axis=-1, keepdims=True)
    lo = ng + g_sel * epg
    le = jnp.where(jnp.logical_and(lane >= lo, lane < lo + epg), logits, -jnp.inf)
    m1, i1 = first_max(le)
    se = jnp.sum(jnp.exp(le - m1), axis=-1, keepdims=True)
    m2, i2 = first_max(jnp.where(lane == i1, -jnp.inf, le))
    p1 = 1.0 / se
    p2 = jnp.exp(m2 - m1) / se
    w1 = p_g * p1 / (p1 + p2)
    w2 = p_g * p2 / (p1 + p2)

    hot = jnp.logical_or(lane == i1, lane == i2)
    hot_b = jnp.where(hot, 1.0, 0.0).astype(bf16)
    n = x.shape[0]
    strict = (lax.broadcasted_iota(i32, (n, n), 0) > lax.broadcasted_iota(i32, (n, n), 1))
    before = jnp.dot(jnp.where(strict, 1.0, 0.0).astype(bf16), hot_b, preferred_element_type=f32)
    before = before + carry_ref[...]
    r1 = jnp.sum(jnp.where(lane == i1, before, 0.0), axis=-1, keepdims=True)
    r2 = jnp.sum(jnp.where(lane == i2, before, 0.0), axis=-1, keepdims=True)
    carry_ref[...] = carry_ref[...] + jnp.sum(jnp.where(hot, 1.0, 0.0), axis=0, keepdims=True)

    out = jnp.where(lane == 0, (i1 - ng).astype(f32), 0.0)
    out = jnp.where(lane == 1, (i2 - ng).astype(f32), out)
    out = jnp.where(lane == 2, w1, out)
    out = jnp.where(lane == 3, w2, out)
    out = jnp.where(lane == 4, r1, out)
    out = jnp.where(lane == 5, r2, out)
    route_ref[...] = out
    cnt_ref[...] = carry_ref[...]


def _router(x1, g, w_r, b_r):
    n, d = x1.shape
    return pl.pallas_call(
        _router_kernel,
        grid=(n // ROW_BLOCK,),
        in_specs=[pl.BlockSpec((ROW_BLOCK, d), lambda i: (i, 0)),
                  pl.BlockSpec((1, d), lambda i: (0, 0)),
                  pl.BlockSpec((d, LANE), lambda i: (0, 0)),
                  pl.BlockSpec((1, LANE), lambda i: (0, 0))],
        out_specs=[pl.BlockSpec((ROW_BLOCK, LANE), lambda i: (i, 0)),
                   pl.BlockSpec((1, LANE), lambda i: (0, 0))],
        out_shape=[jax.ShapeDtypeStruct((n, LANE), f32), jax.ShapeDtypeStruct((1, LANE), f32)],
        scratch_shapes=[pltpu.VMEM((1, LANE), f32)],
        compiler_params=_params(("arbitrary",)),
        name="moe_router",
    )(x1, g.reshape(1, d), w_r, b_r)


def _row_copy(src_hbm, dst_vmem, sem, src_row, dst_row):
    return pltpu.make_async_copy(src_hbm.at[pl.ds(src_row, 1), :], dst_vmem.at[pl.ds(dst_row, 1), :], sem)


def _dispatch_kernel(tok_ref, x_hbm, g_ref, o_ref, buf_ref, sem):
    i = pl.program_id(0)
    n = pl.num_programs(0)

    def start(blk, slot):
        def body(r, c):
            _row_copy(x_hbm, buf_ref.at[slot], sem.at[slot], tok_ref[blk * MOE_BLOCK + r], r).start()
            return c
        lax.fori_loop(0, MOE_BLOCK, body, 0)

    def wait(slot):
        def body(r, c):
            _row_copy(x_hbm, buf_ref.at[slot], sem.at[slot], 0, r).wait()
            return c
        lax.fori_loop(0, MOE_BLOCK, body, 0)

    @pl.when(i == 0)
    def _():
        start(0, 0)

    slot = i % 2

    @pl.when(i + 1 < n)
    def _():
        start(i + 1, 1 - slot)

    wait(slot)
    x = buf_ref[slot]
    ms = jnp.mean(x * x, axis=-1, keepdims=True)
    o_ref[...] = (x * lax.rsqrt(ms + NORM_EPS) * g_ref[...]).astype(o_ref.dtype)


def _dispatch(row_tok, x1, g):
    n, d = x1.shape
    rows = row_tok.shape[0]
    return pl.pallas_call(
        _dispatch_kernel,
        grid_spec=pltpu.PrefetchScalarGridSpec(
            num_scalar_prefetch=1,
            grid=(rows // MOE_BLOCK,),
            in_specs=[pl.BlockSpec(memory_space=pl.ANY),
                      pl.BlockSpec((1, d), lambda i, tok: (0, 0))],
            out_specs=pl.BlockSpec((MOE_BLOCK, d), lambda i, tok: (i, 0)),
            scratch_shapes=[pltpu.VMEM((2, MOE_BLOCK, d), f32), pltpu.SemaphoreType.DMA((2,))]),
        out_shape=jax.ShapeDtypeStruct((rows, d), bf16),
        compiler_params=_params(("arbitrary",)),
        name="moe_dispatch",
    )(row_tok, x1, g.reshape(1, d))


def _expert_ffn_kernel(ie_ref, ir_ref, in_ref, nit_ref, xs_hbm, wg_ref, wu_ref, wd_ref, y_hbm,
                       x_ref, acc_ref, sem_in, sem_out, *, n_ff):
    it = pl.program_id(0)
    ff = pl.program_id(1)
    nblk = in_ref[it]
    row0 = pl.multiple_of(ir_ref[it] * MOE_BLOCK, MOE_BLOCK)
    span = MOE_ITEM_BLOCKS * MOE_BLOCK

    @pl.when(nblk > 0)
    def _():
        @pl.when(ff == 0)
        def _():
            cp = pltpu.make_async_copy(xs_hbm.at[pl.ds(row0, span), :], x_ref, sem_in)
            cp.start()
            cp.wait()

        x = x_ref[...]
        gate = jnp.dot(x, wg_ref[0].astype(bf16), preferred_element_type=f32)
        up = jnp.dot(x, wu_ref[0].astype(bf16), preferred_element_type=f32)
        act = (gate * _sigmoid(gate) * up).astype(bf16)
        y = jnp.dot(act, wd_ref[0].astype(bf16), preferred_element_type=f32)

        @pl.when(ff == 0)
        def _():
            acc_ref[...] = y

        @pl.when(ff > 0)
        def _():
            acc_ref[...] += y

        @pl.when(ff == n_ff - 1)
        def _():
            for sb in range(MOE_ITEM_BLOCKS):
                @pl.when(sb < nblk)
                def _(sb=sb):
                    rs = pl.ds(sb * MOE_BLOCK, MOE_BLOCK)
                    cp = pltpu.make_async_copy(acc_ref.at[rs, :],
                                               y_hbm.at[pl.ds(row0 + sb * MOE_BLOCK, MOE_BLOCK), :], sem_out)
                    cp.start()
                    cp.wait()

    @pl.when(jnp.logical_and(it == pl.num_programs(0) - 1, ff == n_ff - 1))
    def _():
        acc_ref[0:MOE_BLOCK, :] = jnp.zeros((MOE_BLOCK, acc_ref.shape[1]), f32)

        def fill(blk, c):
            cp = pltpu.make_async_copy(acc_ref.at[pl.ds(0, MOE_BLOCK), :],
                                       y_hbm.at[pl.ds(pl.multiple_of(blk * MOE_BLOCK, MOE_BLOCK), MOE_BLOCK), :],
                                       sem_out)
            cp.start()
            cp.wait()
            return c

        lax.fori_loop(nit_ref[1], y_hbm.shape[0] // MOE_BLOCK, fill, 0)


def _expert_ffn(item_e, item_r, item_n, n_items, xs, w_gate, w_up, w_down, max_items):
    rows, d = xs.shape
    dff = w_gate.shape[3]
    tf = _pick_tile(dff, MOE_FF_TILE, 128)
    n_ff = dff // tf
    span = MOE_ITEM_BLOCKS * MOE_BLOCK

    def f_eff(i, f, nit):
        return jnp.where(i < nit[0], f, n_ff - 1)

    return pl.pallas_call(
        functools.partial(_expert_ffn_kernel, n_ff=n_ff),
        grid_spec=pltpu.PrefetchScalarGridSpec(
            num_scalar_prefetch=4,
            grid=(max_items, n_ff),
            in_specs=[pl.BlockSpec(memory_space=pl.ANY),
                      pl.BlockSpec((None, 1, d, tf), lambda i, f, ie, ir, inn, nit: (0, ie[i], 0, f_eff(i, f, nit))),
                      pl.BlockSpec((None, 1, d, tf), lambda i, f, ie, ir, inn, nit: (0, ie[i], 0, f_eff(i, f, nit))),
                      pl.BlockSpec((None, 1, tf, d), lambda i, f, ie, ir, inn, nit: (0, ie[i], f_eff(i, f, nit), 0))],
            out_specs=pl.BlockSpec(memory_space=pl.ANY),
            scratch_shapes=[pltpu.VMEM((span, d), bf16), pltpu.VMEM((span, d), f32),
                            pltpu.SemaphoreType.DMA(()), pltpu.SemaphoreType.DMA(())]),
        out_shape=jax.ShapeDtypeStruct((rows, d), f32),
        compiler_params=_params(("arbitrary", "arbitrary")),
        name="moe_expert_ffn",
    )(item_e, item_r, item_n, n_items, xs, w_gate, w_up, w_down)


def _combine_kernel(d1_ref, d2_ref, y_hbm, x_ref, route_ref, g_ref, o_ref, buf_ref, sem):
    i = pl.program_id(0)
    n = pl.num_programs(0)

    def start(blk, slot):
        def body(r, c):
            _row_copy(y_hbm, buf_ref.at[slot, 0], sem.at[slot], d1_ref[blk * ROW_BLOCK + r], r).start()
            _row_copy(y_hbm, buf_ref.at[slot, 1], sem.at[slot], d2_ref[blk * ROW_BLOCK + r], r).start()
            return c
        lax.fori_loop(0, ROW_BLOCK, body, 0)

    def wait(slot):
        def body(r, c):
            _row_copy(y_hbm, buf_ref.at[slot, 0], sem.at[slot], 0, r).wait()
            _row_copy(y_hbm, buf_ref.at[slot, 1], sem.at[slot], 0, r).wait()
            return c
        lax.fori_loop(0, ROW_BLOCK, body, 0)

    @pl.when(i == 0)
    def _():
        start(0, 0)

    slot = i % 2

    @pl.when(i + 1 < n)
    def _():
        start(i + 1, 1 - slot)

    wait(slot)
    route = route_ref[...]
    x = x_ref[...] + route[:, 2:3] * buf_ref[slot, 0] + route[:, 3:4] * buf_ref[slot, 1]
    ms = jnp.mean(x * x, axis=-1, keepdims=True)
    o_ref[...] = x * lax.rsqrt(ms + NORM_EPS) * g_ref[...]


def _combine(d1, d2, yb, x1, route, g):
    n, d = x1.shape
    return pl.pallas_call(
        _combine_kernel,
        grid_spec=pltpu.PrefetchScalarGridSpec(
            num_scalar_prefetch=2,
            grid=(n // ROW_BLOCK,),
            in_specs=[pl.BlockSpec(memory_space=pl.ANY),
                      pl.BlockSpec((ROW_BLOCK, d), lambda i, a, b: (i, 0)),
                      pl.BlockSpec((ROW_BLOCK, LANE), lambda i, a, b: (i, 0)),
                      pl.BlockSpec((1, d), lambda i, a, b: (0, 0))],
            out_specs=pl.BlockSpec((ROW_BLOCK, d), lambda i, a, b: (i, 0)),
            scratch_shapes=[pltpu.VMEM((2, 2, ROW_BLOCK, d), f32), pltpu.SemaphoreType.DMA((2,))]),
        out_shape=jax.ShapeDtypeStruct((n, d), f32),
        compiler_params=_params(("arbitrary",)),
        name="moe_combine",
    )(d1, d2, yb, x1, route, g.reshape(1, d))


def _moe(x1, norm_ffn, w_rg, b_rg, w_re, b_re, w_gate, w_up, w_down, norm_final):
    n, d = x1.shape
    n_exp = w_gate.shape[1]
    ng = w_rg.shape[1]
    w_r = jnp.zeros((d, LANE), f32).at[:, :ng].set(w_rg).at[:, ng:ng + n_exp].set(w_re)
    b_r = jnp.zeros((1, LANE), f32).at[0, :ng].set(b_rg).at[0, ng:ng + n_exp].set(b_re)
    route, cnt = _router(x1, norm_ffn, w_r, b_r)

    e1 = route[:, 0].astype(i32)
    e2 = route[:, 1].astype(i32)
    counts = cnt[0, ng:ng + n_exp].astype(i32)
    nblk_e = (counts + MOE_BLOCK - 1) // MOE_BLOCK
    blk_start = jnp.cumsum(nblk_e) - nblk_e
    d1 = blk_start[e1] * MOE_BLOCK + route[:, 4].astype(i32)
    d2 = blk_start[e2] * MOE_BLOCK + route[:, 5].astype(i32)
    max_blocks = -(-2 * n // MOE_BLOCK) + n_exp
    rows = (max_blocks + MOE_ITEM_BLOCKS) * MOE_BLOCK
    tok = jnp.arange(n, dtype=i32)
    row_tok = jnp.zeros((rows,), i32).at[d1].set(tok).at[d2].set(tok)
    items_e = (nblk_e + MOE_ITEM_BLOCKS - 1) // MOE_ITEM_BLOCKS
    item_end = jnp.cumsum(items_e)
    n_items = item_end[-1]
    max_items = -(-max_blocks // MOE_ITEM_BLOCKS) + n_exp
    it = jnp.arange(max_items, dtype=i32)
    it_c = jnp.minimum(it, n_items - 1)
    item_e = jnp.minimum(jnp.searchsorted(item_end, it_c, side="right"), n_exp - 1).astype(i32)
    local = it_c - (item_end - items_e)[item_e]
    item_r = (blk_start[item_e] + local * MOE_ITEM_BLOCKS).astype(i32)
    item_n = jnp.where(it < n_items, jnp.minimum(MOE_ITEM_BLOCKS, nblk_e[item_e] - local * MOE_ITEM_BLOCKS), 0)

    xs = _dispatch(row_tok, x1, norm_ffn)
    meta = jnp.stack([n_items, jnp.sum(nblk_e)]).astype(i32)
    yb = _expert_ffn(item_e, item_r, item_n.astype(i32), meta, xs, w_gate, w_up, w_down, max_items)
    return _combine(d1, d2, yb, x1, route, norm_final)


def kernel(x_prompt, x_sample, cache_k, cache_v, cache_idx_k, state_hgrn, page_table, norm_mix, w_in,
           lb_logits, norm_hgrn_out, w_out, norm_ffn, router_group, router_group_bias, router_expert,
           router_expert_bias, expert_w_gate, expert_w_up, expert_w_down, norm_final):
    n_b, seq, d = x_prompt.shape
    n_s, n_tok, _ = x_sample.shape
    depth = norm_mix.shape[0]
    assert depth == 1
    kd = A_HEADS * A_EXPAND
    dh = d // B_HEADS
    kvd = B_KV_HEADS * dh
    group = B_HEADS // B_KV_HEADS
    np_rows = n_b * seq
    ns_rows = n_s * n_tok
    n_all = np_rows + ns_rows
    n_pad = -(-n_all // ROW_BLOCK) * ROW_BLOCK
    assert np_rows % ROW_BLOCK == 0

    lower = jnp.cumsum(jax.nn.softmax(lb_logits.astype(f32), axis=0), axis=0)[0]
    x_all = jnp.concatenate([x_prompt.reshape(np_rows, d), x_sample.reshape(ns_rows, d),
                             jnp.zeros((n_pad - n_all, d), f32)], axis=0)

    w = w_in[0]
    o_bq = 4 * kd
    o_bk = o_bq + d
    o_bqi = o_bk + 2 * kvd
    o_sm = o_bqi + IDX_HEADS * IDX_DIM
    o_g = o_sm + IDX_DIM + IDX_HEADS
    small_w = 2 * LANE
    h = _rmsnorm_bf16(x_all, norm_mix[0])
    z_a = _matmul(h, w[:, :o_bq].astype(bf16), name="proj_hgrn")
    z_bq = _matmul(h, w[:, o_bq:o_bk].astype(bf16), name="proj_q")
    z_kv = _matmul(h, w[:, o_bk:o_bqi].astype(bf16), name="proj_kv")
    z_qi = _matmul(h, w[:, o_bqi:o_sm].astype(bf16), name="proj_qidx")
    w_small = jnp.zeros((d, small_w), bf16).at[:, :o_g - o_sm].set(w[:, o_sm:o_g].astype(bf16))
    z_sm = _matmul(h, w_small, name="proj_small")
    z_g = _matmul(h, w[:, o_g:].astype(bf16), name="proj_gates")

    rb = min(HGRN_ROWS, seq)
    oa_p, s_p = _hgrn(z_a, lower, norm_hgrn_out[0], n_b, seq, rb, min(HGRN_CHUNK, rb), rb,
                      min(HGRN_PROMPT_HEADS, A_HEADS))
    sp = HGRN_SAMPLE_ROWS
    zs = z_a[np_rows:n_all].reshape(n_s, n_tok, 4 * kd)
    zs = jnp.concatenate([zs, jnp.zeros((n_s, sp - n_tok, 4 * kd), f32)], axis=1).reshape(n_s * sp, 4 * kd)
    oa_sp, s_s = _hgrn(zs, lower, norm_hgrn_out[0], n_s, sp, sp, sp, n_tok, min(HGRN_SAMPLE_HEADS, A_HEADS),
                       s0=state_hgrn[0])
    oa_s = oa_sp.reshape(n_s, sp, d)[:, :n_tok].reshape(ns_rows, d)

    ob_p = _dsa_prompt(z_qi, z_sm, z_bq, z_kv, n_b, seq)
    sm_s = z_sm[np_rows:n_all]
    w_s = sm_s[:, IDX_DIM:IDX_DIM + IDX_HEADS].reshape(n_s, n_tok, IDX_HEADS) * (IDX_HEADS ** -0.5 * IDX_DIM ** -0.5)
    eye = jnp.eye(8, n_tok, dtype=f32)
    wsel = (eye[None, :, :, None] * w_s[:, None, :, :]).reshape(n_s, 8, n_tok * IDX_HEADS)
    qi_s = z_qi[np_rows:n_all].reshape(n_s, n_tok * IDX_HEADS, IDX_DIM)
    q_s = z_bq[np_rows:n_all].reshape(n_s, n_tok, B_KV_HEADS, group, dh).transpose(0, 2, 1, 3, 4)
    q_s = q_s.reshape(n_s, B_KV_HEADS, n_tok * group, dh)

    def pad_rows_to_lane(a):
        return jnp.concatenate([a, jnp.zeros((n_s, LANE - n_tok) + a.shape[2:], f32)], axis=1)

    def new_kv(a):
        return pad_rows_to_lane(a.reshape(n_s, n_tok, B_KV_HEADS, dh)).transpose(0, 2, 1, 3)

    kv_s = z_kv[np_rows:n_all]
    ob_s = _dsa_sample(qi_s, wsel, q_s, pad_rows_to_lane(sm_s[:, :IDX_DIM].reshape(n_s, n_tok, IDX_DIM)),
                       new_kv(kv_s[:, :kvd]), new_kv(kv_s[:, kvd:]),
                       cache_k, cache_v, cache_idx_k, page_table, n_tok)
    ob_s = ob_s.reshape(n_s, B_KV_HEADS, n_tok, group, dh).transpose(0, 2, 1, 3, 4).reshape(ns_rows, d)

    pad_rows = jnp.zeros((n_pad - n_all, d), f32)
    oa = jnp.concatenate([oa_p, oa_s, pad_rows], axis=0)
    ob = jnp.concatenate([ob_p, ob_s, pad_rows], axis=0)
    m = _merge(z_g, oa, ob)
    x1 = _matmul(m, w_out[0].astype(bf16), res=x_all, name="out_proj")

    y = _moe(x1, norm_ffn[0], router_group[0], router_group_bias[0], router_expert[0], router_expert_bias[0],
             expert_w_gate, expert_w_up, expert_w_down, norm_final)

    kp = z_kv[:np_rows]
    ks = z_kv[np_rows:n_all]
    return (y[:np_rows].reshape(n_b, seq, d),
            y[np_rows:n_all].reshape(n_s, n_tok, d),
            kp[:, :kvd].reshape(1, n_b, seq, B_KV_HEADS, dh),
            kp[:, kvd:].reshape(1, n_b, seq, B_KV_HEADS, dh),
            z_sm[:np_rows, :IDX_DIM].reshape(1, n_b, seq, IDX_DIM),
            s_p[None],
            ks[:, :kvd].reshape(1, n_s, n_tok, B_KV_HEADS, dh),
            ks[:, kvd:].reshape(1, n_s, n_tok, B_KV_HEADS, dh),
            z_sm[np_rows:n_all, :IDX_DIM].reshape(1, n_s, n_tok, IDX_DIM),
            s_s[None])
```

```python
import functools
import math

import jax
import jax.numpy as jnp
import numpy as np
from jax import lax
from jax.experimental import pallas as pl
from jax.experimental.pallas import tpu as pltpu

A_HEADS = 32
A_EXPAND = 128
B_HEADS = 32
B_KV_HEADS = 8
IDX_HEADS = 32
IDX_DIM = 128
TOPK_MAX = 256
N_GROUPS = 8
EXPERTS_PER_GROUP = 8
NORM_EPS = 1e-6

LANE = 128
ROW_BLOCK = 128
MOE_BLOCK = 128
MOE_ITEM_BLOCKS = 4
MOE_FF_TILE = 256
MOE_K_TILE = 512
HGRN_CHUNK = 64
HGRN_ROWS = 512
HGRN_SAMPLE_ROWS = 16
HGRN_SAMPLE_HEADS = 8
HGRN_PROMPT_HEADS = 2
HGRN_SAFE_DECAY = 60.0
PROJ_COL_TILE = 512
SAMPLE_PAGES_PER_STEP = 8
DMA_ISSUE_UNROLL = 8
COUNT_CHAINS = 4
NEG_BIG = -1e30
LOG2E = 1.4426950408889634
VMEM_LIMIT = 56 * 1024 * 1024

f32 = jnp.float32
bf16 = jnp.bfloat16
i32 = jnp.int32
HIGHEST = lax.Precision.HIGHEST
NT_DIMS = (((1,), (1,)), ((), ()))
TN_DIMS = (((0,), (0,)), ((), ()))


def _pick_tile(n, cap, align):
    t = (min(cap, n) // align) * align
    while t > align and n % t:
        t -= align
    assert t > 0 and n % t == 0, (n, cap, align)
    return t


def _params(sem, limit=VMEM_LIMIT):
    return pltpu.CompilerParams(dimension_semantics=sem, vmem_limit_bytes=limit)


def _sigmoid(x):
    return 1.0 / (1.0 + jnp.exp(-x))


def _log2(n):
    s = int(math.log2(n))
    assert 1 << s == n, n
    return s


def _rmsnorm_kernel(x_ref, g_ref, o_ref):
    x = x_ref[...]
    ms = jnp.mean(x * x, axis=-1, keepdims=True)
    o_ref[...] = (x * lax.rsqrt(ms + NORM_EPS) * g_ref[...]).astype(o_ref.dtype)


def _rmsnorm_bf16(x, g):
    n, d = x.shape
    return pl.pallas_call(
        _rmsnorm_kernel,
        grid=(n // ROW_BLOCK,),
        in_specs=[pl.BlockSpec((ROW_BLOCK, d), lambda i: (i, 0)),
                  pl.BlockSpec((1, d), lambda i: (0, 0))],
        out_specs=pl.BlockSpec((ROW_BLOCK, d), lambda i: (i, 0)),
        out_shape=jax.ShapeDtypeStruct((n, d), bf16),
        compiler_params=_params(("parallel",)),
        name="rmsnorm_bf16",
    )(x, g.reshape(1, d))


def _mm_kernel(a_ref, b_ref, o_ref):
    o_ref[...] = jnp.dot(a_ref[...], b_ref[...], preferred_element_type=f32)


def _mm_res_kernel(a_ref, b_ref, r_ref, o_ref):
    o_ref[...] = r_ref[...] + jnp.dot(a_ref[...], b_ref[...], preferred_element_type=f32)


def _matmul(a, b, res=None, name="matmul"):
    m, k = a.shape
    n = b.shape[1]
    tm = _pick_tile(m, 640, 128)
    tn = _pick_tile(n, 1024, 128)
    in_specs = [pl.BlockSpec((tm, k), lambda j, i: (i, 0)),
                pl.BlockSpec((k, tn), lambda j, i: (0, j))]
    args = [a, b]
    kern = _mm_kernel
    if res is not None:
        in_specs.append(pl.BlockSpec((tm, tn), lambda j, i: (i, j)))
        args.append(res)
        kern = _mm_res_kernel
    return pl.pallas_call(
        kern,
        grid=(n // tn, m // tm),
        in_specs=in_specs,
        out_specs=pl.BlockSpec((tm, tn), lambda j, i: (i, j)),
        out_shape=jax.ShapeDtypeStruct((m, n), f32),
        compiler_params=_params(("parallel", "parallel")),
        name=name,
    )(*args)


def _mm_w_kernel(*refs, has_res):
    if has_res:
        a_ref, w_ref, r_ref, o_ref, wb_ref = refs
    else:
        a_ref, w_ref, o_ref, wb_ref = refs

    @pl.when(pl.program_id(1) == 0)
    def _():
        wb_ref[...] = w_ref[...].astype(bf16)

    acc = jnp.dot(a_ref[...], wb_ref[...], preferred_element_type=f32)
    o_ref[...] = r_ref[...] + acc if has_res else acc


def _matmul_w(a, w3, col0, n, tn, res=None, name="matmul_w"):
    m, k = a.shape
    assert col0 % tn == 0 and n % tn == 0 and w3.shape[1] == k
    tm = _pick_tile(m, 640, 128)
    cb = col0 // tn
    in_specs = [pl.BlockSpec((tm, k), lambda j, i: (i, 0)),
                pl.BlockSpec((None, k, tn), lambda j, i: (0, 0, cb + j))]
    args = [a, w3]
    if res is not None:
        in_specs.append(pl.BlockSpec((tm, tn), lambda j, i: (i, j)))
        args.append(res)
    return pl.pallas_call(
        functools.partial(_mm_w_kernel, has_res=res is not None),
        grid=(n // tn, m // tm),
        in_specs=in_specs,
        out_specs=pl.BlockSpec((tm, tn), lambda j, i: (i, j)),
        out_shape=jax.ShapeDtypeStruct((m, n), f32),
        scratch_shapes=[pltpu.VMEM((k, tn), bf16)],
        compiler_params=_params(("parallel", "arbitrary")),
        name=name,
    )(*args)


def _hgrn_static(c):
    levels = _log2(c)
    t = np.arange(c)
    cmat = np.zeros((2 * c, c), np.float32)
    cmat[:c] = t[None, :] <= t[:, None]
    cmat[c:] = 1.0
    lmat = np.zeros((levels * c, c), np.float32)
    lv = np.full((c, c), -1, np.int32)
    for l in range(levels):
        blk = c >> l
        half = blk // 2
        mid = (t // blk) * blk + half
        lmat[l * c:(l + 1) * c] = t[None, :] <= (mid - 1)[:, None]
        same = (t[:, None] // blk) == (t[None, :] // blk)
        up = ((t // half) % 2 == 1)[:, None]
        lo = ((t // half) % 2 == 0)[None, :]
        lv[same & up & lo] = l
    lv[t, t] = levels
    return cmat, lmat, lv, levels


def _split3(x):
    hi = x.astype(bf16)
    r = x - hi.astype(f32)
    mid = r.astype(bf16)
    lo = (r - mid.astype(f32)).astype(bf16)
    return jnp.concatenate([hi, mid, lo], axis=1)


def _sum3(y):
    w = y.shape[1] // 3
    return y[:, :w] + y[:, w:2 * w] + y[:, 2 * w:]


def _hgrn_kernel(*refs, rows, chunk, levels, n_valid, heads, has_init):
    if has_init:
        (zq_ref, zf_ref, zi_ref, zg_ref, lb_ref, gain_ref, cmat_ref, lmat_ref, lv_ref, s0_ref,
         oa_ref, sfin_ref, st_ref, a_ref, q_sc, k_sc, b_sc, p_sc) = refs
    else:
        (zq_ref, zf_ref, zi_ref, zg_ref, lb_ref, gain_ref, cmat_ref, lmat_ref, lv_ref,
         oa_ref, sfin_ref, st_ref, a_ref, q_sc, k_sc, b_sc, p_sc) = refs
        s0_ref = None
    tb = pl.program_id(2)
    c = chunk
    nc = rows // c
    dk = A_EXPAND

    @pl.when(tb == 0)
    def _():
        for hh in range(heads):
            if has_init:
                st_ref[hh] = s0_ref[0, hh].T
            else:
                st_ref[hh] = jnp.zeros(st_ref.shape[1:], f32)

    lv = lv_ref[...]
    cmat = cmat_ref[...]
    row_c = lax.broadcasted_iota(i32, (c, dk), 0)

    pre = []
    unsafe = None
    for hh in range(heads):
        ls = slice(hh * dk, (hh + 1) * dk)
        zq = zq_ref[:, ls]
        lb = lb_ref[:, ls]
        f = lb + (1.0 - lb) * _sigmoid(zf_ref[:, ls])
        logf = jnp.log(f)
        k = 1.0 - f
        q = zq * _sigmoid(zq)
        if n_valid < rows:
            live = lax.broadcasted_iota(i32, (rows, dk), 0) < n_valid
            logf = jnp.where(live, logf, 0.0)
            k = jnp.where(live, k, 0.0)
        parts = _split3(logf)
        bs, bls = [], []
        for ci in range(nc):
            y = _sum3(jnp.dot(cmat, parts[ci * c:(ci + 1) * c], preferred_element_type=f32))
            bs.append(y[:c])
            bls.append(y[c:])
        b = jnp.concatenate(bs, axis=0) if nc > 1 else bs[0]
        bl = jnp.concatenate(bls, axis=0) if nc > 1 else bls[0]
        head_unsafe = jnp.min(bl) < -HGRN_SAFE_DECAY
        unsafe = head_unsafe if unsafe is None else jnp.logical_or(unsafe, head_unsafe)
        pre.append(dict(q=q, k=k, b=b, bl=bl, parts=parts, qh_b=(q * jnp.exp(b)).astype(bf16),
                        kd=(k * jnp.exp(bl - b)).astype(bf16), v_b=zi_ref[:, ls].astype(bf16)))

    @pl.when(jnp.logical_not(unsafe))
    def _():
        for hh, hv in enumerate(pre):
            kt = (hv["k"] * jnp.exp(-hv["b"])).astype(bf16)
            for ci in range(nc):
                rs = slice(ci * c, (ci + 1) * c)
                al = lax.dot_general(hv["qh_b"][rs], kt[rs], NT_DIMS, preferred_element_type=f32)
                a_ref[hh * nc + ci] = jnp.where(lv >= 0, al, 0.0)

    @pl.when(unsafe)
    def _():
        for hh, hv in enumerate(pre):
            hr = slice(hh * rows, (hh + 1) * rows)
            q_sc[hr, :] = hv["q"]
            k_sc[hr, :] = hv["k"]
            b_sc[hr, :] = hv["b"]
            p_sc[hr, :] = hv["parts"]

        def safe_chunk(ci, carry):
            r0 = pl.multiple_of(ci * c, c)
            q_c = q_sc[pl.ds(r0, c), :]
            k_c = k_sc[pl.ds(r0, c), :]
            b_c = b_sc[pl.ds(r0, c), :]
            refs_b = _sum3(jnp.dot(lmat_ref[...], p_sc[pl.ds(r0, c), :], preferred_element_type=f32))
            a = jnp.where(lv == levels, jnp.sum(q_c * k_c, axis=-1, keepdims=True), 0.0)
            for l in range(levels):
                e = jnp.exp(-jnp.abs(b_c - refs_b[l * c:(l + 1) * c]))
                upper = ((row_c >> (levels - l - 1)) & 1) == 1
                qt = jnp.where(upper, q_c * e, 0.0).astype(bf16)
                kt = jnp.where(upper, 0.0, k_c * e).astype(bf16)
                al = lax.dot_general(qt, kt, NT_DIMS, preferred_element_type=f32)
                a = jnp.where(lv == l, al, a)
            a_ref[ci] = a
            return carry

        lax.fori_loop(0, heads * nc, safe_chunk, 0)

    for hh, hv in enumerate(pre):
        ls = slice(hh * dk, (hh + 1) * dk)
        gain = gain_ref[:, ls]
        zg = zg_ref[:, ls]
        st = st_ref[hh]
        for ci in range(nc):
            rs = slice(ci * c, (ci + 1) * c)
            o = (jnp.dot(a_ref[hh * nc + ci].astype(bf16), hv["v_b"][rs], preferred_element_type=f32)
                 + lax.dot_general(hv["qh_b"][rs], st.astype(bf16), NT_DIMS, preferred_element_type=f32))
            ms = jnp.mean(o * o, axis=-1, keepdims=True)
            zg_c = zg[rs]
            oa_ref[rs, ls] = o * lax.rsqrt(ms + NORM_EPS) * gain * (zg_c * _sigmoid(zg_c))
            st = (st * jnp.exp(hv["bl"][ci * c:ci * c + 1])
                  + lax.dot_general(hv["v_b"][rs], hv["kd"][rs], TN_DIMS, preferred_element_type=f32))
        st_ref[hh] = st

    @pl.when(tb == pl.num_programs(2) - 1)
    def _():
        for hh in range(heads):
            sfin_ref[0, hh] = st_ref[hh].T


def _hgrn(z_a, lb, gain, n_seq, seq_rows, row_block, chunk, n_valid, heads, s0=None):
    kd = A_HEADS * A_EXPAND
    dv = LANE
    assert z_a.shape[1] == 2 * kd + 2 * A_HEADS * dv and A_EXPAND == LANE and A_HEADS % heads == 0
    cmat, lmat, lv, levels = _hgrn_static(chunk)
    nrb = seq_rows // row_block
    hb = A_HEADS // heads
    wd = heads * LANE

    def zspec(seg):
        return pl.BlockSpec((row_block, wd), lambda b, h, t, seg=seg: (b * nrb + t, seg * hb + h))

    const = lambda b, h, t: (0, 0)
    in_specs = [zspec(0), zspec(1), zspec(2), zspec(3),
                pl.BlockSpec((1, wd), lambda b, h, t: (0, h)),
                pl.BlockSpec((1, wd), lambda b, h, t: (0, h)),
                pl.BlockSpec(cmat.shape, const), pl.BlockSpec(lmat.shape, const), pl.BlockSpec(lv.shape, const)]
    args = [z_a, z_a, z_a, z_a, lb.reshape(1, kd), gain.reshape(1, -1),
            jnp.asarray(cmat, bf16), jnp.asarray(lmat, bf16), jnp.asarray(lv)]
    if s0 is not None:
        in_specs.append(pl.BlockSpec((1, heads, A_EXPAND, dv), lambda b, h, t: (b, h, 0, 0)))
        args.append(s0)
    kern = functools.partial(_hgrn_kernel, rows=row_block, chunk=chunk, levels=levels, n_valid=n_valid,
                             heads=heads, has_init=s0 is not None)
    return pl.pallas_call(
        kern,
        grid=(n_seq, hb, nrb),
        in_specs=in_specs,
        out_specs=[pl.BlockSpec((row_block, wd), lambda b, h, t: (b * nrb + t, h)),
                   pl.BlockSpec((1, heads, A_EXPAND, dv), lambda b, h, t: (b, h, 0, 0))],
        out_shape=[jax.ShapeDtypeStruct((n_seq * seq_rows, A_HEADS * dv), f32),
                   jax.ShapeDtypeStruct((n_seq, A_HEADS, A_EXPAND, dv), f32)],
        scratch_shapes=[pltpu.VMEM((heads, dv, A_EXPAND), f32),
                        pltpu.VMEM((heads * row_block // chunk, chunk, chunk), f32),
                        pltpu.VMEM((heads * row_block, LANE), f32), pltpu.VMEM((heads * row_block, LANE), f32),
                        pltpu.VMEM((heads * row_block, LANE), f32),
                        pltpu.VMEM((heads * row_block, 3 * LANE), bf16)],
        compiler_params=_params(("parallel", "parallel", "arbitrary")),
        name="hgrn2_prompt" if s0 is None else "hgrn2_sample",
    )(*args)


def _sort_key(x):
    b = pltpu.bitcast(x + 0.0, i32)
    return b ^ ((b >> 31) & 0x7FFFFFFF)


def _kth_largest_key(key_ref, k, axis):
    shape = (1, key_ref.shape[1]) if axis == 0 else (key_ref.shape[0], 1)
    int_min = jnp.int32(-2 ** 31)

    n = key_ref.shape[axis]
    parts = COUNT_CHAINS if n % (COUNT_CHAINS * LANE) == 0 else 1
    seg = n // parts

    def step(i, lo):
        cand = lo ^ lax.shift_left(jnp.int32(1), 31 - i)
        cnt = None
        for pi in range(parts):
            sl = slice(pi * seg, (pi + 1) * seg)
            keys = key_ref[sl, :] if axis == 0 else key_ref[:, sl]
            c = jnp.sum((keys >= cand).astype(i32), axis=axis, keepdims=True)
            cnt = c if cnt is None else cnt + c
        return jnp.where(cnt >= k, cand, lo)

    return lax.fori_loop(0, 32, step, jnp.full(shape, int_min, i32))


def _dsa_prompt_kernel(qi_ref, smq_ref, smk_ref, q_ref, k_ref, v_ref, o_ref, mask_ref, key_ref, qs_ref,
                       *, kext, topk, qblk, group, qb0):
    qb = qb0 + pl.program_id(1)
    g = pl.program_id(2)
    n_tiles = kext // qblk

    def causal(r0):
        kpos = r0 + lax.broadcasted_iota(i32, (qblk, qblk), 0)
        qpos = qb * qblk + lax.broadcasted_iota(i32, (qblk, qblk), 1)
        return kpos <= qpos

    @pl.when(g == 0)
    def _():
        qs_ref[...] = qi_ref[...].astype(bf16)
        w_t = smq_ref[...].T * (IDX_HEADS ** -0.5 * IDX_DIM ** -0.5)

        def score_tile(ti, carry):
            r0 = pl.multiple_of(ti * qblk, qblk)
            kid = smk_ref[pl.ds(r0, qblk), 0:IDX_DIM].astype(bf16)
            acc = jnp.zeros((qblk, qblk), f32)
            for h in range(IDX_HEADS):
                s = lax.dot_general(kid, qs_ref[:, h * IDX_DIM:(h + 1) * IDX_DIM], NT_DIMS,
                                    preferred_element_type=f32)
                acc = acc + jnp.maximum(s, 0.0) * w_t[IDX_DIM + h:IDX_DIM + h + 1, :]
            key_ref[pl.ds(r0, qblk), :] = _sort_key(jnp.where(causal(r0), acc, -jnp.inf))
            return carry

        lax.fori_loop(0, n_tiles, score_tile, 0)
        thr = _kth_largest_key(key_ref, topk, axis=0)
        keys = key_ref[...]
        kpos = lax.broadcasted_iota(i32, (kext, qblk), 0)
        qpos = qb * qblk + lax.broadcasted_iota(i32, (kext, qblk), 1)
        mask_ref[...] = jnp.where(jnp.logical_and(kpos <= qpos, keys >= thr), 0.0, -jnp.inf)
        need = topk - jnp.sum((keys > thr).astype(i32), axis=0, keepdims=True)
        n_tie = jnp.sum((keys == thr).astype(i32), axis=0, keepdims=True)

        @pl.when(jnp.max(n_tie - need) > 0)
        def _():
            strict = (lax.broadcasted_iota(i32, (qblk, qblk), 0) > lax.broadcasted_iota(i32, (qblk, qblk), 1))
            strict = jnp.where(strict, 1.0, 0.0).astype(bf16)
            need_f = need.astype(f32)

            def tile(ti, seen):
                r0 = pl.multiple_of(ti * qblk, qblk)
                kt = key_ref[pl.ds(r0, qblk), :]
                tie = kt == thr
                tie_f = jnp.where(tie, 1.0, 0.0)
                before = seen + jnp.dot(strict, tie_f.astype(bf16), preferred_element_type=f32)
                keep = jnp.logical_or(kt > thr, jnp.logical_and(tie, before < need_f))
                keep = jnp.logical_and(keep, causal(r0))
                mask_ref[pl.ds(r0, qblk), :] = jnp.where(keep, 0.0, -jnp.inf)
                return seen + jnp.sum(tie_f, axis=0, keepdims=True)

            lax.fori_loop(0, n_tiles, tile, jnp.zeros((1, qblk), f32))

    kk = k_ref[...].astype(bf16)
    vt = v_ref[...].T.astype(bf16)
    dh = kk.shape[1]
    for hq in range(group):
        qh = (q_ref[:, hq * dh:(hq + 1) * dh] * (dh ** -0.5 * LOG2E)).astype(bf16)
        s = lax.dot_general(kk, qh, NT_DIMS, preferred_element_type=f32) + mask_ref[...]
        m = jnp.max(s, axis=0, keepdims=True)
        p = jnp.exp2(s - m)
        den = jnp.sum(p, axis=0, keepdims=True)
        o_t = jnp.dot(vt, p.astype(bf16), preferred_element_type=f32) / den
        o_ref[:, hq * dh:(hq + 1) * dh] = o_t.T


def _dsa_prompt(bqi, small, bq, bkv, n_batch, seq):
    dh = bq.shape[1] // B_HEADS
    group = B_HEADS // B_KV_HEADS
    qblk = LANE
    nqb = seq // qblk
    topk = min(TOPK_MAX, seq // 4)
    parts = []
    qb0 = 0
    kext = 2 * qblk
    while qb0 < nqb:
        kext = min(kext, seq)
        assert seq % kext == 0
        count = kext // qblk - qb0
        nkb = seq // kext
        kern = functools.partial(_dsa_prompt_kernel, kext=kext, topk=topk, qblk=qblk, group=group, qb0=qb0)
        part = pl.pallas_call(
            kern,
            grid=(n_batch, count, B_KV_HEADS),
            in_specs=[pl.BlockSpec((qblk, bqi.shape[1]), lambda b, j, g, q0=qb0: (b * nqb + q0 + j, 0)),
                      pl.BlockSpec((qblk, small.shape[1]), lambda b, j, g, q0=qb0: (b * nqb + q0 + j, 0)),
                      pl.BlockSpec((kext, small.shape[1]), lambda b, j, g, nk=nkb: (b * nk, 0)),
                      pl.BlockSpec((qblk, group * dh), lambda b, j, g, q0=qb0: (b * nqb + q0 + j, g)),
                      pl.BlockSpec((kext, dh), lambda b, j, g, nk=nkb: (b * nk, g)),
                      pl.BlockSpec((kext, dh), lambda b, j, g, nk=nkb: (b * nk, B_KV_HEADS + g))],
            out_specs=pl.BlockSpec((qblk, group * dh), lambda b, j, g, cnt=count: (b * cnt + j, g)),
            out_shape=jax.ShapeDtypeStruct((n_batch * count * qblk, B_HEADS * dh), f32),
            scratch_shapes=[pltpu.VMEM((kext, qblk), f32), pltpu.VMEM((kext, qblk), i32),
                            pltpu.VMEM((qblk, bqi.shape[1]), bf16)],
            compiler_params=_params(("parallel", "parallel", "arbitrary")),
            name=f"dsa_prompt_k{kext}",
        )(bqi, small, small, bq, bkv, bkv)
        parts.append(part.reshape(n_batch, count * qblk, B_HEADS * dh))
        qb0 += count
        kext *= 2
    return jnp.concatenate(parts, axis=1).reshape(n_batch * seq, B_HEADS * dh)


def _dsa_sample_select_kernel(pt_ref, qi_ref, wsel_ref, knew_ref, expand_ref, cik_hbm, mask_ref,
                              slab_ref, sc_ref, key_ref, sel_ref, sem, *, n_pages, page, topk, n_tok):
    b = pl.program_id(0)
    n_seq = pl.num_programs(0)
    past = n_pages * page

    def page_copy(seq, slot, j):
        return pltpu.make_async_copy(cik_hbm.at[0, pt_ref[seq, j]],
                                     slab_ref.at[slot, pl.ds(pl.multiple_of(j * page, page), page), :],
                                     sem.at[slot])

    def start(seq, slot):
        lax.fori_loop(0, n_pages, lambda j, c: (page_copy(seq, slot, j).start(), c)[1], 0)

    def wait(seq, slot):
        lax.fori_loop(0, n_pages, lambda j, c: (page_copy(seq, slot, j).wait(), c)[1], 0)

    @pl.when(b == 0)
    def _():
        start(0, 0)

    slot = b % 2

    @pl.when(b + 1 < n_seq)
    def _():
        start(b + 1, 1 - slot)

    wait(b, slot)
    qi = qi_ref[0].astype(bf16)
    wsel = wsel_ref[0]

    def tile_scores(keys):
        s = lax.dot_general(qi, keys.astype(bf16), NT_DIMS, preferred_element_type=f32)
        return jnp.dot(wsel, jnp.maximum(s, 0.0), precision=HIGHEST, preferred_element_type=f32)

    sc_ref[:, 0:past] = tile_scores(slab_ref[slot])
    new = tile_scores(knew_ref[0])
    r = lax.broadcasted_iota(i32, new.shape, 0)
    cidx = lax.broadcasted_iota(i32, new.shape, 1)
    sc_ref[:, past:past + LANE] = jnp.where(jnp.logical_and(cidx <= r, cidx < n_tok), new, -jnp.inf)
    sc = sc_ref[...]
    key_ref[...] = _sort_key(sc)
    thr = _kth_largest_key(key_ref, topk, axis=1)
    keys = key_ref[...]
    sel = jnp.logical_and(keys >= thr, sc > -jnp.inf)
    sel_ref[...] = jnp.where(sel, 1.0, 0.0)
    need = topk - jnp.sum((keys > thr).astype(i32), axis=1, keepdims=True)
    n_tie = jnp.sum((keys == thr).astype(i32), axis=1, keepdims=True)

    @pl.when(jnp.max(n_tie - need) > 0)
    def _():
        strict = (lax.broadcasted_iota(i32, (LANE, LANE), 0) < lax.broadcasted_iota(i32, (LANE, LANE), 1))
        strict = jnp.where(strict, 1.0, 0.0).astype(bf16)
        need_f = need.astype(f32)

        def tile(ti, seen):
            c1 = pl.multiple_of(ti * LANE, LANE)
            kt = key_ref[:, pl.ds(c1, LANE)]
            tie = kt == thr
            tie_f = jnp.where(tie, 1.0, 0.0)
            before = seen + jnp.dot(tie_f.astype(bf16), strict, preferred_element_type=f32)
            keep = jnp.logical_or(kt > thr, jnp.logical_and(tie, before < need_f))
            keep = jnp.logical_and(keep, sc_ref[:, pl.ds(c1, LANE)] > -jnp.inf)
            sel_ref[:, pl.ds(c1, LANE)] = jnp.where(keep, 1.0, 0.0)
            return seen + jnp.sum(tie_f, axis=1, keepdims=True)

        lax.fori_loop(0, sc_ref.shape[1] // LANE, tile, jnp.zeros((8, 1), f32))

    rep = expand_ref.shape[1]

    def expand(ti, carry):
        c1 = pl.multiple_of(ti * LANE, LANE)
        hit = jnp.dot(sel_ref[:, pl.ds(c1, LANE)].astype(bf16), expand_ref[...], preferred_element_type=f32)
        mask_ref[0, :, pl.ds(pl.multiple_of(ti * rep, rep), rep)] = jnp.where(hit > 0.5, 0.0, NEG_BIG)
        return carry

    lax.fori_loop(0, sc_ref.shape[1] // LANE, expand, 0)


def _dsa_sample_attn_kernel(pt_ref, q_ref, mask_ref, masknew_ref, knew_ref, vnew_ref, gm_ref, ck_hbm, cv_hbm,
                            o_ref, kslab, vslab, m_ref, l_ref, acc_ref, sem,
                            *, n_chunks, ch, n_tok, kvh, group, dh, page):
    b = pl.program_id(0)
    c = pl.program_id(1)
    t = b * n_chunks + c
    total = pl.num_programs(0) * n_chunks
    scale = dh ** -0.5

    def copies(seq, chunk, slot, j):
        pid = pt_ref[seq, chunk * ch + j]
        return (pltpu.make_async_copy(ck_hbm.at[0, pid], kslab.at[slot, j], sem.at[slot, 0]),
                pltpu.make_async_copy(cv_hbm.at[0, pid], vslab.at[slot, j], sem.at[slot, 1]))

    def start(seq, chunk, slot):
        for j in range(ch):
            ck, cv = copies(seq, chunk, slot, j)
            ck.start()
            cv.start()

    def wait(seq, chunk, slot):
        for j in range(ch):
            ck, cv = copies(seq, chunk, slot, j)
            ck.wait()
            cv.wait()

    @pl.when(t == 0)
    def _():
        start(0, 0, 0)

    slot = t % 2

    @pl.when(t + 1 < total)
    def _():
        nxt = t + 1
        start(nxt // n_chunks, nxt % n_chunks, 1 - slot)

    wait(b, c, slot)

    @pl.when(c == 0)
    def _():
        m_ref[...] = jnp.full_like(m_ref, NEG_BIG)
        l_ref[...] = jnp.zeros_like(l_ref)
        acc_ref[...] = jnp.zeros_like(acc_ref)

    qb = (q_ref[0] * scale).astype(bf16)
    heads = kvh * group

    def attend(keys, vals, mask8):
        n_cols = keys.shape[0]
        tok_mask = jnp.concatenate([jnp.broadcast_to(mask8[tk:tk + 1], (heads, n_cols))
                                    for tk in range(n_tok)], axis=0)
        madd = tok_mask + gm_ref[:, 0:n_cols]
        s = lax.dot_general(qb, keys, NT_DIMS, preferred_element_type=f32) + madd
        m_old = m_ref[...]
        m_new = jnp.maximum(m_old, jnp.max(s, axis=-1, keepdims=True))
        pr = jnp.where(madd == 0.0, jnp.exp(s - m_new), 0.0)
        alpha = jnp.exp(m_old - m_new)
        l_ref[...] = alpha * l_ref[...] + jnp.sum(pr, axis=-1, keepdims=True)
        acc_ref[...] = alpha * acc_ref[...] + jnp.dot(pr.astype(bf16), vals, preferred_element_type=f32)
        m_ref[...] = m_new

    attend(kslab[slot].reshape(ch * page * kvh, dh).astype(bf16),
           vslab[slot].reshape(ch * page * kvh, dh).astype(bf16), mask_ref[0])

    @pl.when(c == n_chunks - 1)
    def _():
        attend(knew_ref[0].astype(bf16), vnew_ref[0].astype(bf16), masknew_ref[0])
        o_ref[0] = acc_ref[...] / l_ref[...]


def _dsa_sample(qi, wsel, q, knew_idx, k_new, v_new, cache_k, cache_v, cache_idx_k, page_table, n_tok):
    n_seq, n_pages = page_table.shape
    page, kvh, dh = cache_k.shape[2], cache_k.shape[3], cache_k.shape[4]
    assert page == LANE and LANE % kvh == 0
    rows_i = qi.shape[1]
    rows = q.shape[1]
    heads = rows // n_tok
    group = heads // kvh
    past = n_pages * page
    kp = past + LANE
    topk = min(TOPK_MAX, (past + n_tok) // 4)
    ch = math.gcd(SAMPLE_PAGES_PER_STEP, n_pages)
    n_chunks = n_pages // ch
    expand = np.repeat(np.eye(LANE, dtype=np.float32), kvh, axis=1)
    cw = ch * page * kvh
    own = (np.arange(rows)[:, None] % heads) // group == (np.arange(cw)[None, :] % kvh)
    gm = np.where(own, 0.0, NEG_BIG).astype(np.float32)

    mask = pl.pallas_call(
        functools.partial(_dsa_sample_select_kernel, n_pages=n_pages, page=page, topk=topk, n_tok=n_tok),
        grid_spec=pltpu.PrefetchScalarGridSpec(
            num_scalar_prefetch=1,
            grid=(n_seq,),
            in_specs=[pl.BlockSpec((1, rows_i, IDX_DIM), lambda b, pt: (b, 0, 0)),
                      pl.BlockSpec((1, 8, rows_i), lambda b, pt: (b, 0, 0)),
                      pl.BlockSpec((1, LANE, IDX_DIM), lambda b, pt: (b, 0, 0)),
                      pl.BlockSpec(expand.shape, lambda b, pt: (0, 0)),
                      pl.BlockSpec(memory_space=pl.ANY)],
            out_specs=pl.BlockSpec((1, 8, kp * kvh), lambda b, pt: (b, 0, 0)),
            scratch_shapes=[pltpu.VMEM((2, past, IDX_DIM), f32), pltpu.VMEM((8, kp), f32),
                            pltpu.VMEM((8, kp), i32), pltpu.VMEM((8, kp), f32), pltpu.SemaphoreType.DMA((2,))]),
        out_shape=jax.ShapeDtypeStruct((n_seq, 8, kp * kvh), f32),
        compiler_params=_params(("arbitrary",)),
        name="dsa_sample_select",
    )(page_table, qi, wsel, knew_idx, jnp.asarray(expand, bf16), cache_idx_k)

    return pl.pallas_call(
        functools.partial(_dsa_sample_attn_kernel, n_chunks=n_chunks, ch=ch, n_tok=n_tok, kvh=kvh, group=group,
                          dh=dh, page=page),
        grid_spec=pltpu.PrefetchScalarGridSpec(
            num_scalar_prefetch=1,
            grid=(n_seq, n_chunks),
            in_specs=[pl.BlockSpec((1, rows, dh), lambda b, c, pt: (b, 0, 0)),
                      pl.BlockSpec((1, 8, cw), lambda b, c, pt: (b, 0, c)),
                      pl.BlockSpec((1, 8, LANE * kvh), lambda b, c, pt: (b, 0, past // LANE)),
                      pl.BlockSpec((1, LANE * kvh, dh), lambda b, c, pt: (b, 0, 0)),
                      pl.BlockSpec((1, LANE * kvh, dh), lambda b, c, pt: (b, 0, 0)),
                      pl.BlockSpec(gm.shape, lambda b, c, pt: (0, 0)),
                      pl.BlockSpec(memory_space=pl.ANY),
                      pl.BlockSpec(memory_space=pl.ANY)],
            out_specs=pl.BlockSpec((1, rows, dh), lambda b, c, pt: (b, 0, 0)),
            scratch_shapes=[pltpu.VMEM((2, ch, page, kvh, dh), f32), pltpu.VMEM((2, ch, page, kvh, dh), f32),
                            pltpu.VMEM((rows, 1), f32), pltpu.VMEM((rows, 1), f32),
                            pltpu.VMEM((rows, dh), f32), pltpu.SemaphoreType.DMA((2, 2))]),
        out_shape=jax.ShapeDtypeStruct((n_seq, rows, dh), f32),
        compiler_params=_params(("arbitrary", "arbitrary")),
        name="dsa_sample_attn",
    )(page_table, q, mask, mask, k_new, v_new, jnp.asarray(gm), cache_k, cache_v)


def _merge_kernel(ga_ref, gb_ref, oa_ref, ob_ref, o_ref):
    o_ref[...] = (_sigmoid(ga_ref[...]) * oa_ref[...] + _sigmoid(gb_ref[...]) * ob_ref[...]).astype(o_ref.dtype)


def _merge(gates, oa, ob):
    n, d = oa.shape
    tn = _pick_tile(d, 1024, 128)
    nj = d // tn
    return pl.pallas_call(
        _merge_kernel,
        grid=(n // ROW_BLOCK, nj),
        in_specs=[pl.BlockSpec((ROW_BLOCK, tn), lambda i, j: (i, j)),
                  pl.BlockSpec((ROW_BLOCK, tn), lambda i, j: (i, nj + j)),
                  pl.BlockSpec((ROW_BLOCK, tn), lambda i, j: (i, j)),
                  pl.BlockSpec((ROW_BLOCK, tn), lambda i, j: (i, j))],
        out_specs=pl.BlockSpec((ROW_BLOCK, tn), lambda i, j: (i, j)),
        out_shape=jax.ShapeDtypeStruct((n, d), bf16),
        compiler_params=_params(("parallel", "parallel")),
        name="merge_branches",
    )(gates, gates, oa, ob)


def _router_kernel(x_ref, g_ref, w_ref, b_ref, route_ref, cnt_ref, carry_ref):
    i = pl.program_id(0)

    @pl.when(i == 0)
    def _():
        carry_ref[...] = jnp.zeros_like(carry_ref)

    x = x_ref[...]
    ms = jnp.mean(x * x, axis=-1, keepdims=True)
    h = x * lax.rsqrt(ms + NORM_EPS) * g_ref[...]
    logits = jnp.dot(h, w_ref[...], precision=HIGHEST, preferred_element_type=f32) + b_ref[...]
    lane = lax.broadcasted_iota(i32, logits.shape, 1)
    big = jnp.int32(1 << 20)
    ng, epg = N_GROUPS, EXPERTS_PER_GROUP

    def first_max(vals):
        mx = jnp.max(vals, axis=-1, keepdims=True)
        idx = jnp.min(jnp.where(vals == mx, lane, big), axis=-1, keepdims=True)
        return mx, idx

    lg = jnp.where(lane < ng, logits, -jnp.inf)
    mg, g_sel = first_max(lg)
    p_g = 1.0 / jnp.sum(jnp.exp(lg - mg), axis=-1, keepdims=True)
    lo = ng + g_sel * epg
    le = jnp.where(jnp.logical_and(lane >= lo, lane < lo + epg), logits, -jnp.inf)
    m1, i1 = first_max(le)
    se = jnp.sum(jnp.exp(le - m1), axis=-1, keepdims=True)
    m2, i2 = first_max(jnp.where(lane == i1, -jnp.inf, le))
    p1 = 1.0 / se
    p2 = jnp.exp(m2 - m1) / se
    w1 = p_g * p1 / (p1 + p2)
    w2 = p_g * p2 / (p1 + p2)

    hot = jnp.logical_or(lane == i1, lane == i2)
    hot_b = jnp.where(hot, 1.0, 0.0).astype(bf16)
    n = x.shape[0]
    strict = (lax.broadcasted_iota(i32, (n, n), 0) > lax.broadcasted_iota(i32, (n, n), 1))
    before = jnp.dot(jnp.where(strict, 1.0, 0.0).astype(bf16), hot_b, preferred_element_type=f32)
    before = before + carry_ref[...]
    r1 = jnp.sum(jnp.where(lane == i1, before, 0.0), axis=-1, keepdims=True)
    r2 = jnp.sum(jnp.where(lane == i2, before, 0.0), axis=-1, keepdims=True)
    carry_ref[...] = carry_ref[...] + jnp.sum(jnp.where(hot, 1.0, 0.0), axis=0, keepdims=True)

    out = jnp.where(lane == 0, (i1 - ng).astype(f32), 0.0)
    out = jnp.where(lane == 1, (i2 - ng).astype(f32), out)
    out = jnp.where(lane == 2, w1, out)
    out = jnp.where(lane == 3, w2, out)
    out = jnp.where(lane == 4, r1, out)
    out = jnp.where(lane == 5, r2, out)
    route_ref[...] = out
    cnt_ref[...] = carry_ref[...]


def _router(x1, g, w_r, b_r):
    n, d = x1.shape
    return pl.pallas_call(
        _router_kernel,
        grid=(n // ROW_BLOCK,),
        in_specs=[pl.BlockSpec((ROW_BLOCK, d), lambda i: (i, 0)),
                  pl.BlockSpec((1, d), lambda i: (0, 0)),
                  pl.BlockSpec((d, LANE), lambda i: (0, 0)),
                  pl.BlockSpec((1, LANE), lambda i: (0, 0))],
        out_specs=[pl.BlockSpec((ROW_BLOCK, LANE), lambda i: (i, 0)),
                   pl.BlockSpec((1, LANE), lambda i: (0, 0))],
        out_shape=[jax.ShapeDtypeStruct((n, LANE), f32), jax.ShapeDtypeStruct((1, LANE), f32)],
        scratch_shapes=[pltpu.VMEM((1, LANE), f32)],
        compiler_params=_params(("arbitrary",)),
        name="moe_router",
    )(x1, g.reshape(1, d), w_r, b_r)


def _row_copy(src_hbm, dst_vmem, sem, src_row, dst_row):
    return pltpu.make_async_copy(src_hbm.at[pl.ds(src_row, 1), :], dst_vmem.at[pl.ds(dst_row, 1), :], sem)


def _dispatch_kernel(tok_ref, x_hbm, g_ref, o_ref, buf_ref, sem):
    i = pl.program_id(0)
    n = pl.num_programs(0)

    def start(blk, slot):
        def body(r, c):
            _row_copy(x_hbm, buf_ref.at[slot], sem.at[slot], tok_ref[blk * MOE_BLOCK + r], r).start()
            return c
        lax.fori_loop(0, MOE_BLOCK, body, 0, unroll=DMA_ISSUE_UNROLL)

    def wait(slot):
        for r in range(MOE_BLOCK):
            _row_copy(x_hbm, buf_ref.at[slot], sem.at[slot], 0, r).wait()

    @pl.when(i == 0)
    def _():
        start(0, 0)

    slot = i % 2

    @pl.when(i + 1 < n)
    def _():
        start(i + 1, 1 - slot)

    wait(slot)
    x = buf_ref[slot]
    ms = jnp.mean(x * x, axis=-1, keepdims=True)
    o_ref[...] = (x * lax.rsqrt(ms + NORM_EPS) * g_ref[...]).astype(o_ref.dtype)


def _dispatch(row_tok, x1, g):
    n, d = x1.shape
    rows = row_tok.shape[0]
    return pl.pallas_call(
        _dispatch_kernel,
        grid_spec=pltpu.PrefetchScalarGridSpec(
            num_scalar_prefetch=1,
            grid=(rows // MOE_BLOCK,),
            in_specs=[pl.BlockSpec(memory_space=pl.ANY),
                      pl.BlockSpec((1, d), lambda i, tok: (0, 0))],
            out_specs=pl.BlockSpec((MOE_BLOCK, d), lambda i, tok: (i, 0)),
            scratch_shapes=[pltpu.VMEM((2, MOE_BLOCK, d), f32), pltpu.SemaphoreType.DMA((2,))]),
        out_shape=jax.ShapeDtypeStruct((rows, d), bf16),
        compiler_params=_params(("arbitrary",)),
        name="moe_dispatch",
    )(row_tok, x1, g.reshape(1, d))


def _expert_ffn_kernel(ie_ref, ir_ref, in_ref, nit_ref, xs_hbm, wg_ref, wu_ref, wd_ref, y_hbm,
                       x_ref, gate_ref, up_ref, act_ref, acc_ref, sem_in, sem_out, *, n_k, n_ff):
    it = pl.program_id(0)
    st = pl.program_id(1)
    nblk = in_ref[it]
    row0 = pl.multiple_of(ir_ref[it] * MOE_BLOCK, MOE_BLOCK)
    span = MOE_ITEM_BLOCKS * MOE_BLOCK
    tk = wg_ref.shape[1]
    tf = wd_ref.shape[1]

    @pl.when(nblk > 0)
    def _():
        @pl.when(st == 0)
        def _():
            cp = pltpu.make_async_copy(xs_hbm.at[pl.ds(row0, span), :], x_ref, sem_in)
            cp.start()
            cp.wait()

        @pl.when(st < n_k)
        def _():
            x = x_ref[:, pl.ds(pl.multiple_of(st * tk, tk), tk)]
            gate = jnp.dot(x, wg_ref[0].astype(bf16), preferred_element_type=f32)
            up = jnp.dot(x, wu_ref[0].astype(bf16), preferred_element_type=f32)

            @pl.when(st == 0)
            def _():
                gate_ref[...] = gate
                up_ref[...] = up

            @pl.when(st > 0)
            def _():
                gate_ref[...] += gate
                up_ref[...] += up

        @pl.when(st == n_k - 1)
        def _():
            gate = gate_ref[...]
            act_ref[...] = (gate * _sigmoid(gate) * up_ref[...]).astype(bf16)

        @pl.when(st >= n_k)
        def _():
            ff = st - n_k
            act = act_ref[:, pl.ds(pl.multiple_of(ff * tf, tf), tf)]
            y = jnp.dot(act, wd_ref[0].astype(bf16), preferred_element_type=f32)

            @pl.when(ff == 0)
            def _():
                acc_ref[...] = y

            @pl.when(ff > 0)
            def _():
                acc_ref[...] += y

        @pl.when(st == n_k + n_ff - 1)
        def _():
            for sb in range(MOE_ITEM_BLOCKS):
                @pl.when(sb < nblk)
                def _(sb=sb):
                    rs = pl.ds(sb * MOE_BLOCK, MOE_BLOCK)
                    cp = pltpu.make_async_copy(acc_ref.at[rs, :],
                                               y_hbm.at[pl.ds(row0 + sb * MOE_BLOCK, MOE_BLOCK), :], sem_out)
                    cp.start()
                    cp.wait()

    @pl.when(jnp.logical_and(it == pl.num_programs(0) - 1, st == n_k + n_ff - 1))
    def _():
        acc_ref[0:MOE_BLOCK, :] = jnp.zeros((MOE_BLOCK, acc_ref.shape[1]), f32)

        def fill(blk, c):
            cp = pltpu.make_async_copy(acc_ref.at[pl.ds(0, MOE_BLOCK), :],
                                       y_hbm.at[pl.ds(pl.multiple_of(blk * MOE_BLOCK, MOE_BLOCK), MOE_BLOCK), :],
                                       sem_out)
            cp.start()
            cp.wait()
            return c

        lax.fori_loop(nit_ref[1], y_hbm.shape[0] // MOE_BLOCK, fill, 0)


def _expert_ffn(item_e, item_r, item_n, n_items, xs, w_gate, w_up, w_down, max_items):
    rows, d = xs.shape
    dff = w_gate.shape[3]
    tf = _pick_tile(dff, MOE_FF_TILE, 128)
    tk = _pick_tile(d, MOE_K_TILE, 128)
    n_ff = dff // tf
    n_k = d // tk
    span = MOE_ITEM_BLOCKS * MOE_BLOCK

    def k_idx(i, s, nit):
        return jnp.where(i < nit[0], jnp.minimum(s, n_k - 1), n_k - 1)

    def f_idx(i, s, nit):
        return jnp.where(i < nit[0], jnp.maximum(s - n_k, 0), n_ff - 1)

    return pl.pallas_call(
        functools.partial(_expert_ffn_kernel, n_k=n_k, n_ff=n_ff),
        grid_spec=pltpu.PrefetchScalarGridSpec(
            num_scalar_prefetch=4,
            grid=(max_items, n_k + n_ff),
            in_specs=[pl.BlockSpec(memory_space=pl.ANY),
                      pl.BlockSpec((None, 1, tk, dff), lambda i, s, ie, ir, inn, nit: (0, ie[i], k_idx(i, s, nit), 0)),
                      pl.BlockSpec((None, 1, tk, dff), lambda i, s, ie, ir, inn, nit: (0, ie[i], k_idx(i, s, nit), 0)),
                      pl.BlockSpec((None, 1, tf, d), lambda i, s, ie, ir, inn, nit: (0, ie[i], f_idx(i, s, nit), 0))],
            out_specs=pl.BlockSpec(memory_space=pl.ANY),
            scratch_shapes=[pltpu.VMEM((span, d), bf16), pltpu.VMEM((span, dff), f32), pltpu.VMEM((span, dff), f32),
                            pltpu.VMEM((span, dff), bf16), pltpu.VMEM((span, d), f32),
                            pltpu.SemaphoreType.DMA(()), pltpu.SemaphoreType.DMA(())]),
        out_shape=jax.ShapeDtypeStruct((rows, d), f32),
        compiler_params=_params(("arbitrary", "arbitrary")),
        name="moe_expert_ffn",
    )(item_e, item_r, item_n, n_items, xs, w_gate, w_up, w_down)


def _combine_kernel(d1_ref, d2_ref, y_hbm, x_ref, route_ref, g_ref, o_ref, buf_ref, sem):
    i = pl.program_id(0)
    n = pl.num_programs(0)

    def start(blk, slot):
        def body(r, c):
            _row_copy(y_hbm, buf_ref.at[slot, 0], sem.at[slot], d1_ref[blk * ROW_BLOCK + r], r).start()
            _row_copy(y_hbm, buf_ref.at[slot, 1], sem.at[slot], d2_ref[blk * ROW_BLOCK + r], r).start()
            return c
        lax.fori_loop(0, ROW_BLOCK, body, 0, unroll=DMA_ISSUE_UNROLL)

    def wait(slot):
        for r in range(ROW_BLOCK):
            _row_copy(y_hbm, buf_ref.at[slot, 0], sem.at[slot], 0, r).wait()
            _row_copy(y_hbm, buf_ref.at[slot, 1], sem.at[slot], 0, r).wait()

    @pl.when(i == 0)
    def _():
        start(0, 0)

    slot = i % 2

    @pl.when(i + 1 < n)
    def _():
        start(i + 1, 1 - slot)

    wait(slot)
    route = route_ref[...]
    x = x_ref[...] + route[:, 2:3] * buf_ref[slot, 0] + route[:, 3:4] * buf_ref[slot, 1]
    ms = jnp.mean(x * x, axis=-1, keepdims=True)
    o_ref[...] = x * lax.rsqrt(ms + NORM_EPS) * g_ref[...]


def _combine(d1, d2, yb, x1, route, g):
    n, d = x1.shape
    return pl.pallas_call(
        _combine_kernel,
        grid_spec=pltpu.PrefetchScalarGridSpec(
            num_scalar_prefetch=2,
            grid=(n // ROW_BLOCK,),
            in_specs=[pl.BlockSpec(memory_space=pl.ANY),
                      pl.BlockSpec((ROW_BLOCK, d), lambda i, a, b: (i, 0)),
                      pl.BlockSpec((ROW_BLOCK, LANE), lambda i, a, b: (i, 0)),
                      pl.BlockSpec((1, d), lambda i, a, b: (0, 0))],
            out_specs=pl.BlockSpec((ROW_BLOCK, d), lambda i, a, b: (i, 0)),
            scratch_shapes=[pltpu.VMEM((2, 2, ROW_BLOCK, d), f32), pltpu.SemaphoreType.DMA((2,))]),
        out_shape=jax.ShapeDtypeStruct((n, d), f32),
        compiler_params=_params(("arbitrary",)),
        name="moe_combine",
    )(d1, d2, yb, x1, route, g.reshape(1, d))


def _moe(x1, norm_ffn, w_rg, b_rg, w_re, b_re, w_gate, w_up, w_down, norm_final):
    n, d = x1.shape
    n_exp = w_gate.shape[1]
    ng = w_rg.shape[1]
    w_r = jnp.zeros((d, LANE), f32).at[:, :ng].set(w_rg).at[:, ng:ng + n_exp].set(w_re)
    b_r = jnp.zeros((1, LANE), f32).at[0, :ng].set(b_rg).at[0, ng:ng + n_exp].set(b_re)
    route, cnt = _router(x1, norm_ffn, w_r, b_r)

    e1 = route[:, 0].astype(i32)
    e2 = route[:, 1].astype(i32)
    counts = cnt[0, ng:ng + n_exp].astype(i32)
    nblk_e = (counts + MOE_BLOCK - 1) // MOE_BLOCK
    blk_start = jnp.cumsum(nblk_e) - nblk_e
    d1 = blk_start[e1] * MOE_BLOCK + route[:, 4].astype(i32)
    d2 = blk_start[e2] * MOE_BLOCK + route[:, 5].astype(i32)
    max_blocks = -(-2 * n // MOE_BLOCK) + n_exp
    rows = (max_blocks + MOE_ITEM_BLOCKS) * MOE_BLOCK
    tok = jnp.arange(n, dtype=i32)
    row_tok = jnp.zeros((rows,), i32).at[d1].set(tok).at[d2].set(tok)
    items_e = (nblk_e + MOE_ITEM_BLOCKS - 1) // MOE_ITEM_BLOCKS
    item_end = jnp.cumsum(items_e)
    n_items = item_end[-1]
    max_items = -(-max_blocks // MOE_ITEM_BLOCKS) + n_exp
    it = jnp.arange(max_items, dtype=i32)
    it_c = jnp.minimum(it, n_items - 1)
    item_e = jnp.minimum(jnp.searchsorted(item_end, it_c, side="right"), n_exp - 1).astype(i32)
    local = it_c - (item_end - items_e)[item_e]
    item_r = (blk_start[item_e] + local * MOE_ITEM_BLOCKS).astype(i32)
    item_n = jnp.where(it < n_items, jnp.minimum(MOE_ITEM_BLOCKS, nblk_e[item_e] - local * MOE_ITEM_BLOCKS), 0)

    xs = _dispatch(row_tok, x1, norm_ffn)
    meta = jnp.stack([n_items, jnp.sum(nblk_e)]).astype(i32)
    yb = _expert_ffn(item_e, item_r, item_n.astype(i32), meta, xs, w_gate, w_up, w_down, max_items)
    return _combine(d1, d2, yb, x1, route, norm_final)


def kernel(x_prompt, x_sample, cache_k, cache_v, cache_idx_k, state_hgrn, page_table, norm_mix, w_in,
           lb_logits, norm_hgrn_out, w_out, norm_ffn, router_group, router_group_bias, router_expert,
           router_expert_bias, expert_w_gate, expert_w_up, expert_w_down, norm_final):
    n_b, seq, d = x_prompt.shape
    n_s, n_tok, _ = x_sample.shape
    depth = norm_mix.shape[0]
    assert depth == 1
    kd = A_HEADS * A_EXPAND
    dh = d // B_HEADS
    kvd = B_KV_HEADS * dh
    group = B_HEADS // B_KV_HEADS
    np_rows = n_b * seq
    ns_rows = n_s * n_tok
    n_all = np_rows + ns_rows
    n_pad = -(-n_all // ROW_BLOCK) * ROW_BLOCK
    assert np_rows % ROW_BLOCK == 0

    lower = jnp.cumsum(jax.nn.softmax(lb_logits.astype(f32), axis=0), axis=0)[0]
    x_all = jnp.concatenate([x_prompt.reshape(np_rows, d), x_sample.reshape(ns_rows, d),
                             jnp.zeros((n_pad - n_all, d), f32)], axis=0)

    w = w_in[0]
    o_bq = 4 * kd
    o_bk = o_bq + d
    o_bqi = o_bk + 2 * kvd
    o_sm = o_bqi + IDX_HEADS * IDX_DIM
    o_g = o_sm + IDX_DIM + IDX_HEADS
    small_w = 2 * LANE
    tn = _pick_tile(math.gcd(o_bq, o_bk, o_bqi, o_sm), PROJ_COL_TILE, LANE)
    h = _rmsnorm_bf16(x_all, norm_mix[0])
    z_a = _matmul_w(h, w_in, 0, o_bq, tn, name="proj_hgrn")
    z_bq = _matmul_w(h, w_in, o_bq, o_bk - o_bq, tn, name="proj_q")
    z_kv = _matmul_w(h, w_in, o_bk, o_bqi - o_bk, tn, name="proj_kv")
    z_qi = _matmul_w(h, w_in, o_bqi, o_sm - o_bqi, tn, name="proj_qidx")
    assert o_sm % small_w == 0 and o_sm + small_w <= w_in.shape[2]
    z_sm = _matmul_w(h, w_in, o_sm, small_w, small_w, name="proj_small")
    z_g = _matmul(h, w[:, o_g:].astype(bf16), name="proj_gates")

    rb = min(HGRN_ROWS, seq)
    oa_p, s_p = _hgrn(z_a, lower, norm_hgrn_out[0], n_b, seq, rb, min(HGRN_CHUNK, rb), rb,
                      min(HGRN_PROMPT_HEADS, A_HEADS))
    sp = HGRN_SAMPLE_ROWS
    zs = z_a[np_rows:n_all].reshape(n_s, n_tok, 4 * kd)
    zs = jnp.concatenate([zs, jnp.zeros((n_s, sp - n_tok, 4 * kd), f32)], axis=1).reshape(n_s * sp, 4 * kd)
    oa_sp, s_s = _hgrn(zs, lower, norm_hgrn_out[0], n_s, sp, sp, sp, n_tok, min(HGRN_SAMPLE_HEADS, A_HEADS),
                       s0=state_hgrn[0])
    oa_s = oa_sp.reshape(n_s, sp, d)[:, :n_tok].reshape(ns_rows, d)

    ob_p = _dsa_prompt(z_qi, z_sm, z_bq, z_kv, n_b, seq)
    sm_s = z_sm[np_rows:n_all]
    w_s = sm_s[:, IDX_DIM:IDX_DIM + IDX_HEADS].reshape(n_s, n_tok, IDX_HEADS) * (IDX_HEADS ** -0.5 * IDX_DIM ** -0.5)
    eye = jnp.eye(8, n_tok, dtype=f32)
    wsel = (eye[None, :, :, None] * w_s[:, None, :, :]).reshape(n_s, 8, n_tok * IDX_HEADS)
    qi_s = z_qi[np_rows:n_all].reshape(n_s, n_tok * IDX_HEADS, IDX_DIM)
    q_s = z_bq[np_rows:n_all].reshape(n_s, n_tok * B_HEADS, dh)

    def pad_rows_to_lane(a):
        return jnp.concatenate([a, jnp.zeros((n_s, LANE - n_tok) + a.shape[2:], f32)], axis=1)

    def new_kv(a):
        return pad_rows_to_lane(a.reshape(n_s, n_tok, B_KV_HEADS, dh)).reshape(n_s, LANE * B_KV_HEADS, dh)

    kv_s = z_kv[np_rows:n_all]
    ob_s = _dsa_sample(qi_s, wsel, q_s, pad_rows_to_lane(sm_s[:, :IDX_DIM].reshape(n_s, n_tok, IDX_DIM)),
                       new_kv(kv_s[:, :kvd]), new_kv(kv_s[:, kvd:]),
                       cache_k, cache_v, cache_idx_k, page_table, n_tok)
    ob_s = ob_s.reshape(ns_rows, d)

    pad_rows = jnp.zeros((n_pad - n_all, d), f32)
    oa = jnp.concatenate([oa_p, oa_s, pad_rows], axis=0)
    ob = jnp.concatenate([ob_p, ob_s, pad_rows], axis=0)
    m = _merge(z_g, oa, ob)
    x1 = _matmul_w(m, w_out, 0, d, _pick_tile(d, PROJ_COL_TILE, LANE), res=x_all, name="out_proj")

    y = _moe(x1, norm_ffn[0], router_group[0], router_group_bias[0], router_expert[0], router_expert_bias[0],
             expert_w_gate, expert_w_up, expert_w_down, norm_final)

    kp = z_kv[:np_rows]
    ks = z_kv[np_rows:n_all]
    return (y[:np_rows].reshape(n_b, seq, d),
            y[np_rows:n_all].reshape(n_s, n_tok, d),
            kp[:, :kvd].reshape(1, n_b, seq, B_KV_HEADS, dh),
            kp[:, kvd:].reshape(1, n_b, seq, B_KV_HEADS, dh),
            z_sm[:np_rows, :IDX_DIM].reshape(1, n_b, seq, IDX_DIM),
            s_p[None],
            ks[:, :kvd].reshape(1, n_s, n_tok, B_KV_HEADS, dh),
            ks[:, kvd:].reshape(1, n_s, n_tok, B_KV_HEADS, dh),
            z_sm[np_rows:n_all, :IDX_DIM].reshape(1, n_s, n_tok, IDX_DIM),
            s_s[None])
```

```python
import functools
import math

import jax
import jax.numpy as jnp
import numpy as np
from jax import lax
from jax.experimental import pallas as pl
from jax.experimental.pallas import tpu as pltpu

A_HEADS = 32
A_EXPAND = 128
B_HEADS = 32
B_KV_HEADS = 8
IDX_HEADS = 32
IDX_DIM = 128
TOPK_MAX = 256
N_GROUPS = 8
EXPERTS_PER_GROUP = 8
NORM_EPS = 1e-6

LANE = 128
ROW_BLOCK = 128
MOE_BLOCK = 128
MOE_ITEM_BLOCKS = 4
MOE_FF_TILE = 256
MOE_WEIGHT_SPLIT = 4
HGRN_CHUNK = 64
HGRN_ROWS = 512
HGRN_SAMPLE_ROWS = 16
HGRN_SAMPLE_HEADS = 8
HGRN_PROMPT_HEADS = 4
HGRN_SAFE_DECAY = 60.0
SAMPLE_PAGES_PER_STEP = 8
DMA_ISSUE_UNROLL = 8
COUNT_CHAINS = 4
NEG_BIG = -1e30
LOG2E = 1.4426950408889634
VMEM_LIMIT = 56 * 1024 * 1024

f32 = jnp.float32
bf16 = jnp.bfloat16
i32 = jnp.int32
HIGHEST = lax.Precision.HIGHEST
NT_DIMS = (((1,), (1,)), ((), ()))
TN_DIMS = (((0,), (0,)), ((), ()))


def _pick_tile(n, cap, align):
    t = (min(cap, n) // align) * align
    while t > align and n % t:
        t -= align
    assert t > 0 and n % t == 0, (n, cap, align)
    return t


def _params(sem, limit=VMEM_LIMIT):
    return pltpu.CompilerParams(dimension_semantics=sem, vmem_limit_bytes=limit)


def _sigmoid(x):
    return 1.0 / (1.0 + jnp.exp(-x))


def _log2(n):
    s = int(math.log2(n))
    assert 1 << s == n, n
    return s


def _rmsnorm_kernel(x_ref, g_ref, o_ref):
    x = x_ref[...]
    ms = jnp.mean(x * x, axis=-1, keepdims=True)
    o_ref[...] = (x * lax.rsqrt(ms + NORM_EPS) * g_ref[...]).astype(o_ref.dtype)


def _rmsnorm_bf16(x, g):
    n, d = x.shape
    return pl.pallas_call(
        _rmsnorm_kernel,
        grid=(n // ROW_BLOCK,),
        in_specs=[pl.BlockSpec((ROW_BLOCK, d), lambda i: (i, 0)),
                  pl.BlockSpec((1, d), lambda i: (0, 0))],
        out_specs=pl.BlockSpec((ROW_BLOCK, d), lambda i: (i, 0)),
        out_shape=jax.ShapeDtypeStruct((n, d), bf16),
        compiler_params=_params(("parallel",)),
        name="rmsnorm_bf16",
    )(x, g.reshape(1, d))


def _mm_kernel(a_ref, b_ref, o_ref):
    o_ref[...] = jnp.dot(a_ref[...], b_ref[...], preferred_element_type=f32)


def _mm_res_kernel(a_ref, b_ref, r_ref, o_ref):
    o_ref[...] = r_ref[...] + jnp.dot(a_ref[...], b_ref[...], preferred_element_type=f32)


def _matmul(a, b, res=None, name="matmul"):
    m, k = a.shape
    n = b.shape[1]
    tm = _pick_tile(m, 640, 128)
    tn = _pick_tile(n, 1024, 128)
    in_specs = [pl.BlockSpec((tm, k), lambda j, i: (i, 0)),
                pl.BlockSpec((k, tn), lambda j, i: (0, j))]
    args = [a, b]
    kern = _mm_kernel
    if res is not None:
        in_specs.append(pl.BlockSpec((tm, tn), lambda j, i: (i, j)))
        args.append(res)
        kern = _mm_res_kernel
    return pl.pallas_call(
        kern,
        grid=(n // tn, m // tm),
        in_specs=in_specs,
        out_specs=pl.BlockSpec((tm, tn), lambda j, i: (i, j)),
        out_shape=jax.ShapeDtypeStruct((m, n), f32),
        compiler_params=_params(("parallel", "parallel")),
        name=name,
    )(*args)


def _hgrn_static(c):
    levels = _log2(c)
    t = np.arange(c)
    cmat = np.zeros((2 * c, c), np.float32)
    cmat[:c] = t[None, :] <= t[:, None]
    cmat[c:] = 1.0
    lmat = np.zeros((levels * c, c), np.float32)
    lv = np.full((c, c), -1, np.int32)
    for l in range(levels):
        blk = c >> l
        half = blk // 2
        mid = (t // blk) * blk + half
        lmat[l * c:(l + 1) * c] = t[None, :] <= (mid - 1)[:, None]
        same = (t[:, None] // blk) == (t[None, :] // blk)
        up = ((t // half) % 2 == 1)[:, None]
        lo = ((t // half) % 2 == 0)[None, :]
        lv[same & up & lo] = l
    lv[t, t] = levels
    return cmat, lmat, lv, levels


def _split3(x):
    hi = x.astype(bf16)
    r = x - hi.astype(f32)
    mid = r.astype(bf16)
    lo = (r - mid.astype(f32)).astype(bf16)
    return jnp.concatenate([hi, mid, lo], axis=1)


def _sum3(y):
    w = y.shape[1] // 3
    return y[:, :w] + y[:, w:2 * w] + y[:, 2 * w:]


def _hgrn_kernel(*refs, rows, chunk, levels, n_valid, heads, has_init):
    if has_init:
        (zq_ref, zf_ref, zi_ref, zg_ref, lb_ref, gain_ref, cmat_ref, lmat_ref, lv_ref, s0_ref,
         oa_ref, sfin_ref, st_ref, a_ref, q_sc, k_sc, b_sc, p_sc) = refs
    else:
        (zq_ref, zf_ref, zi_ref, zg_ref, lb_ref, gain_ref, cmat_ref, lmat_ref, lv_ref,
         oa_ref, sfin_ref, st_ref, a_ref, q_sc, k_sc, b_sc, p_sc) = refs
        s0_ref = None
    tb = pl.program_id(2)
    c = chunk
    nc = rows // c
    dk = A_EXPAND

    @pl.when(tb == 0)
    def _():
        for hh in range(heads):
            if has_init:
                st_ref[hh] = s0_ref[0, hh].T
            else:
                st_ref[hh] = jnp.zeros(st_ref.shape[1:], f32)

    lv = lv_ref[...]
    cmat = cmat_ref[...]
    row_c = lax.broadcasted_iota(i32, (c, dk), 0)

    pre = []
    unsafe = None
    for hh in range(heads):
        ls = slice(hh * dk, (hh + 1) * dk)
        zq = zq_ref[:, ls]
        lb = lb_ref[:, ls]
        f = lb + (1.0 - lb) * _sigmoid(zf_ref[:, ls])
        logf = jnp.log(f)
        k = 1.0 - f
        q = zq * _sigmoid(zq)
        if n_valid < rows:
            live = lax.broadcasted_iota(i32, (rows, dk), 0) < n_valid
            logf = jnp.where(live, logf, 0.0)
            k = jnp.where(live, k, 0.0)
        parts = _split3(logf)
        bs, bls = [], []
        for ci in range(nc):
            y = _sum3(jnp.dot(cmat, parts[ci * c:(ci + 1) * c], preferred_element_type=f32))
            bs.append(y[:c])
            bls.append(y[c:])
        b = jnp.concatenate(bs, axis=0) if nc > 1 else bs[0]
        bl = jnp.concatenate(bls, axis=0) if nc > 1 else bls[0]
        head_unsafe = jnp.min(bl) < -HGRN_SAFE_DECAY
        unsafe = head_unsafe if unsafe is None else jnp.logical_or(unsafe, head_unsafe)
        pre.append(dict(q=q, k=k, b=b, bl=bl, parts=parts, qh_b=(q * jnp.exp(b)).astype(bf16),
                        kd=(k * jnp.exp(bl - b)).astype(bf16), v_b=zi_ref[:, ls].astype(bf16)))

    @pl.when(jnp.logical_not(unsafe))
    def _():
        for hh, hv in enumerate(pre):
            kt = (hv["k"] * jnp.exp(-hv["b"])).astype(bf16)
            for ci in range(nc):
                rs = slice(ci * c, (ci + 1) * c)
                al = lax.dot_general(hv["qh_b"][rs], kt[rs], NT_DIMS, preferred_element_type=f32)
                a_ref[hh * nc + ci] = jnp.where(lv >= 0, al, 0.0)

    @pl.when(unsafe)
    def _():
        for hh, hv in enumerate(pre):
            hr = slice(hh * rows, (hh + 1) * rows)
            q_sc[hr, :] = hv["q"]
            k_sc[hr, :] = hv["k"]
            b_sc[hr, :] = hv["b"]
            p_sc[hr, :] = hv["parts"]

        def safe_chunk(ci, carry):
            r0 = pl.multiple_of(ci * c, c)
            q_c = q_sc[pl.ds(r0, c), :]
            k_c = k_sc[pl.ds(r0, c), :]
            b_c = b_sc[pl.ds(r0, c), :]
            refs_b = _sum3(jnp.dot(lmat_ref[...], p_sc[pl.ds(r0, c), :], preferred_element_type=f32))
            a = jnp.where(lv == levels, jnp.sum(q_c * k_c, axis=-1, keepdims=True), 0.0)
            for l in range(levels):
                e = jnp.exp(-jnp.abs(b_c - refs_b[l * c:(l + 1) * c]))
                upper = ((row_c >> (levels - l - 1)) & 1) == 1
                qt = jnp.where(upper, q_c * e, 0.0).astype(bf16)
                kt = jnp.where(upper, 0.0, k_c * e).astype(bf16)
                al = lax.dot_general(qt, kt, NT_DIMS, preferred_element_type=f32)
                a = jnp.where(lv == l, al, a)
            a_ref[ci] = a
            return carry

        lax.fori_loop(0, heads * nc, safe_chunk, 0)

    for hh, hv in enumerate(pre):
        ls = slice(hh * dk, (hh + 1) * dk)
        gain = gain_ref[:, ls]
        zg = zg_ref[:, ls]
        st = st_ref[hh]
        for ci in range(nc):
            rs = slice(ci * c, (ci + 1) * c)
            o = (jnp.dot(a_ref[hh * nc + ci].astype(bf16), hv["v_b"][rs], preferred_element_type=f32)
                 + lax.dot_general(hv["qh_b"][rs], st.astype(bf16), NT_DIMS, preferred_element_type=f32))
            ms = jnp.mean(o * o, axis=-1, keepdims=True)
            zg_c = zg[rs]
            oa_ref[rs, ls] = o * lax.rsqrt(ms + NORM_EPS) * gain * (zg_c * _sigmoid(zg_c))
            st = (st * jnp.exp(hv["bl"][ci * c:ci * c + 1])
                  + lax.dot_general(hv["v_b"][rs], hv["kd"][rs], TN_DIMS, preferred_element_type=f32))
        st_ref[hh] = st

    @pl.when(tb == pl.num_programs(2) - 1)
    def _():
        for hh in range(heads):
            sfin_ref[0, hh] = st_ref[hh].T


def _hgrn(z_a, lb, gain, n_seq, seq_rows, row_block, chunk, n_valid, heads, s0=None):
    kd = A_HEADS * A_EXPAND
    dv = LANE
    assert z_a.shape[1] == 2 * kd + 2 * A_HEADS * dv and A_EXPAND == LANE and A_HEADS % heads == 0
    cmat, lmat, lv, levels = _hgrn_static(chunk)
    nrb = seq_rows // row_block
    hb = A_HEADS // heads
    wd = heads * LANE

    def zspec(seg):
        return pl.BlockSpec((row_block, wd), lambda b, h, t, seg=seg: (b * nrb + t, seg * hb + h))

    const = lambda b, h, t: (0, 0)
    in_specs = [zspec(0), zspec(1), zspec(2), zspec(3),
                pl.BlockSpec((1, wd), lambda b, h, t: (0, h)),
                pl.BlockSpec((1, wd), lambda b, h, t: (0, h)),
                pl.BlockSpec(cmat.shape, const), pl.BlockSpec(lmat.shape, const), pl.BlockSpec(lv.shape, const)]
    args = [z_a, z_a, z_a, z_a, lb.reshape(1, kd), gain.reshape(1, -1),
            jnp.asarray(cmat, bf16), jnp.asarray(lmat, bf16), jnp.asarray(lv)]
    if s0 is not None:
        in_specs.append(pl.BlockSpec((1, heads, A_EXPAND, dv), lambda b, h, t: (b, h, 0, 0)))
        args.append(s0)
    kern = functools.partial(_hgrn_kernel, rows=row_block, chunk=chunk, levels=levels, n_valid=n_valid,
                             heads=heads, has_init=s0 is not None)
    return pl.pallas_call(
        kern,
        grid=(n_seq, hb, nrb),
        in_specs=in_specs,
        out_specs=[pl.BlockSpec((row_block, wd), lambda b, h, t: (b * nrb + t, h)),
                   pl.BlockSpec((1, heads, A_EXPAND, dv), lambda b, h, t: (b, h, 0, 0))],
        out_shape=[jax.ShapeDtypeStruct((n_seq * seq_rows, A_HEADS * dv), f32),
                   jax.ShapeDtypeStruct((n_seq, A_HEADS, A_EXPAND, dv), f32)],
        scratch_shapes=[pltpu.VMEM((heads, dv, A_EXPAND), f32),
                        pltpu.VMEM((heads * row_block // chunk, chunk, chunk), f32),
                        pltpu.VMEM((heads * row_block, LANE), f32), pltpu.VMEM((heads * row_block, LANE), f32),
                        pltpu.VMEM((heads * row_block, LANE), f32),
                        pltpu.VMEM((heads * row_block, 3 * LANE), bf16)],
        compiler_params=_params(("parallel", "parallel", "arbitrary")),
        name="hgrn2_prompt" if s0 is None else "hgrn2_sample",
    )(*args)


def _sort_key(x):
    b = pltpu.bitcast(x + 0.0, i32)
    return b ^ ((b >> 31) & 0x7FFFFFFF)


def _kth_largest_key(key_ref, k, axis):
    shape = (1, key_ref.shape[1]) if axis == 0 else (key_ref.shape[0], 1)
    int_min = jnp.int32(-2 ** 31)

    n = key_ref.shape[axis]
    parts = COUNT_CHAINS if n % (COUNT_CHAINS * LANE) == 0 else 1
    seg = n // parts

    def step(i, lo):
        cand = lo ^ lax.shift_left(jnp.int32(1), 31 - i)
        cnt = None
        for pi in range(parts):
            sl = slice(pi * seg, (pi + 1) * seg)
            keys = key_ref[sl, :] if axis == 0 else key_ref[:, sl]
            c = jnp.sum((keys >= cand).astype(i32), axis=axis, keepdims=True)
            cnt = c if cnt is None else cnt + c
        return jnp.where(cnt >= k, cand, lo)

    return lax.fori_loop(0, 32, step, jnp.full(shape, int_min, i32))


def _dsa_prompt_kernel(qi_ref, smq_ref, smk_ref, q_ref, k_ref, v_ref, o_ref, mask_ref, key_ref, qs_ref,
                       *, kext, topk, qblk, group, qb0):
    qb = qb0 + pl.program_id(1)
    g = pl.program_id(2)
    n_tiles = kext // qblk

    def causal(r0):
        kpos = r0 + lax.broadcasted_iota(i32, (qblk, qblk), 0)
        qpos = qb * qblk + lax.broadcasted_iota(i32, (qblk, qblk), 1)
        return kpos <= qpos

    @pl.when(g == 0)
    def _():
        qs_ref[...] = qi_ref[...].astype(bf16)
        w_t = smq_ref[...].T * (IDX_HEADS ** -0.5 * IDX_DIM ** -0.5)

        def score_tile(ti, carry):
            r0 = pl.multiple_of(ti * qblk, qblk)
            kid = smk_ref[pl.ds(r0, qblk), 0:IDX_DIM].astype(bf16)
            acc = jnp.zeros((qblk, qblk), f32)
            for h in range(IDX_HEADS):
                s = lax.dot_general(kid, qs_ref[:, h * IDX_DIM:(h + 1) * IDX_DIM], NT_DIMS,
                                    preferred_element_type=f32)
                acc = acc + jnp.maximum(s, 0.0) * w_t[IDX_DIM + h:IDX_DIM + h + 1, :]
            key_ref[pl.ds(r0, qblk), :] = _sort_key(jnp.where(causal(r0), acc, -jnp.inf))
            return carry

        lax.fori_loop(0, n_tiles, score_tile, 0)
        thr = _kth_largest_key(key_ref, topk, axis=0)
        keys = key_ref[...]
        kpos = lax.broadcasted_iota(i32, (kext, qblk), 0)
        qpos = qb * qblk + lax.broadcasted_iota(i32, (kext, qblk), 1)
        mask_ref[...] = jnp.where(jnp.logical_and(kpos <= qpos, keys >= thr), 0.0, -jnp.inf)
        need = topk - jnp.sum((keys > thr).astype(i32), axis=0, keepdims=True)
        n_tie = jnp.sum((keys == thr).astype(i32), axis=0, keepdims=True)

        @pl.when(jnp.max(n_tie - need) > 0)
        def _():
            strict = (lax.broadcasted_iota(i32, (qblk, qblk), 0) > lax.broadcasted_iota(i32, (qblk, qblk), 1))
            strict = jnp.where(strict, 1.0, 0.0).astype(bf16)
            need_f = need.astype(f32)

            def tile(ti, seen):
                r0 = pl.multiple_of(ti * qblk, qblk)
                kt = key_ref[pl.ds(r0, qblk), :]
                tie = kt == thr
                tie_f = jnp.where(tie, 1.0, 0.0)
                before = seen + jnp.dot(strict, tie_f.astype(bf16), preferred_element_type=f32)
                keep = jnp.logical_or(kt > thr, jnp.logical_and(tie, before < need_f))
                keep = jnp.logical_and(keep, causal(r0))
                mask_ref[pl.ds(r0, qblk), :] = jnp.where(keep, 0.0, -jnp.inf)
                return seen + jnp.sum(tie_f, axis=0, keepdims=True)

            lax.fori_loop(0, n_tiles, tile, jnp.zeros((1, qblk), f32))

    kk = k_ref[...].astype(bf16)
    vt = v_ref[...].T.astype(bf16)
    dh = kk.shape[1]
    for hq in range(group):
        qh = (q_ref[:, hq * dh:(hq + 1) * dh] * (dh ** -0.5 * LOG2E)).astype(bf16)
        s = lax.dot_general(kk, qh, NT_DIMS, preferred_element_type=f32) + mask_ref[...]
        m = jnp.max(s, axis=0, keepdims=True)
        p = jnp.exp2(s - m)
        den = jnp.sum(p, axis=0, keepdims=True)
        o_t = jnp.dot(vt, p.astype(bf16), preferred_element_type=f32) / den
        o_ref[:, hq * dh:(hq + 1) * dh] = o_t.T


def _dsa_prompt(bqi, small, bq, bkv, n_batch, seq):
    dh = bq.shape[1] // B_HEADS
    group = B_HEADS // B_KV_HEADS
    qblk = LANE
    nqb = seq // qblk
    topk = min(TOPK_MAX, seq // 4)
    parts = []
    qb0 = 0
    kext = 2 * qblk
    while qb0 < nqb:
        kext = min(kext, seq)
        assert seq % kext == 0
        count = kext // qblk - qb0
        nkb = seq // kext
        kern = functools.partial(_dsa_prompt_kernel, kext=kext, topk=topk, qblk=qblk, group=group, qb0=qb0)
        part = pl.pallas_call(
            kern,
            grid=(n_batch, count, B_KV_HEADS),
            in_specs=[pl.BlockSpec((qblk, bqi.shape[1]), lambda b, j, g, q0=qb0: (b * nqb + q0 + j, 0)),
                      pl.BlockSpec((qblk, small.shape[1]), lambda b, j, g, q0=qb0: (b * nqb + q0 + j, 0)),
                      pl.BlockSpec((kext, small.shape[1]), lambda b, j, g, nk=nkb: (b * nk, 0)),
                      pl.BlockSpec((qblk, group * dh), lambda b, j, g, q0=qb0: (b * nqb + q0 + j, g)),
                      pl.BlockSpec((kext, dh), lambda b, j, g, nk=nkb: (b * nk, g)),
                      pl.BlockSpec((kext, dh), lambda b, j, g, nk=nkb: (b * nk, B_KV_HEADS + g))],
            out_specs=pl.BlockSpec((qblk, group * dh), lambda b, j, g, cnt=count: (b * cnt + j, g)),
            out_shape=jax.ShapeDtypeStruct((n_batch * count * qblk, B_HEADS * dh), f32),
            scratch_shapes=[pltpu.VMEM((kext, qblk), f32), pltpu.VMEM((kext, qblk), i32),
                            pltpu.VMEM((qblk, bqi.shape[1]), bf16)],
            compiler_params=_params(("parallel", "parallel", "arbitrary")),
            name=f"dsa_prompt_k{kext}",
        )(bqi, small, small, bq, bkv, bkv)
        parts.append(part.reshape(n_batch, count * qblk, B_HEADS * dh))
        qb0 += count
        kext *= 2
    return jnp.concatenate(parts, axis=1).reshape(n_batch * seq, B_HEADS * dh)


def _dsa_sample_select_kernel(pt_ref, qi_ref, wsel_ref, knew_ref, cik_hbm, mask_ref,
                              slab_ref, sc_ref, key_ref, sel_ref, sem, *, n_pages, page, topk, n_tok, kvh):
    b = pl.program_id(0)
    n_seq = pl.num_programs(0)
    past = n_pages * page

    def page_copy(seq, slot, j):
        return pltpu.make_async_copy(cik_hbm.at[0, pt_ref[seq, j]],
                                     slab_ref.at[slot, pl.ds(pl.multiple_of(j * page, page), page), :],
                                     sem.at[slot])

    def start(seq, slot):
        lax.fori_loop(0, n_pages, lambda j, c: (page_copy(seq, slot, j).start(), c)[1], 0)

    def wait(seq, slot):
        lax.fori_loop(0, n_pages, lambda j, c: (page_copy(seq, slot, j).wait(), c)[1], 0)

    @pl.when(b == 0)
    def _():
        start(0, 0)

    slot = b % 2

    @pl.when(b + 1 < n_seq)
    def _():
        start(b + 1, 1 - slot)

    wait(b, slot)
    qi = qi_ref[0].astype(bf16)
    wsel = wsel_ref[0]

    def tile_scores(keys):
        s = lax.dot_general(qi, keys.astype(bf16), NT_DIMS, preferred_element_type=f32)
        return jnp.dot(wsel, jnp.maximum(s, 0.0), precision=HIGHEST, preferred_element_type=f32)

    sc_ref[:, 0:past] = tile_scores(slab_ref[slot])
    new = tile_scores(knew_ref[0])
    r = lax.broadcasted_iota(i32, new.shape, 0)
    cidx = lax.broadcasted_iota(i32, new.shape, 1)
    sc_ref[:, past:past + LANE] = jnp.where(jnp.logical_and(cidx <= r, cidx < n_tok), new, -jnp.inf)
    sc = sc_ref[...]
    key_ref[...] = _sort_key(sc)
    thr = _kth_largest_key(key_ref, topk, axis=1)
    keys = key_ref[...]
    sel = jnp.logical_and(keys >= thr, sc > -jnp.inf)
    sel_ref[...] = jnp.where(sel, 1.0, 0.0)
    need = topk - jnp.sum((keys > thr).astype(i32), axis=1, keepdims=True)
    n_tie = jnp.sum((keys == thr).astype(i32), axis=1, keepdims=True)

    @pl.when(jnp.max(n_tie - need) > 0)
    def _():
        strict = (lax.broadcasted_iota(i32, (LANE, LANE), 0) < lax.broadcasted_iota(i32, (LANE, LANE), 1))
        strict = jnp.where(strict, 1.0, 0.0).astype(bf16)
        need_f = need.astype(f32)

        def tile(ti, seen):
            c1 = pl.multiple_of(ti * LANE, LANE)
            kt = key_ref[:, pl.ds(c1, LANE)]
            tie = kt == thr
            tie_f = jnp.where(tie, 1.0, 0.0)
            before = seen + jnp.dot(tie_f.astype(bf16), strict, preferred_element_type=f32)
            keep = jnp.logical_or(kt > thr, jnp.logical_and(tie, before < need_f))
            keep = jnp.logical_and(keep, sc_ref[:, pl.ds(c1, LANE)] > -jnp.inf)
            sel_ref[:, pl.ds(c1, LANE)] = jnp.where(keep, 1.0, 0.0)
            return seen + jnp.sum(tie_f, axis=1, keepdims=True)

        lax.fori_loop(0, sc_ref.shape[1] // LANE, tile, jnp.zeros((8, 1), f32))

    lane = lax.broadcasted_iota(i32, (8, LANE), 1)

    def expand(ti, carry):
        c1 = pl.multiple_of(ti * LANE, LANE)
        hit = sel_ref[:, pl.ds(c1, LANE)]
        for j in range(kvh):
            src = j * (LANE // kvh) + (lane >> _log2(kvh))
            piece = jnp.take_along_axis(hit, src, axis=1)
            c2 = pl.multiple_of(ti * (LANE * kvh) + j * LANE, LANE)
            mask_ref[0, :, pl.ds(c2, LANE)] = jnp.where(piece > 0.5, 0.0, NEG_BIG)
        return carry

    lax.fori_loop(0, sc_ref.shape[1] // LANE, expand, 0)


def _dsa_sample_attn_kernel(pt_ref, q_ref, mask_ref, masknew_ref, knew_ref, vnew_ref, gm_ref, ck_hbm, cv_hbm,
                            o_ref, kslab, vslab, m_ref, l_ref, acc_ref, sem,
                            *, n_chunks, ch, n_tok, kvh, group, dh, page):
    b = pl.program_id(0)
    c = pl.program_id(1)
    t = b * n_chunks + c
    total = pl.num_programs(0) * n_chunks
    scale = dh ** -0.5

    def copies(seq, chunk, slot, j):
        pid = pt_ref[seq, chunk * ch + j]
        return (pltpu.make_async_copy(ck_hbm.at[0, pid], kslab.at[slot, j], sem.at[slot, 0]),
                pltpu.make_async_copy(cv_hbm.at[0, pid], vslab.at[slot, j], sem.at[slot, 1]))

    def start(seq, chunk, slot):
        for j in range(ch):
            ck, cv = copies(seq, chunk, slot, j)
            ck.start()
            cv.start()

    def wait(seq, chunk, slot):
        for j in range(ch):
            ck, cv = copies(seq, chunk, slot, j)
            ck.wait()
            cv.wait()

    @pl.when(t == 0)
    def _():
        start(0, 0, 0)

    slot = t % 2

    @pl.when(t + 1 < total)
    def _():
        nxt = t + 1
        start(nxt // n_chunks, nxt % n_chunks, 1 - slot)

    wait(b, c, slot)

    @pl.when(c == 0)
    def _():
        m_ref[...] = jnp.full_like(m_ref, NEG_BIG)
        l_ref[...] = jnp.zeros_like(l_ref)
        acc_ref[...] = jnp.zeros_like(acc_ref)

    qb = (q_ref[0] * scale).astype(bf16)
    heads = kvh * group

    def attend(keys, vals, mask8):
        n_cols = keys.shape[0]
        tok_mask = jnp.concatenate([jnp.broadcast_to(mask8[tk:tk + 1], (heads, n_cols))
                                    for tk in range(n_tok)], axis=0)
        madd = tok_mask + gm_ref[:, 0:n_cols]
        s = lax.dot_general(qb, keys, NT_DIMS, preferred_element_type=f32) + madd
        m_old = m_ref[...]
        m_new = jnp.maximum(m_old, jnp.max(s, axis=-1, keepdims=True))
        pr = jnp.where(madd == 0.0, jnp.exp(s - m_new), 0.0)
        alpha = jnp.exp(m_old - m_new)
        l_ref[...] = alpha * l_ref[...] + jnp.sum(pr, axis=-1, keepdims=True)
        acc_ref[...] = alpha * acc_ref[...] + jnp.dot(pr.astype(bf16), vals, preferred_element_type=f32)
        m_ref[...] = m_new

    attend(kslab[slot].reshape(ch * page * kvh, dh).astype(bf16),
           vslab[slot].reshape(ch * page * kvh, dh).astype(bf16), mask_ref[0])

    @pl.when(c == n_chunks - 1)
    def _():
        attend(knew_ref[0].astype(bf16), vnew_ref[0].astype(bf16), masknew_ref[0])
        o_ref[0] = acc_ref[...] / l_ref[...]


def _dsa_sample(qi, wsel, q, knew_idx, k_new, v_new, cache_k, cache_v, cache_idx_k, page_table, n_tok):
    n_seq, n_pages = page_table.shape
    page, kvh, dh = cache_k.shape[2], cache_k.shape[3], cache_k.shape[4]
    assert page == LANE and LANE % kvh == 0
    rows_i = qi.shape[1]
    rows = q.shape[1]
    heads = rows // n_tok
    group = heads // kvh
    past = n_pages * page
    kp = past + LANE
    topk = min(TOPK_MAX, (past + n_tok) // 4)
    ch = math.gcd(SAMPLE_PAGES_PER_STEP, n_pages)
    n_chunks = n_pages // ch
    cw = ch * page * kvh
    own = (np.arange(rows)[:, None] % heads) // group == (np.arange(cw)[None, :] % kvh)
    gm = np.where(own, 0.0, NEG_BIG).astype(np.float32)

    mask = pl.pallas_call(
        functools.partial(_dsa_sample_select_kernel, n_pages=n_pages, page=page, topk=topk, n_tok=n_tok,
                          kvh=kvh),
        grid_spec=pltpu.PrefetchScalarGridSpec(
            num_scalar_prefetch=1,
            grid=(n_seq,),
            in_specs=[pl.BlockSpec((1, rows_i, IDX_DIM), lambda b, pt: (b, 0, 0)),
                      pl.BlockSpec((1, 8, rows_i), lambda b, pt: (b, 0, 0)),
                      pl.BlockSpec((1, LANE, IDX_DIM), lambda b, pt: (b, 0, 0)),
                      pl.BlockSpec(memory_space=pl.ANY)],
            out_specs=pl.BlockSpec((1, 8, kp * kvh), lambda b, pt: (b, 0, 0)),
            scratch_shapes=[pltpu.VMEM((2, past, IDX_DIM), f32), pltpu.VMEM((8, kp), f32),
                            pltpu.VMEM((8, kp), i32), pltpu.VMEM((8, kp), f32), pltpu.SemaphoreType.DMA((2,))]),
        out_shape=jax.ShapeDtypeStruct((n_seq, 8, kp * kvh), f32),
        compiler_params=_params(("arbitrary",)),
        name="dsa_sample_select",
    )(page_table, qi, wsel, knew_idx, cache_idx_k)

    return pl.pallas_call(
        functools.partial(_dsa_sample_attn_kernel, n_chunks=n_chunks, ch=ch, n_tok=n_tok, kvh=kvh, group=group,
                          dh=dh, page=page),
        grid_spec=pltpu.PrefetchScalarGridSpec(
            num_scalar_prefetch=1,
            grid=(n_seq, n_chunks),
            in_specs=[pl.BlockSpec((1, rows, dh), lambda b, c, pt: (b, 0, 0)),
                      pl.BlockSpec((1, 8, cw), lambda b, c, pt: (b, 0, c)),
                      pl.BlockSpec((1, 8, LANE * kvh), lambda b, c, pt: (b, 0, past // LANE)),
                      pl.BlockSpec((1, LANE * kvh, dh), lambda b, c, pt: (b, 0, 0)),
                      pl.BlockSpec((1, LANE * kvh, dh), lambda b, c, pt: (b, 0, 0)),
                      pl.BlockSpec(gm.shape, lambda b, c, pt: (0, 0)),
                      pl.BlockSpec(memory_space=pl.ANY),
                      pl.BlockSpec(memory_space=pl.ANY)],
            out_specs=pl.BlockSpec((1, rows, dh), lambda b, c, pt: (b, 0, 0)),
            scratch_shapes=[pltpu.VMEM((2, ch, page, kvh, dh), f32), pltpu.VMEM((2, ch, page, kvh, dh), f32),
                            pltpu.VMEM((rows, 1), f32), pltpu.VMEM((rows, 1), f32),
                            pltpu.VMEM((rows, dh), f32), pltpu.SemaphoreType.DMA((2, 2))]),
        out_shape=jax.ShapeDtypeStruct((n_seq, rows, dh), f32),
        compiler_params=_params(("arbitrary", "arbitrary")),
        name="dsa_sample_attn",
    )(page_table, q, mask, mask, k_new, v_new, jnp.asarray(gm), cache_k, cache_v)


def _merge_kernel(ga_ref, gb_ref, oa_ref, ob_ref, o_ref):
    o_ref[...] = (_sigmoid(ga_ref[...]) * oa_ref[...] + _sigmoid(gb_ref[...]) * ob_ref[...]).astype(o_ref.dtype)


def _merge(gates, oa, ob):
    n, d = oa.shape
    tn = _pick_tile(d, 1024, 128)
    nj = d // tn
    return pl.pallas_call(
        _merge_kernel,
        grid=(n // ROW_BLOCK, nj),
        in_specs=[pl.BlockSpec((ROW_BLOCK, tn), lambda i, j: (i, j)),
                  pl.BlockSpec((ROW_BLOCK, tn), lambda i, j: (i, nj + j)),
                  pl.BlockSpec((ROW_BLOCK, tn), lambda i, j: (i, j)),
                  pl.BlockSpec((ROW_BLOCK, tn), lambda i, j: (i, j))],
        out_specs=pl.BlockSpec((ROW_BLOCK, tn), lambda i, j: (i, j)),
        out_shape=jax.ShapeDtypeStruct((n, d), bf16),
        compiler_params=_params(("parallel", "parallel")),
        name="merge_branches",
    )(gates, gates, oa, ob)


def _router_kernel(x_ref, g_ref, w_ref, b_ref, route_ref, cnt_ref, carry_ref):
    i = pl.program_id(0)

    @pl.when(i == 0)
    def _():
        carry_ref[...] = jnp.zeros_like(carry_ref)

    x = x_ref[...]
    ms = jnp.mean(x * x, axis=-1, keepdims=True)
    h = x * lax.rsqrt(ms + NORM_EPS) * g_ref[...]
    logits = jnp.dot(h, w_ref[...], precision=HIGHEST, preferred_element_type=f32) + b_ref[...]
    lane = lax.broadcasted_iota(i32, logits.shape, 1)
    big = jnp.int32(1 << 20)
    ng, epg = N_GROUPS, EXPERTS_PER_GROUP

    def first_max(vals):
        mx = jnp.max(vals, axis=-1, keepdims=True)
        idx = jnp.min(jnp.where(vals == mx, lane, big), axis=-1, keepdims=True)
        return mx, idx

    lg = jnp.where(lane < ng, logits, -jnp.inf)
    mg, g_sel = first_max(lg)
    p_g = 1.0 / jnp.sum(jnp.exp(lg - mg), axis=-1, keepdims=True)
    lo = ng + g_sel * epg
    le = jnp.where(jnp.logical_and(lane >= lo, lane < lo + epg), logits, -jnp.inf)
    m1, i1 = first_max(le)
    se = jnp.sum(jnp.exp(le - m1), axis=-1, keepdims=True)
    m2, i2 = first_max(jnp.where(lane == i1, -jnp.inf, le))
    p1 = 1.0 / se
    p2 = jnp.exp(m2 - m1) / se
    w1 = p_g * p1 / (p1 + p2)
    w2 = p_g * p2 / (p1 + p2)

    hot = jnp.logical_or(lane == i1, lane == i2)
    hot_b = jnp.where(hot, 1.0, 0.0).astype(bf16)
    n = x.shape[0]
    strict = (lax.broadcasted_iota(i32, (n, n), 0) > lax.broadcasted_iota(i32, (n, n), 1))
    before = jnp.dot(jnp.where(strict, 1.0, 0.0).astype(bf16), hot_b, preferred_element_type=f32)
    before = before + carry_ref[...]
    r1 = jnp.sum(jnp.where(lane == i1, before, 0.0), axis=-1, keepdims=True)
    r2 = jnp.sum(jnp.where(lane == i2, before, 0.0), axis=-1, keepdims=True)
    carry_ref[...] = carry_ref[...] + jnp.sum(jnp.where(hot, 1.0, 0.0), axis=0, keepdims=True)

    out = jnp.where(lane == 0, (i1 - ng).astype(f32), 0.0)
    out = jnp.where(lane == 1, (i2 - ng).astype(f32), out)
    out = jnp.where(lane == 2, w1, out)
    out = jnp.where(lane == 3, w2, out)
    out = jnp.where(lane == 4, r1, out)
    out = jnp.where(lane == 5, r2, out)
    route_ref[...] = out
    cnt_ref[...] = carry_ref[...]


def _router(x1, g, w_r, b_r):
    n, d = x1.shape
    return pl.pallas_call(
        _router_kernel,
        grid=(n // ROW_BLOCK,),
        in_specs=[pl.BlockSpec((ROW_BLOCK, d), lambda i: (i, 0)),
                  pl.BlockSpec((1, d), lambda i: (0, 0)),
                  pl.BlockSpec((d, LANE), lambda i: (0, 0)),
                  pl.BlockSpec((1, LANE), lambda i: (0, 0))],
        out_specs=[pl.BlockSpec((ROW_BLOCK, LANE), lambda i: (i, 0)),
                   pl.BlockSpec((1, LANE), lambda i: (0, 0))],
        out_shape=[jax.ShapeDtypeStruct((n, LANE), f32), jax.ShapeDtypeStruct((1, LANE), f32)],
        scratch_shapes=[pltpu.VMEM((1, LANE), f32)],
        compiler_params=_params(("arbitrary",)),
        name="moe_router",
    )(x1, g.reshape(1, d), w_r, b_r)


def _row_copy(src_hbm, dst_vmem, sem, src_row, dst_row):
    return pltpu.make_async_copy(src_hbm.at[pl.ds(src_row, 1), :], dst_vmem.at[pl.ds(dst_row, 1), :], sem)


def _dispatch_kernel(tok_ref, x_hbm, g_ref, o_ref, buf_ref, sem):
    i = pl.program_id(0)
    n = pl.num_programs(0)

    def start(blk, slot):
        def body(r8, c):
            for u in range(DMA_ISSUE_UNROLL):
                r = r8 * DMA_ISSUE_UNROLL + u
                _row_copy(x_hbm, buf_ref.at[slot], sem.at[slot], tok_ref[blk * MOE_BLOCK + r], r).start(
                    priority=u % 2)
            return c
        lax.fori_loop(0, MOE_BLOCK // DMA_ISSUE_UNROLL, body, 0)

    def wait(slot):
        for r in range(MOE_BLOCK):
            _row_copy(x_hbm, buf_ref.at[slot], sem.at[slot], 0, r).wait()

    @pl.when(i == 0)
    def _():
        start(0, 0)

    slot = i % 2

    @pl.when(i + 1 < n)
    def _():
        start(i + 1, 1 - slot)

    wait(slot)
    x = buf_ref[slot]
    ms = jnp.mean(x * x, axis=-1, keepdims=True)
    o_ref[...] = (x * lax.rsqrt(ms + NORM_EPS) * g_ref[...]).astype(o_ref.dtype)


def _dispatch(row_tok, x1, g):
    n, d = x1.shape
    rows = row_tok.shape[0]
    return pl.pallas_call(
        _dispatch_kernel,
        grid_spec=pltpu.PrefetchScalarGridSpec(
            num_scalar_prefetch=1,
            grid=(rows // MOE_BLOCK,),
            in_specs=[pl.BlockSpec(memory_space=pl.ANY),
                      pl.BlockSpec((1, d), lambda i, tok: (0, 0))],
            out_specs=pl.BlockSpec((MOE_BLOCK, d), lambda i, tok: (i, 0)),
            scratch_shapes=[pltpu.VMEM((2, MOE_BLOCK, d), f32), pltpu.SemaphoreType.DMA((2,))]),
        out_shape=jax.ShapeDtypeStruct((rows, d), bf16),
        compiler_params=_params(("arbitrary",)),
        name="moe_dispatch",
    )(row_tok, x1, g.reshape(1, d))


def _expert_ffn_kernel(ie_ref, ir_ref, in_ref, nit_ref, xs_hbm, *refs, n_ff, split):
    wg_refs, wu_refs, wd_refs = refs[:split], refs[split:2 * split], refs[2 * split:3 * split]
    y_hbm, x_ref, acc_ref, sem_in, sem_out = refs[3 * split:]
    it = pl.program_id(0)
    ff = pl.program_id(1)
    nblk = in_ref[it]
    row0 = pl.multiple_of(ir_ref[it] * MOE_BLOCK, MOE_BLOCK)
    span = MOE_ITEM_BLOCKS * MOE_BLOCK
    dq = x_ref.shape[1] // split

    @pl.when(nblk > 0)
    def _():
        @pl.when(ff == 0)
        def _():
            cp = pltpu.make_async_copy(xs_hbm.at[pl.ds(row0, span), :], x_ref, sem_in)
            cp.start()
            cp.wait()

        gate = up = None
        for q in range(split):
            xq = x_ref[:, q * dq:(q + 1) * dq]
            g_q = jnp.dot(xq, wg_refs[q][0].astype(bf16), preferred_element_type=f32)
            u_q = jnp.dot(xq, wu_refs[q][0].astype(bf16), preferred_element_type=f32)
            gate = g_q if gate is None else gate + g_q
            up = u_q if up is None else up + u_q
        act = (gate * _sigmoid(gate) * up).astype(bf16)
        ys = [jnp.dot(act, wd_refs[q][0].astype(bf16), preferred_element_type=f32) for q in range(split)]

        @pl.when(ff == 0)
        def _():
            for q in range(split):
                acc_ref[:, q * dq:(q + 1) * dq] = ys[q]

        @pl.when(ff > 0)
        def _():
            for q in range(split):
                acc_ref[:, q * dq:(q + 1) * dq] += ys[q]

        @pl.when(ff == n_ff - 1)
        def _():
            for sb in range(MOE_ITEM_BLOCKS):
                @pl.when(sb < nblk)
                def _(sb=sb):
                    rs = pl.ds(sb * MOE_BLOCK, MOE_BLOCK)
                    cp = pltpu.make_async_copy(acc_ref.at[rs, :],
                                               y_hbm.at[pl.ds(row0 + sb * MOE_BLOCK, MOE_BLOCK), :], sem_out)
                    cp.start()
                    cp.wait()

    @pl.when(jnp.logical_and(it == pl.num_programs(0) - 1, ff == n_ff - 1))
    def _():
        acc_ref[0:MOE_BLOCK, :] = jnp.zeros((MOE_BLOCK, acc_ref.shape[1]), f32)

        def fill(blk, c):
            cp = pltpu.make_async_copy(acc_ref.at[pl.ds(0, MOE_BLOCK), :],
                                       y_hbm.at[pl.ds(pl.multiple_of(blk * MOE_BLOCK, MOE_BLOCK), MOE_BLOCK), :],
                                       sem_out)
            cp.start()
            cp.wait()
            return c

        lax.fori_loop(nit_ref[1], y_hbm.shape[0] // MOE_BLOCK, fill, 0)


def _expert_ffn(item_e, item_r, item_n, n_items, xs, w_gate, w_up, w_down, max_items):
    rows, d = xs.shape
    dff = w_gate.shape[3]
    tf = _pick_tile(dff, MOE_FF_TILE, 128)
    n_ff = dff // tf
    span = MOE_ITEM_BLOCKS * MOE_BLOCK
    split = min(MOE_WEIGHT_SPLIT, d // LANE)
    dq = d // split
    assert dq % LANE == 0 and dq * split == d

    def f_eff(i, f, nit):
        return jnp.where(i < nit[0], f, n_ff - 1)

    def up_spec(q):
        return pl.BlockSpec((None, 1, dq, tf), lambda i, f, ie, ir, inn, nit: (0, ie[i], q, f_eff(i, f, nit)))

    def down_spec(q):
        return pl.BlockSpec((None, 1, tf, dq), lambda i, f, ie, ir, inn, nit: (0, ie[i], f_eff(i, f, nit), q))

    return pl.pallas_call(
        functools.partial(_expert_ffn_kernel, n_ff=n_ff, split=split),
        grid_spec=pltpu.PrefetchScalarGridSpec(
            num_scalar_prefetch=4,
            grid=(max_items, n_ff),
            in_specs=([pl.BlockSpec(memory_space=pl.ANY)] + [up_spec(q) for q in range(split)]
                      + [up_spec(q) for q in range(split)] + [down_spec(q) for q in range(split)]),
            out_specs=pl.BlockSpec(memory_space=pl.ANY),
            scratch_shapes=[pltpu.VMEM((span, d), bf16), pltpu.VMEM((span, d), f32),
                            pltpu.SemaphoreType.DMA(()), pltpu.SemaphoreType.DMA(())]),
        out_shape=jax.ShapeDtypeStruct((rows, d), f32),
        compiler_params=_params(("arbitrary", "arbitrary")),
        name="moe_expert_ffn",
    )(item_e, item_r, item_n, n_items, xs, *([w_gate] * split), *([w_up] * split), *([w_down] * split))


def _combine_kernel(d1_ref, d2_ref, y_hbm, x_ref, route_ref, g_ref, o_ref, buf_ref, sem):
    i = pl.program_id(0)
    n = pl.num_programs(0)

    def start(blk, slot):
        def body(r8, c):
            for u in range(DMA_ISSUE_UNROLL):
                r = r8 * DMA_ISSUE_UNROLL + u
                _row_copy(y_hbm, buf_ref.at[slot, 0], sem.at[slot], d1_ref[blk * ROW_BLOCK + r], r).start(priority=0)
                _row_copy(y_hbm, buf_ref.at[slot, 1], sem.at[slot], d2_ref[blk * ROW_BLOCK + r], r).start(priority=1)
            return c
        lax.fori_loop(0, ROW_BLOCK // DMA_ISSUE_UNROLL, body, 0)

    def wait(slot):
        for r in range(ROW_BLOCK):
            _row_copy(y_hbm, buf_ref.at[slot, 0], sem.at[slot], 0, r).wait()
            _row_copy(y_hbm, buf_ref.at[slot, 1], sem.at[slot], 0, r).wait()

    @pl.when(i == 0)
    def _():
        start(0, 0)

    slot = i % 2

    @pl.when(i + 1 < n)
    def _():
        start(i + 1, 1 - slot)

    wait(slot)
    route = route_ref[...]
    x = x_ref[...] + route[:, 2:3] * buf_ref[slot, 0] + route[:, 3:4] * buf_ref[slot, 1]
    ms = jnp.mean(x * x, axis=-1, keepdims=True)
    o_ref[...] = x * lax.rsqrt(ms + NORM_EPS) * g_ref[...]


def _combine(d1, d2, yb, x1, route, g):
    n, d = x1.shape
    return pl.pallas_call(
        _combine_kernel,
        grid_spec=pltpu.PrefetchScalarGridSpec(
            num_scalar_prefetch=2,
            grid=(n // ROW_BLOCK,),
            in_specs=[pl.BlockSpec(memory_space=pl.ANY),
                      pl.BlockSpec((ROW_BLOCK, d), lambda i, a, b: (i, 0)),
                      pl.BlockSpec((ROW_BLOCK, LANE), lambda i, a, b: (i, 0)),
                      pl.BlockSpec((1, d), lambda i, a, b: (0, 0))],
            out_specs=pl.BlockSpec((ROW_BLOCK, d), lambda i, a, b: (i, 0)),
            scratch_shapes=[pltpu.VMEM((2, 2, ROW_BLOCK, d), f32), pltpu.SemaphoreType.DMA((2,))]),
        out_shape=jax.ShapeDtypeStruct((n, d), f32),
        compiler_params=_params(("arbitrary",)),
        name="moe_combine",
    )(d1, d2, yb, x1, route, g.reshape(1, d))


def _moe(x1, norm_ffn, w_rg, b_rg, w_re, b_re, w_gate, w_up, w_down, norm_final):
    n, d = x1.shape
    n_exp = w_gate.shape[1]
    ng = w_rg.shape[1]
    w_r = jnp.zeros((d, LANE), f32).at[:, :ng].set(w_rg).at[:, ng:ng + n_exp].set(w_re)
    b_r = jnp.zeros((1, LANE), f32).at[0, :ng].set(b_rg).at[0, ng:ng + n_exp].set(b_re)
    route, cnt = _router(x1, norm_ffn, w_r, b_r)

    e1 = route[:, 0].astype(i32)
    e2 = route[:, 1].astype(i32)
    counts = cnt[0, ng:ng + n_exp].astype(i32)
    nblk_e = (counts + MOE_BLOCK - 1) // MOE_BLOCK
    blk_start = jnp.cumsum(nblk_e) - nblk_e
    d1 = blk_start[e1] * MOE_BLOCK + route[:, 4].astype(i32)
    d2 = blk_start[e2] * MOE_BLOCK + route[:, 5].astype(i32)
    max_blocks = -(-2 * n // MOE_BLOCK) + n_exp
    rows = (max_blocks + MOE_ITEM_BLOCKS) * MOE_BLOCK
    tok = jnp.arange(n, dtype=i32)
    row_tok = jnp.zeros((rows,), i32).at[d1].set(tok).at[d2].set(tok)
    items_e = (nblk_e + MOE_ITEM_BLOCKS - 1) // MOE_ITEM_BLOCKS
    item_end = jnp.cumsum(items_e)
    n_items = item_end[-1]
    max_items = -(-max_blocks // MOE_ITEM_BLOCKS) + n_exp
    it = jnp.arange(max_items, dtype=i32)
    it_c = jnp.minimum(it, n_items - 1)
    item_e = jnp.minimum(jnp.searchsorted(item_end, it_c, side="right"), n_exp - 1).astype(i32)
    local = it_c - (item_end - items_e)[item_e]
    item_r = (blk_start[item_e] + local * MOE_ITEM_BLOCKS).astype(i32)
    item_n = jnp.where(it < n_items, jnp.minimum(MOE_ITEM_BLOCKS, nblk_e[item_e] - local * MOE_ITEM_BLOCKS), 0)

    xs = _dispatch(row_tok, x1, norm_ffn)
    meta = jnp.stack([n_items, jnp.sum(nblk_e)]).astype(i32)
    yb = _expert_ffn(item_e, item_r, item_n.astype(i32), meta, xs, w_gate, w_up, w_down, max_items)
    return _combine(d1, d2, yb, x1, route, norm_final)


def kernel(x_prompt, x_sample, cache_k, cache_v, cache_idx_k, state_hgrn, page_table, norm_mix, w_in,
           lb_logits, norm_hgrn_out, w_out, norm_ffn, router_group, router_group_bias, router_expert,
           router_expert_bias, expert_w_gate, expert_w_up, expert_w_down, norm_final):
    n_b, seq, d = x_prompt.shape
    n_s, n_tok, _ = x_sample.shape
    depth = norm_mix.shape[0]
    assert depth == 1
    kd = A_HEADS * A_EXPAND
    dh = d // B_HEADS
    kvd = B_KV_HEADS * dh
    group = B_HEADS // B_KV_HEADS
    np_rows = n_b * seq
    ns_rows = n_s * n_tok
    n_all = np_rows + ns_rows
    n_pad = -(-n_all // ROW_BLOCK) * ROW_BLOCK
    assert np_rows % ROW_BLOCK == 0

    lower = jnp.cumsum(jax.nn.softmax(lb_logits.astype(f32), axis=0), axis=0)[0]
    x_all = jnp.concatenate([x_prompt.reshape(np_rows, d), x_sample.reshape(ns_rows, d),
                             jnp.zeros((n_pad - n_all, d), f32)], axis=0)

    w = w_in[0]
    o_bq = 4 * kd
    o_bk = o_bq + d
    o_bqi = o_bk + 2 * kvd
    o_sm = o_bqi + IDX_HEADS * IDX_DIM
    o_g = o_sm + IDX_DIM + IDX_HEADS
    small_w = 2 * LANE
    h = _rmsnorm_bf16(x_all, norm_mix[0])
    z_a = _matmul(h, w[:, :o_bq].astype(bf16), name="proj_hgrn")
    z_bq = _matmul(h, w[:, o_bq:o_bk].astype(bf16), name="proj_q")
    z_kv = _matmul(h, w[:, o_bk:o_bqi].astype(bf16), name="proj_kv")
    z_qi = _matmul(h, w[:, o_bqi:o_sm].astype(bf16), name="proj_qidx")
    w_small = jnp.zeros((d, small_w), bf16).at[:, :o_g - o_sm].set(w[:, o_sm:o_g].astype(bf16))
    z_sm = _matmul(h, w_small, name="proj_small")
    z_g = _matmul(h, w[:, o_g:].astype(bf16), name="proj_gates")

    rb = min(HGRN_ROWS, seq)
    oa_p, s_p = _hgrn(z_a, lower, norm_hgrn_out[0], n_b, seq, rb, min(HGRN_CHUNK, rb), rb,
                      min(HGRN_PROMPT_HEADS, A_HEADS))
    sp = HGRN_SAMPLE_ROWS
    zs = z_a[np_rows:n_all].reshape(n_s, n_tok, 4 * kd)
    zs = jnp.concatenate([zs, jnp.zeros((n_s, sp - n_tok, 4 * kd), f32)], axis=1).reshape(n_s * sp, 4 * kd)
    oa_sp, s_s = _hgrn(zs, lower, norm_hgrn_out[0], n_s, sp, sp, sp, n_tok, min(HGRN_SAMPLE_HEADS, A_HEADS),
                       s0=state_hgrn[0])
    oa_s = oa_sp.reshape(n_s, sp, d)[:, :n_tok].reshape(ns_rows, d)

    ob_p = _dsa_prompt(z_qi, z_sm, z_bq, z_kv, n_b, seq)
    sm_s = z_sm[np_rows:n_all]
    w_s = sm_s[:, IDX_DIM:IDX_DIM + IDX_HEADS].reshape(n_s, n_tok, IDX_HEADS) * (IDX_HEADS ** -0.5 * IDX_DIM ** -0.5)
    eye = jnp.eye(8, n_tok, dtype=f32)
    wsel = (eye[None, :, :, None] * w_s[:, None, :, :]).reshape(n_s, 8, n_tok * IDX_HEADS)
    qi_s = z_qi[np_rows:n_all].reshape(n_s, n_tok * IDX_HEADS, IDX_DIM)
    q_s = z_bq[np_rows:n_all].reshape(n_s, n_tok * B_HEADS, dh)

    def pad_rows_to_lane(a):
        return jnp.concatenate([a, jnp.zeros((n_s, LANE - n_tok) + a.shape[2:], f32)], axis=1)

    def new_kv(a):
        return pad_rows_to_lane(a.reshape(n_s, n_tok, B_KV_HEADS, dh)).reshape(n_s, LANE * B_KV_HEADS, dh)

    kv_s = z_kv[np_rows:n_all]
    ob_s = _dsa_sample(qi_s, wsel, q_s, pad_rows_to_lane(sm_s[:, :IDX_DIM].reshape(n_s, n_tok, IDX_DIM)),
                       new_kv(kv_s[:, :kvd]), new_kv(kv_s[:, kvd:]),
                       cache_k, cache_v, cache_idx_k, page_table, n_tok)
    ob_s = ob_s.reshape(ns_rows, d)

    pad_rows = jnp.zeros((n_pad - n_all, d), f32)
    oa = jnp.concatenate([oa_p, oa_s, pad_rows], axis=0)
    ob = jnp.concatenate([ob_p, ob_s, pad_rows], axis=0)
    m = _merge(z_g, oa, ob)
    x1 = _matmul(m, w_out[0].astype(bf16), res=x_all, name="out_proj")

    y = _moe(x1, norm_ffn[0], router_group[0], router_group_bias[0], router_expert[0], router_expert_bias[0],
             expert_w_gate, expert_w_up, expert_w_down, norm_final)

    kp = z_kv[:np_rows]
    ks = z_kv[np_rows:n_all]
    return (y[:np_rows].reshape(n_b, seq, d),
            y[np_rows:n_all].reshape(n_s, n_tok, d),
            kp[:, :kvd].reshape(1, n_b, seq, B_KV_HEADS, dh),
            kp[:, kvd:].reshape(1, n_b, seq, B_KV_HEADS, dh),
            z_sm[:np_rows, :IDX_DIM].reshape(1, n_b, seq, IDX_DIM),
            s_p[None],
            ks[:, :kvd].reshape(1, n_s, n_tok, B_KV_HEADS, dh),
            ks[:, kvd:].reshape(1, n_s, n_tok, B_KV_HEADS, dh),
            z_sm[np_rows:n_all, :IDX_DIM].reshape(1, n_s, n_tok, IDX_DIM),
            s_s[None])
```

```python
import functools
import math

import jax
import jax.numpy as jnp
import numpy as np
from jax import lax
from jax.experimental import pallas as pl
from jax.experimental.pallas import tpu as pltpu

A_HEADS = 32
A_EXPAND = 128
B_HEADS = 32
B_KV_HEADS = 8
IDX_HEADS = 32
IDX_DIM = 128
TOPK_MAX = 256
N_GROUPS = 8
EXPERTS_PER_GROUP = 8
NORM_EPS = 1e-6

LANE = 128
ROW_BLOCK = 128
MOE_BLOCK = 128
MOE_ITEM_BLOCKS = 4
MOE_FF_TILE = 256
MOE_WEIGHT_SPLIT = 4
HGRN_CHUNK = 64
HGRN_ROWS = 512
HGRN_SAMPLE_ROWS = 16
HGRN_SAMPLE_HEADS = 8
HGRN_PROMPT_HEADS = 4
HGRN_SAFE_DECAY = 60.0
SAMPLE_PAGES_PER_STEP = 8
DMA_ISSUE_UNROLL = 8
COUNT_CHAINS = 4
NEG_BIG = -1e30
LOG2E = 1.4426950408889634
VMEM_LIMIT = 56 * 1024 * 1024

f32 = jnp.float32
bf16 = jnp.bfloat16
i32 = jnp.int32
HIGHEST = lax.Precision.HIGHEST
NT_DIMS = (((1,), (1,)), ((), ()))
TN_DIMS = (((0,), (0,)), ((), ()))


def _pick_tile(n, cap, align):
    t = (min(cap, n) // align) * align
    while t > align and n % t:
        t -= align
    assert t > 0 and n % t == 0, (n, cap, align)
    return t


def _params(sem, limit=VMEM_LIMIT):
    return pltpu.CompilerParams(dimension_semantics=sem, vmem_limit_bytes=limit)


def _sigmoid(x):
    return 1.0 / (1.0 + jnp.exp(-x))


def _log2(n):
    s = int(math.log2(n))
    assert 1 << s == n, n
    return s


def _rmsnorm_kernel(x_ref, g_ref, o_ref):
    x = x_ref[...]
    ms = jnp.mean(x * x, axis=-1, keepdims=True)
    o_ref[...] = (x * lax.rsqrt(ms + NORM_EPS) * g_ref[...]).astype(o_ref.dtype)


def _rmsnorm_bf16(x, g):
    n, d = x.shape
    return pl.pallas_call(
        _rmsnorm_kernel,
        grid=(n // ROW_BLOCK,),
        in_specs=[pl.BlockSpec((ROW_BLOCK, d), lambda i: (i, 0)),
                  pl.BlockSpec((1, d), lambda i: (0, 0))],
        out_specs=pl.BlockSpec((ROW_BLOCK, d), lambda i: (i, 0)),
        out_shape=jax.ShapeDtypeStruct((n, d), bf16),
        compiler_params=_params(("parallel",)),
        name="rmsnorm_bf16",
    )(x, g.reshape(1, d))


def _mm_kernel(a_ref, b_ref, o_ref):
    o_ref[...] = jnp.dot(a_ref[...], b_ref[...], preferred_element_type=f32)


def _mm_res_kernel(a_ref, b_ref, r_ref, o_ref):
    o_ref[...] = r_ref[...] + jnp.dot(a_ref[...], b_ref[...], preferred_element_type=f32)


def _matmul(a, b, res=None, name="matmul"):
    m, k = a.shape
    n = b.shape[1]
    tm = _pick_tile(m, 640, 128)
    tn = _pick_tile(n, 1024, 128)
    in_specs = [pl.BlockSpec((tm, k), lambda j, i: (i, 0)),
                pl.BlockSpec((k, tn), lambda j, i: (0, j))]
    args = [a, b]
    kern = _mm_kernel
    if res is not None:
        in_specs.append(pl.BlockSpec((tm, tn), lambda j, i: (i, j)))
        args.append(res)
        kern = _mm_res_kernel
    return pl.pallas_call(
        kern,
        grid=(n // tn, m // tm),
        in_specs=in_specs,
        out_specs=pl.BlockSpec((tm, tn), lambda j, i: (i, j)),
        out_shape=jax.ShapeDtypeStruct((m, n), f32),
        compiler_params=_params(("parallel", "parallel")),
        name=name,
    )(*args)


def _mm_heads_kernel(a_ref, b_ref, o_ref):
    acc = jnp.dot(a_ref[...], b_ref[...], preferred_element_type=f32)
    for r in range(o_ref.shape[0]):
        for hh in range(o_ref.shape[1]):
            o_ref[r, hh] = acc[r * ROW_BLOCK:(r + 1) * ROW_BLOCK, hh * LANE:(hh + 1) * LANE]


def _matmul_heads(a, b, name):
    m, k = a.shape
    n = b.shape[1]
    tm = _pick_tile(m, 640, ROW_BLOCK)
    tn = _pick_tile(n, 1024, LANE)
    return pl.pallas_call(
        _mm_heads_kernel,
        grid=(n // tn, m // tm),
        in_specs=[pl.BlockSpec((tm, k), lambda j, i: (i, 0)),
                  pl.BlockSpec((k, tn), lambda j, i: (0, j))],
        out_specs=pl.BlockSpec((tm // ROW_BLOCK, tn // LANE, ROW_BLOCK, LANE), lambda j, i: (i, j, 0, 0)),
        out_shape=jax.ShapeDtypeStruct((m // ROW_BLOCK, n // LANE, ROW_BLOCK, LANE), f32),
        compiler_params=_params(("parallel", "parallel")),
        name=name,
    )(a, b)


def _hgrn_static(c):
    levels = _log2(c)
    t = np.arange(c)
    cmat = np.zeros((2 * c, c), np.float32)
    cmat[:c] = t[None, :] <= t[:, None]
    cmat[c:] = 1.0
    lmat = np.zeros((levels * c, c), np.float32)
    lv = np.full((c, c), -1, np.int32)
    for l in range(levels):
        blk = c >> l
        half = blk // 2
        mid = (t // blk) * blk + half
        lmat[l * c:(l + 1) * c] = t[None, :] <= (mid - 1)[:, None]
        same = (t[:, None] // blk) == (t[None, :] // blk)
        up = ((t // half) % 2 == 1)[:, None]
        lo = ((t // half) % 2 == 0)[None, :]
        lv[same & up & lo] = l
    lv[t, t] = levels
    return cmat, lmat, lv, levels


def _split3(x):
    hi = x.astype(bf16)
    r = x - hi.astype(f32)
    mid = r.astype(bf16)
    lo = (r - mid.astype(f32)).astype(bf16)
    return jnp.concatenate([hi, mid, lo], axis=1)


def _sum3(y):
    w = y.shape[1] // 3
    return y[:, :w] + y[:, w:2 * w] + y[:, 2 * w:]


def _hgrn_kernel(*refs, rows, chunk, levels, n_valid, heads, has_init):
    if has_init:
        (zq_ref, zf_ref, zi_ref, zg_ref, lb_ref, gain_ref, cmat_ref, lmat_ref, lv_ref, s0_ref,
         oa_ref, sfin_ref, st_ref, a_ref, q_sc, k_sc, b_sc, p_sc) = refs
    else:
        (zq_ref, zf_ref, zi_ref, zg_ref, lb_ref, gain_ref, cmat_ref, lmat_ref, lv_ref,
         oa_ref, sfin_ref, st_ref, a_ref, q_sc, k_sc, b_sc, p_sc) = refs
        s0_ref = None
    tb = pl.program_id(2)
    c = chunk
    nc = rows // c
    dk = A_EXPAND

    @pl.when(tb == 0)
    def _():
        for hh in range(heads):
            if has_init:
                st_ref[hh] = s0_ref[0, hh].T
            else:
                st_ref[hh] = jnp.zeros(st_ref.shape[1:], f32)

    lv = lv_ref[...]
    cmat = cmat_ref[...]
    row_c = lax.broadcasted_iota(i32, (c, dk), 0)

    pre = []
    unsafe = None
    for hh in range(heads):
        ls = slice(hh * dk, (hh + 1) * dk)
        zq = zq_ref[:, ls]
        lb = lb_ref[:, ls]
        f = lb + (1.0 - lb) * _sigmoid(zf_ref[:, ls])
        logf = jnp.log(f)
        k = 1.0 - f
        q = zq * _sigmoid(zq)
        if n_valid < rows:
            live = lax.broadcasted_iota(i32, (rows, dk), 0) < n_valid
            logf = jnp.where(live, logf, 0.0)
            k = jnp.where(live, k, 0.0)
        parts = _split3(logf)
        bs, bls = [], []
        for ci in range(nc):
            y = _sum3(jnp.dot(cmat, parts[ci * c:(ci + 1) * c], preferred_element_type=f32))
            bs.append(y[:c])
            bls.append(y[c:])
        b = jnp.concatenate(bs, axis=0) if nc > 1 else bs[0]
        bl = jnp.concatenate(bls, axis=0) if nc > 1 else bls[0]
        head_unsafe = jnp.min(bl) < -HGRN_SAFE_DECAY
        unsafe = head_unsafe if unsafe is None else jnp.logical_or(unsafe, head_unsafe)
        pre.append(dict(q=q, k=k, b=b, bl=bl, parts=parts, qh_b=(q * jnp.exp(b)).astype(bf16),
                        kd=(k * jnp.exp(bl - b)).astype(bf16), v_b=zi_ref[:, ls].astype(bf16)))

    @pl.when(jnp.logical_not(unsafe))
    def _():
        for hh, hv in enumerate(pre):
            kt = (hv["k"] * jnp.exp(-hv["b"])).astype(bf16)
            for ci in range(nc):
                rs = slice(ci * c, (ci + 1) * c)
                al = lax.dot_general(hv["qh_b"][rs], kt[rs], NT_DIMS, preferred_element_type=f32)
                a_ref[hh * nc + ci] = jnp.where(lv >= 0, al, 0.0)

    @pl.when(unsafe)
    def _():
        for hh, hv in enumerate(pre):
            hr = slice(hh * rows, (hh + 1) * rows)
            q_sc[hr, :] = hv["q"]
            k_sc[hr, :] = hv["k"]
            b_sc[hr, :] = hv["b"]
            p_sc[hr, :] = hv["parts"]

        def safe_chunk(ci, carry):
            r0 = pl.multiple_of(ci * c, c)
            q_c = q_sc[pl.ds(r0, c), :]
            k_c = k_sc[pl.ds(r0, c), :]
            b_c = b_sc[pl.ds(r0, c), :]
            refs_b = _sum3(jnp.dot(lmat_ref[...], p_sc[pl.ds(r0, c), :], preferred_element_type=f32))
            a = jnp.where(lv == levels, jnp.sum(q_c * k_c, axis=-1, keepdims=True), 0.0)
            for l in range(levels):
                e = jnp.exp(-jnp.abs(b_c - refs_b[l * c:(l + 1) * c]))
                upper = ((row_c >> (levels - l - 1)) & 1) == 1
                qt = jnp.where(upper, q_c * e, 0.0).astype(bf16)
                kt = jnp.where(upper, 0.0, k_c * e).astype(bf16)
                al = lax.dot_general(qt, kt, NT_DIMS, preferred_element_type=f32)
                a = jnp.where(lv == l, al, a)
            a_ref[ci] = a
            return carry

        lax.fori_loop(0, heads * nc, safe_chunk, 0)

    for hh, hv in enumerate(pre):
        ls = slice(hh * dk, (hh + 1) * dk)
        gain = gain_ref[:, ls]
        zg = zg_ref[:, ls]
        st = st_ref[hh]
        for ci in range(nc):
            rs = slice(ci * c, (ci + 1) * c)
            o = (jnp.dot(a_ref[hh * nc + ci].astype(bf16), hv["v_b"][rs], preferred_element_type=f32)
                 + lax.dot_general(hv["qh_b"][rs], st.astype(bf16), NT_DIMS, preferred_element_type=f32))
            ms = jnp.mean(o * o, axis=-1, keepdims=True)
            zg_c = zg[rs]
            oa_ref[rs, ls] = o * lax.rsqrt(ms + NORM_EPS) * gain * (zg_c * _sigmoid(zg_c))
            st = (st * jnp.exp(hv["bl"][ci * c:ci * c + 1])
                  + lax.dot_general(hv["v_b"][rs], hv["kd"][rs], TN_DIMS, preferred_element_type=f32))
        st_ref[hh] = st

    @pl.when(tb == pl.num_programs(2) - 1)
    def _():
        for hh in range(heads):
            sfin_ref[0, hh] = st_ref[hh].T


def _hgrn(z_a, lb, gain, n_seq, seq_rows, row_block, chunk, n_valid, heads, s0=None):
    kd = A_HEADS * A_EXPAND
    dv = LANE
    assert z_a.shape[1] == 2 * kd + 2 * A_HEADS * dv and A_EXPAND == LANE and A_HEADS % heads == 0
    cmat, lmat, lv, levels = _hgrn_static(chunk)
    nrb = seq_rows // row_block
    hb = A_HEADS // heads
    wd = heads * LANE

    def zspec(seg):
        return pl.BlockSpec((row_block, wd), lambda b, h, t, seg=seg: (b * nrb + t, seg * hb + h))

    const = lambda b, h, t: (0, 0)
    in_specs = [zspec(0), zspec(1), zspec(2), zspec(3),
                pl.BlockSpec((1, wd), lambda b, h, t: (0, h)),
                pl.BlockSpec((1, wd), lambda b, h, t: (0, h)),
                pl.BlockSpec(cmat.shape, const), pl.BlockSpec(lmat.shape, const), pl.BlockSpec(lv.shape, const)]
    args = [z_a, z_a, z_a, z_a, lb.reshape(1, kd), gain.reshape(1, -1),
            jnp.asarray(cmat, bf16), jnp.asarray(lmat, bf16), jnp.asarray(lv)]
    if s0 is not None:
        in_specs.append(pl.BlockSpec((1, heads, A_EXPAND, dv), lambda b, h, t: (b, h, 0, 0)))
        args.append(s0)
    kern = functools.partial(_hgrn_kernel, rows=row_block, chunk=chunk, levels=levels, n_valid=n_valid,
                             heads=heads, has_init=s0 is not None)
    return pl.pallas_call(
        kern,
        grid=(n_seq, hb, nrb),
        in_specs=in_specs,
        out_specs=[pl.BlockSpec((row_block, wd), lambda b, h, t: (b * nrb + t, h)),
                   pl.BlockSpec((1, heads, A_EXPAND, dv), lambda b, h, t: (b, h, 0, 0))],
        out_shape=[jax.ShapeDtypeStruct((n_seq * seq_rows, A_HEADS * dv), f32),
                   jax.ShapeDtypeStruct((n_seq, A_HEADS, A_EXPAND, dv), f32)],
        scratch_shapes=[pltpu.VMEM((heads, dv, A_EXPAND), f32),
                        pltpu.VMEM((heads * row_block // chunk, chunk, chunk), f32),
                        pltpu.VMEM((heads * row_block, LANE), f32), pltpu.VMEM((heads * row_block, LANE), f32),
                        pltpu.VMEM((heads * row_block, LANE), f32),
                        pltpu.VMEM((heads * row_block, 3 * LANE), bf16)],
        compiler_params=_params(("parallel", "parallel", "arbitrary")),
        name="hgrn2_prompt" if s0 is None else "hgrn2_sample",
    )(*args)


def _sort_key(x):
    b = pltpu.bitcast(x + 0.0, i32)
    return b ^ ((b >> 31) & 0x7FFFFFFF)


def _kth_largest_key(key_ref, k, axis):
    shape = (1, key_ref.shape[1]) if axis == 0 else (key_ref.shape[0], 1)
    int_min = jnp.int32(-2 ** 31)

    n = key_ref.shape[axis]
    parts = COUNT_CHAINS if n % (COUNT_CHAINS * LANE) == 0 else 1
    seg = n // parts

    def step(i, lo):
        cand = lo ^ lax.shift_left(jnp.int32(1), 31 - i)
        cnt = None
        for pi in range(parts):
            sl = slice(pi * seg, (pi + 1) * seg)
            keys = key_ref[sl, :] if axis == 0 else key_ref[:, sl]
            c = jnp.sum((keys >= cand).astype(i32), axis=axis, keepdims=True)
            cnt = c if cnt is None else cnt + c
        return jnp.where(cnt >= k, cand, lo)

    return lax.fori_loop(0, 32, step, jnp.full(shape, int_min, i32))


def _dsa_prompt_kernel(qi_ref, smq_ref, smk_ref, q_ref, k_ref, v_ref, o_ref, mask_ref, key_ref, qs_ref,
                       *, kext, topk, qblk, group, qb0):
    qb = qb0 + pl.program_id(1)
    g = pl.program_id(2)
    n_tiles = kext // qblk

    def causal(r0):
        kpos = r0 + lax.broadcasted_iota(i32, (qblk, qblk), 0)
        qpos = qb * qblk + lax.broadcasted_iota(i32, (qblk, qblk), 1)
        return kpos <= qpos

    @pl.when(g == 0)
    def _():
        qs_ref[...] = qi_ref[0].reshape(IDX_HEADS * qblk, IDX_DIM).astype(bf16)
        w_t = smq_ref[...].T * (IDX_HEADS ** -0.5 * IDX_DIM ** -0.5)

        def score_tile(ti, carry):
            r0 = pl.multiple_of(ti * qblk, qblk)
            kid = smk_ref[pl.ds(r0, qblk), 0:IDX_DIM].astype(bf16)
            s_all = lax.dot_general(kid, qs_ref[...], NT_DIMS, preferred_element_type=f32)
            acc = jnp.zeros((qblk, qblk), f32)
            for h in range(IDX_HEADS):
                acc = acc + (jnp.maximum(s_all[:, h * qblk:(h + 1) * qblk], 0.0)
                             * w_t[IDX_DIM + h:IDX_DIM + h + 1, :])
            key_ref[pl.ds(r0, qblk), :] = _sort_key(jnp.where(causal(r0), acc, -jnp.inf))
            return carry

        lax.fori_loop(0, n_tiles, score_tile, 0)
        thr = _kth_largest_key(key_ref, topk, axis=0)
        keys = key_ref[...]
        kpos = lax.broadcasted_iota(i32, (kext, qblk), 0)
        qpos = qb * qblk + lax.broadcasted_iota(i32, (kext, qblk), 1)
        mask_ref[...] = jnp.where(jnp.logical_and(kpos <= qpos, keys >= thr), 0.0, -jnp.inf)
        need = topk - jnp.sum((keys > thr).astype(i32), axis=0, keepdims=True)
        n_tie = jnp.sum((keys == thr).astype(i32), axis=0, keepdims=True)

        @pl.when(jnp.max(n_tie - need) > 0)
        def _():
            strict = (lax.broadcasted_iota(i32, (qblk, qblk), 0) > lax.broadcasted_iota(i32, (qblk, qblk), 1))
            strict = jnp.where(strict, 1.0, 0.0).astype(bf16)
            need_f = need.astype(f32)

            def tile(ti, seen):
                r0 = pl.multiple_of(ti * qblk, qblk)
                kt = key_ref[pl.ds(r0, qblk), :]
                tie = kt == thr
                tie_f = jnp.where(tie, 1.0, 0.0)
                before = seen + jnp.dot(strict, tie_f.astype(bf16), preferred_element_type=f32)
                keep = jnp.logical_or(kt > thr, jnp.logical_and(tie, before < need_f))
                keep = jnp.logical_and(keep, causal(r0))
                mask_ref[pl.ds(r0, qblk), :] = jnp.where(keep, 0.0, -jnp.inf)
                return seen + jnp.sum(tie_f, axis=0, keepdims=True)

            lax.fori_loop(0, n_tiles, tile, jnp.zeros((1, qblk), f32))

    kk = k_ref[...].astype(bf16)
    vt = v_ref[...].T.astype(bf16)
    dh = kk.shape[1]
    for hq in range(group):
        qh = (q_ref[:, hq * dh:(hq + 1) * dh] * (dh ** -0.5 * LOG2E)).astype(bf16)
        s = lax.dot_general(kk, qh, NT_DIMS, preferred_element_type=f32) + mask_ref[...]
        m = jnp.max(s, axis=0, keepdims=True)
        p = jnp.exp2(s - m)
        den = jnp.sum(p, axis=0, keepdims=True)
        o_t = jnp.dot(vt, p.astype(bf16), preferred_element_type=f32) / den
        o_ref[:, hq * dh:(hq + 1) * dh] = o_t.T


def _dsa_prompt(bqi, small, bq, bkv, n_batch, seq):
    dh = bq.shape[1] // B_HEADS
    group = B_HEADS // B_KV_HEADS
    qblk = LANE
    nqb = seq // qblk
    topk = min(TOPK_MAX, seq // 4)
    parts = []
    qb0 = 0
    kext = 2 * qblk
    while qb0 < nqb:
        kext = min(kext, seq)
        assert seq % kext == 0
        count = kext // qblk - qb0
        nkb = seq // kext
        kern = functools.partial(_dsa_prompt_kernel, kext=kext, topk=topk, qblk=qblk, group=group, qb0=qb0)
        part = pl.pallas_call(
            kern,
            grid=(n_batch, count, B_KV_HEADS),
            in_specs=[pl.BlockSpec((1, IDX_HEADS, qblk, IDX_DIM), lambda b, j, g, q0=qb0: (b * nqb + q0 + j, 0, 0, 0)),
                      pl.BlockSpec((qblk, small.shape[1]), lambda b, j, g, q0=qb0: (b * nqb + q0 + j, 0)),
                      pl.BlockSpec((kext, small.shape[1]), lambda b, j, g, nk=nkb: (b * nk, 0)),
                      pl.BlockSpec((qblk, group * dh), lambda b, j, g, q0=qb0: (b * nqb + q0 + j, g)),
                      pl.BlockSpec((kext, dh), lambda b, j, g, nk=nkb: (b * nk, g)),
                      pl.BlockSpec((kext, dh), lambda b, j, g, nk=nkb: (b * nk, B_KV_HEADS + g))],
            out_specs=pl.BlockSpec((qblk, group * dh), lambda b, j, g, cnt=count: (b * cnt + j, g)),
            out_shape=jax.ShapeDtypeStruct((n_batch * count * qblk, B_HEADS * dh), f32),
            scratch_shapes=[pltpu.VMEM((kext, qblk), f32), pltpu.VMEM((kext, qblk), i32),
                            pltpu.VMEM((IDX_HEADS * qblk, IDX_DIM), bf16)],
            compiler_params=_params(("parallel", "parallel", "arbitrary")),
            name=f"dsa_prompt_k{kext}",
        )(bqi, small, small, bq, bkv, bkv)
        parts.append(part.reshape(n_batch, count * qblk, B_HEADS * dh))
        qb0 += count
        kext *= 2
    return jnp.concatenate(parts, axis=1).reshape(n_batch * seq, B_HEADS * dh)


def _dsa_sample_select_kernel(pt_ref, qi_ref, wsel_ref, knew_ref, cik_hbm, mask_ref,
                              slab_ref, sc_ref, key_ref, sel_ref, sem, *, n_pages, page, topk, n_tok, kvh):
    b = pl.program_id(0)
    n_seq = pl.num_programs(0)
    past = n_pages * page

    def page_copy(seq, slot, j):
        return pltpu.make_async_copy(cik_hbm.at[0, pt_ref[seq, j]],
                                     slab_ref.at[slot, pl.ds(pl.multiple_of(j * page, page), page), :],
                                     sem.at[slot])

    def start(seq, slot):
        lax.fori_loop(0, n_pages, lambda j, c: (page_copy(seq, slot, j).start(), c)[1], 0)

    def wait(seq, slot):
        lax.fori_loop(0, n_pages, lambda j, c: (page_copy(seq, slot, j).wait(), c)[1], 0)

    @pl.when(b == 0)
    def _():
        start(0, 0)

    slot = b % 2

    @pl.when(b + 1 < n_seq)
    def _():
        start(b + 1, 1 - slot)

    wait(b, slot)
    qi = qi_ref[0].astype(bf16)
    wsel = wsel_ref[0]

    def tile_scores(keys):
        s = lax.dot_general(qi, keys.astype(bf16), NT_DIMS, preferred_element_type=f32)
        return jnp.dot(wsel, jnp.maximum(s, 0.0), precision=HIGHEST, preferred_element_type=f32)

    sc_ref[:, 0:past] = tile_scores(slab_ref[slot])
    new = tile_scores(knew_ref[0])
    r = lax.broadcasted_iota(i32, new.shape, 0)
    cidx = lax.broadcasted_iota(i32, new.shape, 1)
    sc_ref[:, past:past + LANE] = jnp.where(jnp.logical_and(cidx <= r, cidx < n_tok), new, -jnp.inf)
    sc = sc_ref[...]
    key_ref[...] = _sort_key(sc)
    thr = _kth_largest_key(key_ref, topk, axis=1)
    keys = key_ref[...]
    sel = jnp.logical_and(keys >= thr, sc > -jnp.inf)
    sel_ref[...] = jnp.where(sel, 1.0, 0.0)
    need = topk - jnp.sum((keys > thr).astype(i32), axis=1, keepdims=True)
    n_tie = jnp.sum((keys == thr).astype(i32), axis=1, keepdims=True)

    @pl.when(jnp.max(n_tie - need) > 0)
    def _():
        strict = (lax.broadcasted_iota(i32, (LANE, LANE), 0) < lax.broadcasted_iota(i32, (LANE, LANE), 1))
        strict = jnp.where(strict, 1.0, 0.0).astype(bf16)
        need_f = need.astype(f32)

        def tile(ti, seen):
            c1 = pl.multiple_of(ti * LANE, LANE)
            kt = key_ref[:, pl.ds(c1, LANE)]
            tie = kt == thr
            tie_f = jnp.where(tie, 1.0, 0.0)
            before = seen + jnp.dot(tie_f.astype(bf16), strict, preferred_element_type=f32)
            keep = jnp.logical_or(kt > thr, jnp.logical_and(tie, before < need_f))
            keep = jnp.logical_and(keep, sc_ref[:, pl.ds(c1, LANE)] > -jnp.inf)
            sel_ref[:, pl.ds(c1, LANE)] = jnp.where(keep, 1.0, 0.0)
            return seen + jnp.sum(tie_f, axis=1, keepdims=True)

        lax.fori_loop(0, sc_ref.shape[1] // LANE, tile, jnp.zeros((8, 1), f32))

    lane = lax.broadcasted_iota(i32, (8, LANE), 1)

    def expand(ti, carry):
        c1 = pl.multiple_of(ti * LANE, LANE)
        hit = sel_ref[:, pl.ds(c1, LANE)]
        for j in range(kvh):
            src = j * (LANE // kvh) + (lane >> _log2(kvh))
            piece = jnp.take_along_axis(hit, src, axis=1)
            c2 = pl.multiple_of(ti * (LANE * kvh) + j * LANE, LANE)
            mask_ref[0, :, pl.ds(c2, LANE)] = jnp.where(piece > 0.5, 0.0, NEG_BIG)
        return carry

    lax.fori_loop(0, sc_ref.shape[1] // LANE, expand, 0)


def _dsa_sample_attn_kernel(pt_ref, q_ref, mask_ref, masknew_ref, knew_ref, vnew_ref, gm_ref, ck_hbm, cv_hbm,
                            o_ref, kslab, vslab, m_ref, l_ref, acc_ref, sem,
                            *, n_chunks, ch, n_tok, kvh, group, dh, page):
    b = pl.program_id(0)
    c = pl.program_id(1)
    t = b * n_chunks + c
    total = pl.num_programs(0) * n_chunks
    scale = dh ** -0.5

    def copies(seq, chunk, slot, j):
        pid = pt_ref[seq, chunk * ch + j]
        return (pltpu.make_async_copy(ck_hbm.at[0, pid], kslab.at[slot, j], sem.at[slot, 0]),
                pltpu.make_async_copy(cv_hbm.at[0, pid], vslab.at[slot, j], sem.at[slot, 1]))

    def start(seq, chunk, slot):
        for j in range(ch):
            ck, cv = copies(seq, chunk, slot, j)
            ck.start()
            cv.start()

    def wait(seq, chunk, slot):
        for j in range(ch):
            ck, cv = copies(seq, chunk, slot, j)
            ck.wait()
            cv.wait()

    @pl.when(t == 0)
    def _():
        start(0, 0, 0)

    slot = t % 2

    @pl.when(t + 1 < total)
    def _():
        nxt = t + 1
        start(nxt // n_chunks, nxt % n_chunks, 1 - slot)

    wait(b, c, slot)

    @pl.when(c == 0)
    def _():
        m_ref[...] = jnp.full_like(m_ref, NEG_BIG)
        l_ref[...] = jnp.zeros_like(l_ref)
        acc_ref[...] = jnp.zeros_like(acc_ref)

    qb = (q_ref[0] * scale).astype(bf16)
    heads = kvh * group

    def attend(keys, vals, mask8):
        n_cols = keys.shape[0]
        tok_mask = jnp.concatenate([jnp.broadcast_to(mask8[tk:tk + 1], (heads, n_cols))
                                    for tk in range(n_tok)], axis=0)
        madd = tok_mask + gm_ref[:, 0:n_cols]
        s = lax.dot_general(qb, keys, NT_DIMS, preferred_element_type=f32) + madd
        m_old = m_ref[...]
        m_new = jnp.maximum(m_old, jnp.max(s, axis=-1, keepdims=True))
        pr = jnp.where(madd == 0.0, jnp.exp(s - m_new), 0.0)
        alpha = jnp.exp(m_old - m_new)
        l_ref[...] = alpha * l_ref[...] + jnp.sum(pr, axis=-1, keepdims=True)
        acc_ref[...] = alpha * acc_ref[...] + jnp.dot(pr.astype(bf16), vals, preferred_element_type=f32)
        m_ref[...] = m_new

    attend(kslab[slot].reshape(ch * page * kvh, dh).astype(bf16),
           vslab[slot].reshape(ch * page * kvh, dh).astype(bf16), mask_ref[0])

    @pl.when(c == n_chunks - 1)
    def _():
        attend(knew_ref[0].astype(bf16), vnew_ref[0].astype(bf16), masknew_ref[0])
        o_ref[0] = acc_ref[...] / l_ref[...]


def _dsa_sample(qi, wsel, q, knew_idx, k_new, v_new, cache_k, cache_v, cache_idx_k, page_table, n_tok):
    n_seq, n_pages = page_table.shape
    page, kvh, dh = cache_k.shape[2], cache_k.shape[3], cache_k.shape[4]
    assert page == LANE and LANE % kvh == 0
    rows_i = qi.shape[1]
    rows = q.shape[1]
    heads = rows // n_tok
    group = heads // kvh
    past = n_pages * page
    kp = past + LANE
    topk = min(TOPK_MAX, (past + n_tok) // 4)
    ch = math.gcd(SAMPLE_PAGES_PER_STEP, n_pages)
    n_chunks = n_pages // ch
    cw = ch * page * kvh
    own = (np.arange(rows)[:, None] % heads) // group == (np.arange(cw)[None, :] % kvh)
    gm = np.where(own, 0.0, NEG_BIG).astype(np.float32)

    mask = pl.pallas_call(
        functools.partial(_dsa_sample_select_kernel, n_pages=n_pages, page=page, topk=topk, n_tok=n_tok,
                          kvh=kvh),
        grid_spec=pltpu.PrefetchScalarGridSpec(
            num_scalar_prefetch=1,
            grid=(n_seq,),
            in_specs=[pl.BlockSpec((1, rows_i, IDX_DIM), lambda b, pt: (b, 0, 0)),
                      pl.BlockSpec((1, 8, rows_i), lambda b, pt: (b, 0, 0)),
                      pl.BlockSpec((1, LANE, IDX_DIM), lambda b, pt: (b, 0, 0)),
                      pl.BlockSpec(memory_space=pl.ANY)],
            out_specs=pl.BlockSpec((1, 8, kp * kvh), lambda b, pt: (b, 0, 0)),
            scratch_shapes=[pltpu.VMEM((2, past, IDX_DIM), f32), pltpu.VMEM((8, kp), f32),
                            pltpu.VMEM((8, kp), i32), pltpu.VMEM((8, kp), f32), pltpu.SemaphoreType.DMA((2,))]),
        out_shape=jax.ShapeDtypeStruct((n_seq, 8, kp * kvh), f32),
        compiler_params=_params(("arbitrary",)),
        name="dsa_sample_select",
    )(page_table, qi, wsel, knew_idx, cache_idx_k)

    return pl.pallas_call(
        functools.partial(_dsa_sample_attn_kernel, n_chunks=n_chunks, ch=ch, n_tok=n_tok, kvh=kvh, group=group,
                          dh=dh, page=page),
        grid_spec=pltpu.PrefetchScalarGridSpec(
            num_scalar_prefetch=1,
            grid=(n_seq, n_chunks),
            in_specs=[pl.BlockSpec((1, rows, dh), lambda b, c, pt: (b, 0, 0)),
                      pl.BlockSpec((1, 8, cw), lambda b, c, pt: (b, 0, c)),
                      pl.BlockSpec((1, 8, LANE * kvh), lambda b, c, pt: (b, 0, past // LANE)),
                      pl.BlockSpec((1, LANE * kvh, dh), lambda b, c, pt: (b, 0, 0)),
                      pl.BlockSpec((1, LANE * kvh, dh), lambda b, c, pt: (b, 0, 0)),
                      pl.BlockSpec(gm.shape, lambda b, c, pt: (0, 0)),
                      pl.BlockSpec(memory_space=pl.ANY),
                      pl.BlockSpec(memory_space=pl.ANY)],
            out_specs=pl.BlockSpec((1, rows, dh), lambda b, c, pt: (b, 0, 0)),
            scratch_shapes=[pltpu.VMEM((2, ch, page, kvh, dh), f32), pltpu.VMEM((2, ch, page, kvh, dh), f32),
                            pltpu.VMEM((rows, 1), f32), pltpu.VMEM((rows, 1), f32),
                            pltpu.VMEM((rows, dh), f32), pltpu.SemaphoreType.DMA((2, 2))]),
        out_shape=jax.ShapeDtypeStruct((n_seq, rows, dh), f32),
        compiler_params=_params(("arbitrary", "arbitrary")),
        name="dsa_sample_attn",
    )(page_table, q, mask, mask, k_new, v_new, jnp.asarray(gm), cache_k, cache_v)


def _merge_kernel(ga_ref, gb_ref, oa_ref, ob_ref, o_ref):
    o_ref[...] = (_sigmoid(ga_ref[...]) * oa_ref[...] + _sigmoid(gb_ref[...]) * ob_ref[...]).astype(o_ref.dtype)


def _merge(gates, oa, ob):
    n, d = oa.shape
    tn = _pick_tile(d, 1024, 128)
    nj = d // tn
    return pl.pallas_call(
        _merge_kernel,
        grid=(n // ROW_BLOCK, nj),
        in_specs=[pl.BlockSpec((ROW_BLOCK, tn), lambda i, j: (i, j)),
                  pl.BlockSpec((ROW_BLOCK, tn), lambda i, j: (i, nj + j)),
                  pl.BlockSpec((ROW_BLOCK, tn), lambda i, j: (i, j)),
                  pl.BlockSpec((ROW_BLOCK, tn), lambda i, j: (i, j))],
        out_specs=pl.BlockSpec((ROW_BLOCK, tn), lambda i, j: (i, j)),
        out_shape=jax.ShapeDtypeStruct((n, d), bf16),
        compiler_params=_params(("parallel", "parallel")),
        name="merge_branches",
    )(gates, gates, oa, ob)


def _router_kernel(x_ref, g_ref, w_ref, b_ref, route_ref, cnt_ref, carry_ref):
    i = pl.program_id(0)

    @pl.when(i == 0)
    def _():
        carry_ref[...] = jnp.zeros_like(carry_ref)

    x = x_ref[...]
    ms = jnp.mean(x * x, axis=-1, keepdims=True)
    h = x * lax.rsqrt(ms + NORM_EPS) * g_ref[...]
    logits = jnp.dot(h, w_ref[...], precision=HIGHEST, preferred_element_type=f32) + b_ref[...]
    lane = lax.broadcasted_iota(i32, logits.shape, 1)
    big = jnp.int32(1 << 20)
    ng, epg = N_GROUPS, EXPERTS_PER_GROUP

    def first_max(vals):
        mx = jnp.max(vals, axis=-1, keepdims=True)
        idx = jnp.min(jnp.where(vals == mx, lane, big), axis=-1, keepdims=True)
        return mx, idx

    lg = jnp.where(lane < ng, logits, -jnp.inf)
    mg, g_sel = first_max(lg)
    p_g = 1.0 / jnp.sum(jnp.exp(lg - mg), axis=-1, keepdims=True)
    lo = ng + g_sel * epg
    le = jnp.where(jnp.logical_and(lane >= lo, lane < lo + epg), logits, -jnp.inf)
    m1, i1 = first_max(le)
    se = jnp.sum(jnp.exp(le - m1), axis=-1, keepdims=True)
    m2, i2 = first_max(jnp.where(lane == i1, -jnp.inf, le))
    p1 = 1.0 / se
    p2 = jnp.exp(m2 - m1) / se
    w1 = p_g * p1 / (p1 + p2)
    w2 = p_g * p2 / (p1 + p2)

    hot = jnp.logical_or(lane == i1, lane == i2)
    hot_b = jnp.where(hot, 1.0, 0.0).astype(bf16)
    n = x.shape[0]
    strict = (lax.broadcasted_iota(i32, (n, n), 0) > lax.broadcasted_iota(i32, (n, n), 1))
    before = jnp.dot(jnp.where(strict, 1.0, 0.0).astype(bf16), hot_b, preferred_element_type=f32)
    before = before + carry_ref[...]
    r1 = jnp.sum(jnp.where(lane == i1, before, 0.0), axis=-1, keepdims=True)
    r2 = jnp.sum(jnp.where(lane == i2, before, 0.0), axis=-1, keepdims=True)
    carry_ref[...] = carry_ref[...] + jnp.sum(jnp.where(hot, 1.0, 0.0), axis=0, keepdims=True)

    out = jnp.where(lane == 0, (i1 - ng).astype(f32), 0.0)
    out = jnp.where(lane == 1, (i2 - ng).astype(f32), out)
    out = jnp.where(lane == 2, w1, out)
    out = jnp.where(lane == 3, w2, out)
    out = jnp.where(lane == 4, r1, out)
    out = jnp.where(lane == 5, r2, out)
    route_ref[...] = out
    cnt_ref[...] = carry_ref[...]


def _router(x1, g, w_r, b_r):
    n, d = x1.shape
    return pl.pallas_call(
        _router_kernel,
        grid=(n // ROW_BLOCK,),
        in_specs=[pl.BlockSpec((ROW_BLOCK, d), lambda i: (i, 0)),
                  pl.BlockSpec((1, d), lambda i: (0, 0)),
                  pl.BlockSpec((d, LANE), lambda i: (0, 0)),
                  pl.BlockSpec((1, LANE), lambda i: (0, 0))],
        out_specs=[pl.BlockSpec((ROW_BLOCK, LANE), lambda i: (i, 0)),
                   pl.BlockSpec((1, LANE), lambda i: (0, 0))],
        out_shape=[jax.ShapeDtypeStruct((n, LANE), f32), jax.ShapeDtypeStruct((1, LANE), f32)],
        scratch_shapes=[pltpu.VMEM((1, LANE), f32)],
        compiler_params=_params(("arbitrary",)),
        name="moe_router",
    )(x1, g.reshape(1, d), w_r, b_r)


def _row_copy(src_hbm, dst_vmem, sem, src_row, dst_row):
    return pltpu.make_async_copy(src_hbm.at[pl.ds(src_row, 1), :], dst_vmem.at[pl.ds(dst_row, 1), :], sem)


def _dispatch_kernel(tok_ref, x_hbm, g_ref, o_ref, buf_ref, sem):
    i = pl.program_id(0)
    n = pl.num_programs(0)

    def for_real_rows(blk, fn):
        def body(r8, c):
            for u in range(DMA_ISSUE_UNROLL):
                r = r8 * DMA_ISSUE_UNROLL + u
                tok = tok_ref[blk * MOE_BLOCK + r]

                @pl.when(tok >= 0)
                def _(tok=tok, r=r, u=u):
                    fn(tok, r, u)
            return c
        lax.fori_loop(0, MOE_BLOCK // DMA_ISSUE_UNROLL, body, 0)

    def start(blk, slot):
        for_real_rows(blk, lambda tok, r, u: _row_copy(x_hbm, buf_ref.at[slot], sem.at[slot], tok, r).start(
            priority=u % 2))

    def wait(blk, slot):
        for_real_rows(blk, lambda tok, r, u: _row_copy(x_hbm, buf_ref.at[slot], sem.at[slot], tok, r).wait())

    @pl.when(i == 0)
    def _():
        buf_ref[...] = jnp.zeros_like(buf_ref)
        start(0, 0)

    slot = i % 2

    @pl.when(i + 1 < n)
    def _():
        start(i + 1, 1 - slot)

    wait(i, slot)
    x = buf_ref[slot]
    ms = jnp.mean(x * x, axis=-1, keepdims=True)
    o_ref[...] = (x * lax.rsqrt(ms + NORM_EPS) * g_ref[...]).astype(o_ref.dtype)


def _dispatch(row_tok, x1, g):
    n, d = x1.shape
    rows = row_tok.shape[0]
    return pl.pallas_call(
        _dispatch_kernel,
        grid_spec=pltpu.PrefetchScalarGridSpec(
            num_scalar_prefetch=1,
            grid=(rows // MOE_BLOCK,),
            in_specs=[pl.BlockSpec(memory_space=pl.ANY),
                      pl.BlockSpec((1, d), lambda i, tok: (0, 0))],
            out_specs=pl.BlockSpec((MOE_BLOCK, d), lambda i, tok: (i, 0)),
            scratch_shapes=[pltpu.VMEM((2, MOE_BLOCK, d), f32), pltpu.SemaphoreType.DMA((2,))]),
        out_shape=jax.ShapeDtypeStruct((rows, d), bf16),
        compiler_params=_params(("arbitrary",)),
        name="moe_dispatch",
    )(row_tok, x1, g.reshape(1, d))


def _expert_ffn_kernel(ie_ref, ir_ref, in_ref, nit_ref, xs_hbm, *refs, n_ff, split):
    wg_refs, wu_refs, wd_refs = refs[:split], refs[split:2 * split], refs[2 * split:3 * split]
    y_hbm, x_ref, acc_ref, sem_in, sem_out = refs[3 * split:]
    it = pl.program_id(0)
    ff = pl.program_id(1)
    nblk = in_ref[it]
    row0 = pl.multiple_of(ir_ref[it] * MOE_BLOCK, MOE_BLOCK)
    span = MOE_ITEM_BLOCKS * MOE_BLOCK
    dq = x_ref.shape[1] // split

    @pl.when(nblk > 0)
    def _():
        @pl.when(ff == 0)
        def _():
            cp = pltpu.make_async_copy(xs_hbm.at[pl.ds(row0, span), :], x_ref, sem_in)
            cp.start()
            acc_ref[...] = jnp.zeros_like(acc_ref)
            cp.wait()

        gate = up = None
        for q in range(split):
            xq = x_ref[:, q * dq:(q + 1) * dq]
            g_q = jnp.dot(xq, wg_refs[q][0].astype(bf16), preferred_element_type=f32)
            u_q = jnp.dot(xq, wu_refs[q][0].astype(bf16), preferred_element_type=f32)
            gate = g_q if gate is None else gate + g_q
            up = u_q if up is None else up + u_q
        act = (gate * _sigmoid(gate) * up).astype(bf16)
        for q in range(split):
            acc_ref[:, q * dq:(q + 1) * dq] += jnp.dot(act, wd_refs[q][0].astype(bf16), preferred_element_type=f32)

        @pl.when(ff == n_ff - 1)
        def _():
            for sb in range(MOE_ITEM_BLOCKS):
                @pl.when(sb < nblk)
                def _(sb=sb):
                    rs = pl.ds(sb * MOE_BLOCK, MOE_BLOCK)
                    cp = pltpu.make_async_copy(acc_ref.at[rs, :],
                                               y_hbm.at[pl.ds(row0 + sb * MOE_BLOCK, MOE_BLOCK), :], sem_out)
                    cp.start()
                    cp.wait()

    @pl.when(jnp.logical_and(it == pl.num_programs(0) - 1, ff == n_ff - 1))
    def _():
        acc_ref[0:MOE_BLOCK, :] = jnp.zeros((MOE_BLOCK, acc_ref.shape[1]), f32)

        def fill(blk, c):
            cp = pltpu.make_async_copy(acc_ref.at[pl.ds(0, MOE_BLOCK), :],
                                       y_hbm.at[pl.ds(pl.multiple_of(blk * MOE_BLOCK, MOE_BLOCK), MOE_BLOCK), :],
                                       sem_out)
            cp.start()
            cp.wait()
            return c

        lax.fori_loop(nit_ref[1], y_hbm.shape[0] // MOE_BLOCK, fill, 0)


def _expert_ffn(item_e, item_r, item_n, n_items, xs, w_gate, w_up, w_down, max_items):
    rows, d = xs.shape
    dff = w_gate.shape[3]
    tf = _pick_tile(dff, MOE_FF_TILE, 128)
    n_ff = dff // tf
    span = MOE_ITEM_BLOCKS * MOE_BLOCK
    split = min(MOE_WEIGHT_SPLIT, d // LANE)
    dq = d // split
    assert dq % LANE == 0 and dq * split == d

    def f_eff(i, f, nit):
        return jnp.where(i < nit[0], f, n_ff - 1)

    def up_spec(q):
        return pl.BlockSpec((None, 1, dq, tf), lambda i, f, ie, ir, inn, nit: (0, ie[i], q, f_eff(i, f, nit)))

    def down_spec(q):
        return pl.BlockSpec((None, 1, tf, dq), lambda i, f, ie, ir, inn, nit: (0, ie[i], f_eff(i, f, nit), q))

    return pl.pallas_call(
        functools.partial(_expert_ffn_kernel, n_ff=n_ff, split=split),
        grid_spec=pltpu.PrefetchScalarGridSpec(
            num_scalar_prefetch=4,
            grid=(max_items, n_ff),
            in_specs=([pl.BlockSpec(memory_space=pl.ANY)] + [up_spec(q) for q in range(split)]
                      + [up_spec(q) for q in range(split)] + [down_spec(q) for q in range(split)]),
            out_specs=pl.BlockSpec(memory_space=pl.ANY),
            scratch_shapes=[pltpu.VMEM((span, d), bf16), pltpu.VMEM((span, d), f32),
                            pltpu.SemaphoreType.DMA(()), pltpu.SemaphoreType.DMA(())]),
        out_shape=jax.ShapeDtypeStruct((rows, d), f32),
        compiler_params=_params(("arbitrary", "arbitrary")),
        name="moe_expert_ffn",
    )(item_e, item_r, item_n, n_items, xs, *([w_gate] * split), *([w_up] * split), *([w_down] * split))


def _combine_kernel(d1_ref, d2_ref, y_hbm, x_ref, route_ref, g_ref, o_ref, buf_ref, sem):
    i = pl.program_id(0)
    n = pl.num_programs(0)

    def start(blk, slot):
        def body(r8, c):
            for u in range(DMA_ISSUE_UNROLL):
                r = r8 * DMA_ISSUE_UNROLL + u
                _row_copy(y_hbm, buf_ref.at[slot, 0], sem.at[slot], d1_ref[blk * ROW_BLOCK + r], r).start(priority=0)
                _row_copy(y_hbm, buf_ref.at[slot, 1], sem.at[slot], d2_ref[blk * ROW_BLOCK + r], r).start(priority=1)
            return c
        lax.fori_loop(0, ROW_BLOCK // DMA_ISSUE_UNROLL, body, 0)

    def wait(slot):
        for r in range(ROW_BLOCK):
            _row_copy(y_hbm, buf_ref.at[slot, 0], sem.at[slot], 0, r).wait()
            _row_copy(y_hbm, buf_ref.at[slot, 1], sem.at[slot], 0, r).wait()

    @pl.when(i == 0)
    def _():
        start(0, 0)

    slot = i % 2

    @pl.when(i + 1 < n)
    def _():
        start(i + 1, 1 - slot)

    wait(slot)
    route = route_ref[...]
    x = x_ref[...] + route[:, 2:3] * buf_ref[slot, 0] + route[:, 3:4] * buf_ref[slot, 1]
    ms = jnp.mean(x * x, axis=-1, keepdims=True)
    o_ref[...] = x * lax.rsqrt(ms + NORM_EPS) * g_ref[...]


def _combine(d1, d2, yb, x1, route, g):
    n, d = x1.shape
    return pl.pallas_call(
        _combine_kernel,
        grid_spec=pltpu.PrefetchScalarGridSpec(
            num_scalar_prefetch=2,
            grid=(n // ROW_BLOCK,),
            in_specs=[pl.BlockSpec(memory_space=pl.ANY),
                      pl.BlockSpec((ROW_BLOCK, d), lambda i, a, b: (i, 0)),
                      pl.BlockSpec((ROW_BLOCK, LANE), lambda i, a, b: (i, 0)),
                      pl.BlockSpec((1, d), lambda i, a, b: (0, 0))],
            out_specs=pl.BlockSpec((ROW_BLOCK, d), lambda i, a, b: (i, 0)),
            scratch_shapes=[pltpu.VMEM((2, 2, ROW_BLOCK, d), f32), pltpu.SemaphoreType.DMA((2,))]),
        out_shape=jax.ShapeDtypeStruct((n, d), f32),
        compiler_params=_params(("arbitrary",)),
        name="moe_combine",
    )(d1, d2, yb, x1, route, g.reshape(1, d))


def _moe(x1, norm_ffn, w_rg, b_rg, w_re, b_re, w_gate, w_up, w_down, norm_final):
    n, d = x1.shape
    n_exp = w_gate.shape[1]
    ng = w_rg.shape[1]
    w_r = jnp.zeros((d, LANE), f32).at[:, :ng].set(w_rg).at[:, ng:ng + n_exp].set(w_re)
    b_r = jnp.zeros((1, LANE), f32).at[0, :ng].set(b_rg).at[0, ng:ng + n_exp].set(b_re)
    route, cnt = _router(x1, norm_ffn, w_r, b_r)

    e1 = route[:, 0].astype(i32)
    e2 = route[:, 1].astype(i32)
    counts = cnt[0, ng:ng + n_exp].astype(i32)
    nblk_e = (counts + MOE_BLOCK - 1) // MOE_BLOCK
    blk_start = jnp.cumsum(nblk_e) - nblk_e
    d1 = blk_start[e1] * MOE_BLOCK + route[:, 4].astype(i32)
    d2 = blk_start[e2] * MOE_BLOCK + route[:, 5].astype(i32)
    max_blocks = -(-2 * n // MOE_BLOCK) + n_exp
    rows = (max_blocks + MOE_ITEM_BLOCKS) * MOE_BLOCK
    tok = jnp.arange(n, dtype=i32)
    row_tok = jnp.full((rows,), -1, i32).at[d1].set(tok).at[d2].set(tok)
    items_e = (nblk_e + MOE_ITEM_BLOCKS - 1) // MOE_ITEM_BLOCKS
    item_end = jnp.cumsum(items_e)
    n_items = item_end[-1]
    max_items = -(-max_blocks // MOE_ITEM_BLOCKS) + n_exp
    it = jnp.arange(max_items, dtype=i32)
    it_c = jnp.minimum(it, n_items - 1)
    item_e = jnp.minimum(jnp.searchsorted(item_end, it_c, side="right"), n_exp - 1).astype(i32)
    local = it_c - (item_end - items_e)[item_e]
    item_r = (blk_start[item_e] + local * MOE_ITEM_BLOCKS).astype(i32)
    item_n = jnp.where(it < n_items, jnp.minimum(MOE_ITEM_BLOCKS, nblk_e[item_e] - local * MOE_ITEM_BLOCKS), 0)

    xs = _dispatch(row_tok, x1, norm_ffn)
    meta = jnp.stack([n_items, jnp.sum(nblk_e)]).astype(i32)
    yb = _expert_ffn(item_e, item_r, item_n.astype(i32), meta, xs, w_gate, w_up, w_down, max_items)
    return _combine(d1, d2, yb, x1, route, norm_final)


def kernel(x_prompt, x_sample, cache_k, cache_v, cache_idx_k, state_hgrn, page_table, norm_mix, w_in,
           lb_logits, norm_hgrn_out, w_out, norm_ffn, router_group, router_group_bias, router_expert,
           router_expert_bias, expert_w_gate, expert_w_up, expert_w_down, norm_final):
    n_b, seq, d = x_prompt.shape
    n_s, n_tok, _ = x_sample.shape
    depth = norm_mix.shape[0]
    assert depth == 1
    kd = A_HEADS * A_EXPAND
    dh = d // B_HEADS
    kvd = B_KV_HEADS * dh
    group = B_HEADS // B_KV_HEADS
    np_rows = n_b * seq
    ns_rows = n_s * n_tok
    n_all = np_rows + ns_rows
    n_pad = -(-n_all // ROW_BLOCK) * ROW_BLOCK
    assert np_rows % ROW_BLOCK == 0

    lower = jnp.cumsum(jax.nn.softmax(lb_logits.astype(f32), axis=0), axis=0)[0]
    x_all = jnp.concatenate([x_prompt.reshape(np_rows, d), x_sample.reshape(ns_rows, d),
                             jnp.zeros((n_pad - n_all, d), f32)], axis=0)

    w = w_in[0]
    o_bq = 4 * kd
    o_bk = o_bq + d
    o_bqi = o_bk + 2 * kvd
    o_sm = o_bqi + IDX_HEADS * IDX_DIM
    o_g = o_sm + IDX_DIM + IDX_HEADS
    small_w = 2 * LANE
    h = _rmsnorm_bf16(x_all, norm_mix[0])
    z_a = _matmul(h, w[:, :o_bq].astype(bf16), name="proj_hgrn")
    z_bq = _matmul(h, w[:, o_bq:o_bk].astype(bf16), name="proj_q")
    z_kv = _matmul(h, w[:, o_bk:o_bqi].astype(bf16), name="proj_kv")
    z_qi = _matmul_heads(h, w[:, o_bqi:o_sm].astype(bf16), name="proj_qidx")
    w_small = jnp.zeros((d, small_w), bf16).at[:, :o_g - o_sm].set(w[:, o_sm:o_g].astype(bf16))
    z_sm = _matmul(h, w_small, name="proj_small")
    z_g = _matmul(h, w[:, o_g:].astype(bf16), name="proj_gates")

    rb = min(HGRN_ROWS, seq)
    oa_p, s_p = _hgrn(z_a, lower, norm_hgrn_out[0], n_b, seq, rb, min(HGRN_CHUNK, rb), rb,
                      min(HGRN_PROMPT_HEADS, A_HEADS))
    sp = HGRN_SAMPLE_ROWS
    zs = z_a[np_rows:n_all].reshape(n_s, n_tok, 4 * kd)
    zs = jnp.concatenate([zs, jnp.zeros((n_s, sp - n_tok, 4 * kd), f32)], axis=1).reshape(n_s * sp, 4 * kd)
    oa_sp, s_s = _hgrn(zs, lower, norm_hgrn_out[0], n_s, sp, sp, sp, n_tok, min(HGRN_SAMPLE_HEADS, A_HEADS),
                       s0=state_hgrn[0])
    oa_s = oa_sp.reshape(n_s, sp, d)[:, :n_tok].reshape(ns_rows, d)

    ob_p = _dsa_prompt(z_qi, z_sm, z_bq, z_kv, n_b, seq)
    sm_s = z_sm[np_rows:n_all]
    w_s = sm_s[:, IDX_DIM:IDX_DIM + IDX_HEADS].reshape(n_s, n_tok, IDX_HEADS) * (IDX_HEADS ** -0.5 * IDX_DIM ** -0.5)
    eye = jnp.eye(8, n_tok, dtype=f32)
    wsel = (eye[None, :, :, None] * w_s[:, None, :, :]).reshape(n_s, 8, n_tok * IDX_HEADS)
    qi_s = z_qi[np_rows // ROW_BLOCK:].transpose(0, 2, 1, 3).reshape(n_pad - np_rows, IDX_HEADS, IDX_DIM)
    qi_s = qi_s[:ns_rows].reshape(n_s, n_tok * IDX_HEADS, IDX_DIM)
    q_s = z_bq[np_rows:n_all].reshape(n_s, n_tok * B_HEADS, dh)

    def pad_rows_to_lane(a):
        return jnp.concatenate([a, jnp.zeros((n_s, LANE - n_tok) + a.shape[2:], f32)], axis=1)

    def new_kv(a):
        return pad_rows_to_lane(a.reshape(n_s, n_tok, B_KV_HEADS, dh)).reshape(n_s, LANE * B_KV_HEADS, dh)

    kv_s = z_kv[np_rows:n_all]
    ob_s = _dsa_sample(qi_s, wsel, q_s, pad_rows_to_lane(sm_s[:, :IDX_DIM].reshape(n_s, n_tok, IDX_DIM)),
                       new_kv(kv_s[:, :kvd]), new_kv(kv_s[:, kvd:]),
                       cache_k, cache_v, cache_idx_k, page_table, n_tok)
    ob_s = ob_s.reshape(ns_rows, d)

    pad_rows = jnp.zeros((n_pad - n_all, d), f32)
    oa = jnp.concatenate([oa_p, oa_s, pad_rows], axis=0)
    ob = jnp.concatenate([ob_p, ob_s, pad_rows], axis=0)
    m = _merge(z_g, oa, ob)
    x1 = _matmul(m, w_out[0].astype(bf16), res=x_all, name="out_proj")

    y = _moe(x1, norm_ffn[0], router_group[0], router_group_bias[0], router_expert[0], router_expert_bias[0],
             expert_w_gate, expert_w_up, expert_w_down, norm_final)

    kp = z_kv[:np_rows]
    ks = z_kv[np_rows:n_all]
    return (y[:np_rows].reshape(n_b, seq, d),
            y[np_rows:n_all].reshape(n_s, n_tok, d),
            kp[:, :kvd].reshape(1, n_b, seq, B_KV_HEADS, dh),
            kp[:, kvd:].reshape(1, n_b, seq, B_KV_HEADS, dh),
            z_sm[:np_rows, :IDX_DIM].reshape(1, n_b, seq, IDX_DIM),
            s_p[None],
            ks[:, :kvd].reshape(1, n_s, n_tok, B_KV_HEADS, dh),
            ks[:, kvd:].reshape(1, n_s, n_tok, B_KV_HEADS, dh),
            z_sm[np_rows:n_all, :IDX_DIM].reshape(1, n_s, n_tok, IDX_DIM),
            s_s[None])
```

```python
import functools
import math

import jax
import jax.numpy as jnp
import numpy as np
from jax import lax
from jax.experimental import pallas as pl
from jax.experimental.pallas import tpu as pltpu

A_HEADS = 32
A_EXPAND = 128
B_HEADS = 32
B_KV_HEADS = 8
IDX_HEADS = 32
IDX_DIM = 128
TOPK_MAX = 256
N_GROUPS = 8
EXPERTS_PER_GROUP = 8
NORM_EPS = 1e-6

LANE = 128
ROW_BLOCK = 128
MOE_BLOCK = 128
MOE_ITEM_BLOCKS = 4
MOE_FF_TILE = 256
MOE_OUT_TILE = 1024
HGRN_CHUNK = 64
HGRN_ROWS = 512
HGRN_SAMPLE_ROWS = 16
HGRN_SAMPLE_HEADS = 8
HGRN_PROMPT_HEADS = 4
HGRN_SAFE_DECAY = 60.0
SAMPLE_PAGES_PER_STEP = 8
DMA_ISSUE_UNROLL = 8
COUNT_CHAINS = 4
DSA_KEY_STEP = 256
NEG_BIG = -1e30
LOG2E = 1.4426950408889634
VMEM_LIMIT = 56 * 1024 * 1024

f32 = jnp.float32
bf16 = jnp.bfloat16
i32 = jnp.int32
HIGHEST = lax.Precision.HIGHEST
NT_DIMS = (((1,), (1,)), ((), ()))
TN_DIMS = (((0,), (0,)), ((), ()))


def _pick_tile(n, cap, align):
    t = (min(cap, n) // align) * align
    while t > align and n % t:
        t -= align
    assert t > 0 and n % t == 0, (n, cap, align)
    return t


def _params(sem, limit=VMEM_LIMIT):
    return pltpu.CompilerParams(dimension_semantics=sem, vmem_limit_bytes=limit)


def _sigmoid(x):
    return 1.0 / (1.0 + jnp.exp(-x))


def _log2(n):
    s = int(math.log2(n))
    assert 1 << s == n, n
    return s


def _rmsnorm_kernel(x_ref, g_ref, o_ref):
    x = x_ref[...]
    ms = jnp.mean(x * x, axis=-1, keepdims=True)
    o_ref[...] = (x * lax.rsqrt(ms + NORM_EPS) * g_ref[...]).astype(o_ref.dtype)


def _rmsnorm_bf16(x, g):
    n, d = x.shape
    return pl.pallas_call(
        _rmsnorm_kernel,
        grid=(n // ROW_BLOCK,),
        in_specs=[pl.BlockSpec((ROW_BLOCK, d), lambda i: (i, 0)),
                  pl.BlockSpec((1, d), lambda i: (0, 0))],
        out_specs=pl.BlockSpec((ROW_BLOCK, d), lambda i: (i, 0)),
        out_shape=jax.ShapeDtypeStruct((n, d), bf16),
        compiler_params=_params(("parallel",)),
        name="rmsnorm_bf16",
    )(x, g.reshape(1, d))


def _mm_kernel(a_ref, b_ref, o_ref):
    o_ref[...] = jnp.dot(a_ref[...], b_ref[...], preferred_element_type=f32)


def _mm_res_kernel(a_ref, b_ref, r_ref, o_ref):
    o_ref[...] = r_ref[...] + jnp.dot(a_ref[...], b_ref[...], preferred_element_type=f32)


def _matmul(a, b, res=None, name="matmul", col0=0, n=None):
    m, k = a.shape
    n = b.shape[1] if n is None else n
    tm = _pick_tile(m, 640, 128)
    tn = _pick_tile(math.gcd(n, col0) if col0 else n, 1024, 128)
    cb = col0 // tn
    in_specs = [pl.BlockSpec((tm, k), lambda j, i: (i, 0)),
                pl.BlockSpec((k, tn), lambda j, i: (0, cb + j))]
    args = [a, b]
    kern = _mm_kernel
    if res is not None:
        in_specs.append(pl.BlockSpec((tm, tn), lambda j, i: (i, j)))
        args.append(res)
        kern = _mm_res_kernel
    return pl.pallas_call(
        kern,
        grid=(n // tn, m // tm),
        in_specs=in_specs,
        out_specs=pl.BlockSpec((tm, tn), lambda j, i: (i, j)),
        out_shape=jax.ShapeDtypeStruct((m, n), f32),
        compiler_params=_params(("parallel", "parallel")),
        name=name,
    )(*args)


def _mm_heads_kernel(a_ref, b_ref, o_ref):
    acc = jnp.dot(a_ref[...], b_ref[...], preferred_element_type=f32)
    for r in range(o_ref.shape[0]):
        for hh in range(o_ref.shape[1]):
            o_ref[r, hh] = acc[r * ROW_BLOCK:(r + 1) * ROW_BLOCK, hh * LANE:(hh + 1) * LANE]


def _matmul_heads(a, b, name, col0, n):
    m, k = a.shape
    tm = _pick_tile(m, 640, ROW_BLOCK)
    tn = _pick_tile(math.gcd(n, col0) if col0 else n, 1024, LANE)
    cb = col0 // tn
    return pl.pallas_call(
        _mm_heads_kernel,
        grid=(n // tn, m // tm),
        in_specs=[pl.BlockSpec((tm, k), lambda j, i: (i, 0)),
                  pl.BlockSpec((k, tn), lambda j, i: (0, cb + j))],
        out_specs=pl.BlockSpec((tm // ROW_BLOCK, tn // LANE, ROW_BLOCK, LANE), lambda j, i: (i, j, 0, 0)),
        out_shape=jax.ShapeDtypeStruct((m // ROW_BLOCK, n // LANE, ROW_BLOCK, LANE), f32),
        compiler_params=_params(("parallel", "parallel")),
        name=name,
    )(a, b)


def _hgrn_static(c):
    levels = _log2(c)
    t = np.arange(c)
    cmat = np.zeros((2 * c, c), np.float32)
    cmat[:c] = t[None, :] <= t[:, None]
    cmat[c:] = 1.0
    lmat = np.zeros((levels * c, c), np.float32)
    lv = np.full((c, c), -1, np.int32)
    for l in range(levels):
        blk = c >> l
        half = blk // 2
        mid = (t // blk) * blk + half
        lmat[l * c:(l + 1) * c] = t[None, :] <= (mid - 1)[:, None]
        same = (t[:, None] // blk) == (t[None, :] // blk)
        up = ((t // half) % 2 == 1)[:, None]
        lo = ((t // half) % 2 == 0)[None, :]
        lv[same & up & lo] = l
    lv[t, t] = levels
    return cmat, lmat, lv, levels


def _split3(x):
    hi = x.astype(bf16)
    r = x - hi.astype(f32)
    mid = r.astype(bf16)
    lo = (r - mid.astype(f32)).astype(bf16)
    return jnp.concatenate([hi, mid, lo], axis=1)


def _sum3(y):
    w = y.shape[1] // 3
    return y[:, :w] + y[:, w:2 * w] + y[:, 2 * w:]


def _hgrn_kernel(*refs, rows, chunk, levels, n_valid, heads, has_init):
    if has_init:
        (zq_ref, zf_ref, zi_ref, zg_ref, lb_ref, gain_ref, cmat_ref, lmat_ref, lv_ref, s0_ref,
         oa_ref, sfin_ref, st_ref, a_ref, q_sc, k_sc, b_sc, p_sc) = refs
    else:
        (zq_ref, zf_ref, zi_ref, zg_ref, lb_ref, gain_ref, cmat_ref, lmat_ref, lv_ref,
         oa_ref, sfin_ref, st_ref, a_ref, q_sc, k_sc, b_sc, p_sc) = refs
        s0_ref = None
    tb = pl.program_id(2)
    c = chunk
    nc = rows // c
    dk = A_EXPAND

    @pl.when(tb == 0)
    def _():
        for hh in range(heads):
            if has_init:
                st_ref[hh] = s0_ref[0, hh].T
            else:
                st_ref[hh] = jnp.zeros(st_ref.shape[1:], f32)

    lv = lv_ref[...]
    cmat = cmat_ref[...]
    row_c = lax.broadcasted_iota(i32, (c, dk), 0)

    pre = []
    unsafe = None
    for hh in range(heads):
        ls = slice(hh * dk, (hh + 1) * dk)
        zq = zq_ref[:, ls]
        lb = lb_ref[:, ls]
        f = lb + (1.0 - lb) * _sigmoid(zf_ref[:, ls])
        logf = jnp.log(f)
        k = 1.0 - f
        q = zq * _sigmoid(zq)
        if n_valid < rows:
            live = lax.broadcasted_iota(i32, (rows, dk), 0) < n_valid
            logf = jnp.where(live, logf, 0.0)
            k = jnp.where(live, k, 0.0)
        parts = _split3(logf)
        bs, bls = [], []
        for ci in range(nc):
            y = _sum3(jnp.dot(cmat, parts[ci * c:(ci + 1) * c], preferred_element_type=f32))
            bs.append(y[:c])
            bls.append(y[c:])
        b = jnp.concatenate(bs, axis=0) if nc > 1 else bs[0]
        bl = jnp.concatenate(bls, axis=0) if nc > 1 else bls[0]
        head_unsafe = jnp.min(bl) < -HGRN_SAFE_DECAY
        unsafe = head_unsafe if unsafe is None else jnp.logical_or(unsafe, head_unsafe)
        pre.append(dict(q=q, k=k, b=b, bl=bl, parts=parts, qh_b=(q * jnp.exp(b)).astype(bf16),
                        kd=(k * jnp.exp(bl - b)).astype(bf16), v_b=zi_ref[:, ls].astype(bf16)))

    @pl.when(jnp.logical_not(unsafe))
    def _():
        for hh, hv in enumerate(pre):
            kt = (hv["k"] * jnp.exp(-hv["b"])).astype(bf16)
            for ci in range(nc):
                rs = slice(ci * c, (ci + 1) * c)
                al = lax.dot_general(hv["qh_b"][rs], kt[rs], NT_DIMS, preferred_element_type=f32)
                a_ref[hh * nc + ci] = jnp.where(lv >= 0, al, 0.0)

    @pl.when(unsafe)
    def _():
        for hh, hv in enumerate(pre):
            hr = slice(hh * rows, (hh + 1) * rows)
            q_sc[hr, :] = hv["q"]
            k_sc[hr, :] = hv["k"]
            b_sc[hr, :] = hv["b"]
            p_sc[hr, :] = hv["parts"]

        def safe_chunk(ci, carry):
            r0 = pl.multiple_of(ci * c, c)
            q_c = q_sc[pl.ds(r0, c), :]
            k_c = k_sc[pl.ds(r0, c), :]
            b_c = b_sc[pl.ds(r0, c), :]
            refs_b = _sum3(jnp.dot(lmat_ref[...], p_sc[pl.ds(r0, c), :], preferred_element_type=f32))
            a = jnp.where(lv == levels, jnp.sum(q_c * k_c, axis=-1, keepdims=True), 0.0)
            for l in range(levels):
                e = jnp.exp(-jnp.abs(b_c - refs_b[l * c:(l + 1) * c]))
                upper = ((row_c >> (levels - l - 1)) & 1) == 1
                qt = jnp.where(upper, q_c * e, 0.0).astype(bf16)
                kt = jnp.where(upper, 0.0, k_c * e).astype(bf16)
                al = lax.dot_general(qt, kt, NT_DIMS, preferred_element_type=f32)
                a = jnp.where(lv == l, al, a)
            a_ref[ci] = a
            return carry

        lax.fori_loop(0, heads * nc, safe_chunk, 0)

    for hh, hv in enumerate(pre):
        ls = slice(hh * dk, (hh + 1) * dk)
        gain = gain_ref[:, ls]
        zg = zg_ref[:, ls]
        st = st_ref[hh]
        for ci in range(nc):
            rs = slice(ci * c, (ci + 1) * c)
            o = (jnp.dot(a_ref[hh * nc + ci].astype(bf16), hv["v_b"][rs], preferred_element_type=f32)
                 + lax.dot_general(hv["qh_b"][rs], st.astype(bf16), NT_DIMS, preferred_element_type=f32))
            ms = jnp.mean(o * o, axis=-1, keepdims=True)
            zg_c = zg[rs]
            oa_ref[rs, ls] = o * lax.rsqrt(ms + NORM_EPS) * gain * (zg_c * _sigmoid(zg_c))
            st = (st * jnp.exp(hv["bl"][ci * c:ci * c + 1])
                  + lax.dot_general(hv["v_b"][rs], hv["kd"][rs], TN_DIMS, preferred_element_type=f32))
        st_ref[hh] = st

    @pl.when(tb == pl.num_programs(2) - 1)
    def _():
        for hh in range(heads):
            sfin_ref[0, hh] = st_ref[hh].T


def _hgrn(z_a, lb, gain, n_seq, seq_rows, row_block, chunk, n_valid, heads, s0=None):
    kd = A_HEADS * A_EXPAND
    dv = LANE
    assert z_a.shape[1] == 2 * kd + 2 * A_HEADS * dv and A_EXPAND == LANE and A_HEADS % heads == 0
    cmat, lmat, lv, levels = _hgrn_static(chunk)
    nrb = seq_rows // row_block
    hb = A_HEADS // heads
    wd = heads * LANE

    def zspec(seg):
        return pl.BlockSpec((row_block, wd), lambda b, h, t, seg=seg: (b * nrb + t, seg * hb + h))

    const = lambda b, h, t: (0, 0)
    in_specs = [zspec(0), zspec(1), zspec(2), zspec(3),
                pl.BlockSpec((1, wd), lambda b, h, t: (0, h)),
                pl.BlockSpec((1, wd), lambda b, h, t: (0, h)),
                pl.BlockSpec(cmat.shape, const), pl.BlockSpec(lmat.shape, const), pl.BlockSpec(lv.shape, const)]
    args = [z_a, z_a, z_a, z_a, lb.reshape(1, kd), gain.reshape(1, -1),
            jnp.asarray(cmat, bf16), jnp.asarray(lmat, bf16), jnp.asarray(lv)]
    if s0 is not None:
        in_specs.append(pl.BlockSpec((1, heads, A_EXPAND, dv), lambda b, h, t: (b, h, 0, 0)))
        args.append(s0)
    kern = functools.partial(_hgrn_kernel, rows=row_block, chunk=chunk, levels=levels, n_valid=n_valid,
                             heads=heads, has_init=s0 is not None)
    return pl.pallas_call(
        kern,
        grid=(n_seq, hb, nrb),
        in_specs=in_specs,
        out_specs=[pl.BlockSpec((row_block, wd), lambda b, h, t: (b * nrb + t, h)),
                   pl.BlockSpec((1, heads, A_EXPAND, dv), lambda b, h, t: (b, h, 0, 0))],
        out_shape=[jax.ShapeDtypeStruct((n_seq * seq_rows, A_HEADS * dv), f32),
                   jax.ShapeDtypeStruct((n_seq, A_HEADS, A_EXPAND, dv), f32)],
        scratch_shapes=[pltpu.VMEM((heads, dv, A_EXPAND), f32),
                        pltpu.VMEM((heads * row_block // chunk, chunk, chunk), f32),
                        pltpu.VMEM((heads * row_block, LANE), f32), pltpu.VMEM((heads * row_block, LANE), f32),
                        pltpu.VMEM((heads * row_block, LANE), f32),
                        pltpu.VMEM((heads * row_block, 3 * LANE), bf16)],
        compiler_params=_params(("parallel", "parallel", "arbitrary")),
        name="hgrn2_prompt" if s0 is None else "hgrn2_sample",
    )(*args)


def _sort_key(x):
    b = pltpu.bitcast(x + 0.0, i32)
    return b ^ ((b >> 31) & 0x7FFFFFFF)


def _kth_largest_key(key_ref, k, axis):
    shape = (1, key_ref.shape[1]) if axis == 0 else (key_ref.shape[0], 1)
    int_min = jnp.int32(-2 ** 31)

    n = key_ref.shape[axis]
    parts = max(p for p in range(1, COUNT_CHAINS + 1) if (n // LANE) % p == 0)
    seg = n // parts

    def step(i, lo):
        cand = lo ^ lax.shift_left(jnp.int32(1), 31 - i)
        cnt = None
        for pi in range(parts):
            sl = slice(pi * seg, (pi + 1) * seg)
            keys = key_ref[sl, :] if axis == 0 else key_ref[:, sl]
            c = jnp.sum((keys >= cand).astype(i32), axis=axis, keepdims=True)
            cnt = c if cnt is None else cnt + c
        return jnp.where(cnt >= k, cand, lo)

    return lax.fori_loop(0, 32, step, jnp.full(shape, int_min, i32))


def _dsa_prompt_kernel(qi_ref, smq_ref, smk_ref, q_ref, k_ref, v_ref, o_ref, mask_ref, key_ref, qs_ref,
                       *, kext, topk, qblk, group, qb0):
    qb = qb0 + pl.program_id(1)
    g = pl.program_id(2)
    n_tiles = kext // qblk

    def causal(r0):
        kpos = r0 + lax.broadcasted_iota(i32, (qblk, qblk), 0)
        qpos = qb * qblk + lax.broadcasted_iota(i32, (qblk, qblk), 1)
        return kpos <= qpos

    @pl.when(g == 0)
    def _():
        qs_ref[...] = qi_ref[0].reshape(IDX_HEADS * qblk, IDX_DIM).astype(bf16)
        w_t = smq_ref[...].T * (IDX_HEADS ** -0.5 * IDX_DIM ** -0.5)

        def score_tile(ti, carry):
            r0 = pl.multiple_of(ti * qblk, qblk)
            kid = smk_ref[pl.ds(r0, qblk), 0:IDX_DIM].astype(bf16)
            s_all = lax.dot_general(kid, qs_ref[...], NT_DIMS, preferred_element_type=f32)
            acc = jnp.zeros((qblk, qblk), f32)
            for h in range(IDX_HEADS):
                acc = acc + (jnp.maximum(s_all[:, h * qblk:(h + 1) * qblk], 0.0)
                             * w_t[IDX_DIM + h:IDX_DIM + h + 1, :])
            key_ref[pl.ds(r0, qblk), :] = _sort_key(jnp.where(causal(r0), acc, -jnp.inf))
            return carry

        lax.fori_loop(0, n_tiles, score_tile, 0)
        thr = _kth_largest_key(key_ref, topk, axis=0)
        keys = key_ref[...]
        kpos = lax.broadcasted_iota(i32, (kext, qblk), 0)
        qpos = qb * qblk + lax.broadcasted_iota(i32, (kext, qblk), 1)
        mask_ref[...] = jnp.where(jnp.logical_and(kpos <= qpos, keys >= thr), 0.0, -jnp.inf)
        need = topk - jnp.sum((keys > thr).astype(i32), axis=0, keepdims=True)
        n_tie = jnp.sum((keys == thr).astype(i32), axis=0, keepdims=True)

        @pl.when(jnp.max(n_tie - need) > 0)
        def _():
            strict = (lax.broadcasted_iota(i32, (qblk, qblk), 0) > lax.broadcasted_iota(i32, (qblk, qblk), 1))
            strict = jnp.where(strict, 1.0, 0.0).astype(bf16)
            need_f = need.astype(f32)

            def tile(ti, seen):
                r0 = pl.multiple_of(ti * qblk, qblk)
                kt = key_ref[pl.ds(r0, qblk), :]
                tie = kt == thr
                tie_f = jnp.where(tie, 1.0, 0.0)
                before = seen + jnp.dot(strict, tie_f.astype(bf16), preferred_element_type=f32)
                keep = jnp.logical_or(kt > thr, jnp.logical_and(tie, before < need_f))
                keep = jnp.logical_and(keep, causal(r0))
                mask_ref[pl.ds(r0, qblk), :] = jnp.where(keep, 0.0, -jnp.inf)
                return seen + jnp.sum(tie_f, axis=0, keepdims=True)

            lax.fori_loop(0, n_tiles, tile, jnp.zeros((1, qblk), f32))

    kk = k_ref[0:kext, :].astype(bf16)
    vt = v_ref[0:kext, :].T.astype(bf16)
    dh = kk.shape[1]
    for hq in range(group):
        qh = (q_ref[:, hq * dh:(hq + 1) * dh] * (dh ** -0.5 * LOG2E)).astype(bf16)
        s = lax.dot_general(kk, qh, NT_DIMS, preferred_element_type=f32) + mask_ref[...]
        m = jnp.max(s, axis=0, keepdims=True)
        p = jnp.exp2(s - m)
        den = jnp.sum(p, axis=0, keepdims=True)
        o_t = jnp.dot(vt, p.astype(bf16), preferred_element_type=f32) / den
        o_ref[:, hq * dh:(hq + 1) * dh] = o_t.T


def _dsa_prompt(bqi, small, bq, bkv, n_batch, seq):
    dh = bq.shape[1] // B_HEADS
    group = B_HEADS // B_KV_HEADS
    qblk = LANE
    nqb = seq // qblk
    topk = min(TOPK_MAX, seq // 4)
    parts = []
    qb0 = 0
    count = max(DSA_KEY_STEP // qblk, 1)
    assert nqb % count == 0
    while qb0 < nqb:
        kext = (qb0 + count) * qblk
        kern = functools.partial(_dsa_prompt_kernel, kext=kext, topk=topk, qblk=qblk, group=group, qb0=qb0)
        part = pl.pallas_call(
            kern,
            grid=(n_batch, count, B_KV_HEADS),
            in_specs=[pl.BlockSpec((1, IDX_HEADS, qblk, IDX_DIM), lambda b, j, g, q0=qb0: (b * nqb + q0 + j, 0, 0, 0)),
                      pl.BlockSpec((qblk, small.shape[1]), lambda b, j, g, q0=qb0: (b * nqb + q0 + j, 0)),
                      pl.BlockSpec((seq, small.shape[1]), lambda b, j, g: (b, 0)),
                      pl.BlockSpec((qblk, group * dh), lambda b, j, g, q0=qb0: (b * nqb + q0 + j, g)),
                      pl.BlockSpec((seq, dh), lambda b, j, g: (b, g)),
                      pl.BlockSpec((seq, dh), lambda b, j, g: (b, B_KV_HEADS + g))],
            out_specs=pl.BlockSpec((qblk, group * dh), lambda b, j, g, cnt=count: (b * cnt + j, g)),
            out_shape=jax.ShapeDtypeStruct((n_batch * count * qblk, B_HEADS * dh), f32),
            scratch_shapes=[pltpu.VMEM((kext, qblk), f32), pltpu.VMEM((kext, qblk), i32),
                            pltpu.VMEM((IDX_HEADS * qblk, IDX_DIM), bf16)],
            compiler_params=_params(("parallel", "parallel", "arbitrary")),
            name=f"dsa_prompt_k{kext}",
        )(bqi, small, small, bq, bkv, bkv)
        parts.append(part.reshape(n_batch, count * qblk, B_HEADS * dh))
        qb0 += count
    return jnp.concatenate(parts, axis=1).reshape(n_batch * seq, B_HEADS * dh)


def _dsa_sample_select_kernel(pt_ref, qi_ref, wsel_ref, knew_ref, cik_hbm, mask_ref,
                              slab_ref, sc_ref, key_ref, sel_ref, sem, *, n_pages, page, topk, n_tok, kvh):
    b = pl.program_id(0)
    n_seq = pl.num_programs(0)
    past = n_pages * page

    def page_copy(seq, slot, j):
        return pltpu.make_async_copy(cik_hbm.at[0, pt_ref[seq, j]],
                                     slab_ref.at[slot, pl.ds(pl.multiple_of(j * page, page), page), :],
                                     sem.at[slot])

    def start(seq, slot):
        lax.fori_loop(0, n_pages, lambda j, c: (page_copy(seq, slot, j).start(), c)[1], 0)

    def wait(seq, slot):
        lax.fori_loop(0, n_pages, lambda j, c: (page_copy(seq, slot, j).wait(), c)[1], 0)

    @pl.when(b == 0)
    def _():
        start(0, 0)

    slot = b % 2

    @pl.when(b + 1 < n_seq)
    def _():
        start(b + 1, 1 - slot)

    wait(b, slot)
    qi = qi_ref[0].astype(bf16)
    wsel = wsel_ref[0]

    def tile_scores(keys):
        s = lax.dot_general(qi, keys.astype(bf16), NT_DIMS, preferred_element_type=f32)
        return jnp.dot(wsel, jnp.maximum(s, 0.0), precision=HIGHEST, preferred_element_type=f32)

    sc_ref[:, 0:past] = tile_scores(slab_ref[slot])
    new = tile_scores(knew_ref[0])
    r = lax.broadcasted_iota(i32, new.shape, 0)
    cidx = lax.broadcasted_iota(i32, new.shape, 1)
    sc_ref[:, past:past + LANE] = jnp.where(jnp.logical_and(cidx <= r, cidx < n_tok), new, -jnp.inf)
    sc = sc_ref[...]
    key_ref[...] = _sort_key(sc)
    thr = _kth_largest_key(key_ref, topk, axis=1)
    keys = key_ref[...]
    sel = jnp.logical_and(keys >= thr, sc > -jnp.inf)
    sel_ref[...] = jnp.where(sel, 1.0, 0.0)
    need = topk - jnp.sum((keys > thr).astype(i32), axis=1, keepdims=True)
    n_tie = jnp.sum((keys == thr).astype(i32), axis=1, keepdims=True)

    @pl.when(jnp.max(n_tie - need) > 0)
    def _():
        strict = (lax.broadcasted_iota(i32, (LANE, LANE), 0) < lax.broadcasted_iota(i32, (LANE, LANE), 1))
        strict = jnp.where(strict, 1.0, 0.0).astype(bf16)
        need_f = need.astype(f32)

        def tile(ti, seen):
            c1 = pl.multiple_of(ti * LANE, LANE)
            kt = key_ref[:, pl.ds(c1, LANE)]
            tie = kt == thr
            tie_f = jnp.where(tie, 1.0, 0.0)
            before = seen + jnp.dot(tie_f.astype(bf16), strict, preferred_element_type=f32)
            keep = jnp.logical_or(kt > thr, jnp.logical_and(tie, before < need_f))
            keep = jnp.logical_and(keep, sc_ref[:, pl.ds(c1, LANE)] > -jnp.inf)
            sel_ref[:, pl.ds(c1, LANE)] = jnp.where(keep, 1.0, 0.0)
            return seen + jnp.sum(tie_f, axis=1, keepdims=True)

        lax.fori_loop(0, sc_ref.shape[1] // LANE, tile, jnp.zeros((8, 1), f32))

    lane = lax.broadcasted_iota(i32, (8, LANE), 1)

    def expand(ti, carry):
        c1 = pl.multiple_of(ti * LANE, LANE)
        hit = sel_ref[:, pl.ds(c1, LANE)]
        for j in range(kvh):
            src = j * (LANE // kvh) + (lane >> _log2(kvh))
            piece = jnp.take_along_axis(hit, src, axis=1)
            c2 = pl.multiple_of(ti * (LANE * kvh) + j * LANE, LANE)
            mask_ref[0, :, pl.ds(c2, LANE)] = jnp.where(piece > 0.5, 0.0, NEG_BIG)
        return carry

    lax.fori_loop(0, sc_ref.shape[1] // LANE, expand, 0)


def _dsa_sample_attn_kernel(pt_ref, q_ref, mask_ref, masknew_ref, knew_ref, vnew_ref, gm_ref, ck_hbm, cv_hbm,
                            o_ref, kslab, vslab, m_ref, l_ref, acc_ref, sem,
                            *, n_chunks, ch, n_tok, kvh, group, dh, page):
    b = pl.program_id(0)
    c = pl.program_id(1)
    t = b * n_chunks + c
    total = pl.num_programs(0) * n_chunks
    scale = dh ** -0.5

    def copies(seq, chunk, slot, j):
        pid = pt_ref[seq, chunk * ch + j]
        return (pltpu.make_async_copy(ck_hbm.at[0, pid], kslab.at[slot, j], sem.at[slot, 0]),
                pltpu.make_async_copy(cv_hbm.at[0, pid], vslab.at[slot, j], sem.at[slot, 1]))

    def start(seq, chunk, slot):
        for j in range(ch):
            ck, cv = copies(seq, chunk, slot, j)
            ck.start()
            cv.start()

    def wait(seq, chunk, slot):
        for j in range(ch):
            ck, cv = copies(seq, chunk, slot, j)
            ck.wait()
            cv.wait()

    @pl.when(t == 0)
    def _():
        start(0, 0, 0)

    slot = t % 2

    @pl.when(t + 1 < total)
    def _():
        nxt = t + 1
        start(nxt // n_chunks, nxt % n_chunks, 1 - slot)

    wait(b, c, slot)

    @pl.when(c == 0)
    def _():
        m_ref[...] = jnp.full_like(m_ref, NEG_BIG)
        l_ref[...] = jnp.zeros_like(l_ref)
        acc_ref[...] = jnp.zeros_like(acc_ref)

    qb = (q_ref[0] * scale).astype(bf16)
    heads = kvh * group

    def attend(keys, vals, mask8):
        n_cols = keys.shape[0]
        tok_mask = jnp.concatenate([jnp.broadcast_to(mask8[tk:tk + 1], (heads, n_cols))
                                    for tk in range(n_tok)], axis=0)
        madd = tok_mask + gm_ref[:, 0:n_cols]
        s = lax.dot_general(qb, keys, NT_DIMS, preferred_element_type=f32) + madd
        m_old = m_ref[...]
        m_new = jnp.maximum(m_old, jnp.max(s, axis=-1, keepdims=True))
        pr = jnp.where(madd == 0.0, jnp.exp(s - m_new), 0.0)
        alpha = jnp.exp(m_old - m_new)
        l_ref[...] = alpha * l_ref[...] + jnp.sum(pr, axis=-1, keepdims=True)
        acc_ref[...] = alpha * acc_ref[...] + jnp.dot(pr.astype(bf16), vals, preferred_element_type=f32)
        m_ref[...] = m_new

    attend(kslab[slot].reshape(ch * page * kvh, dh).astype(bf16),
           vslab[slot].reshape(ch * page * kvh, dh).astype(bf16), mask_ref[0])

    @pl.when(c == n_chunks - 1)
    def _():
        attend(knew_ref[0].astype(bf16), vnew_ref[0].astype(bf16), masknew_ref[0])
        o_ref[0] = acc_ref[...] / l_ref[...]


def _dsa_sample(qi, wsel, q, knew_idx, k_new, v_new, cache_k, cache_v, cache_idx_k, page_table, n_tok):
    n_seq, n_pages = page_table.shape
    page, kvh, dh = cache_k.shape[2], cache_k.shape[3], cache_k.shape[4]
    assert page == LANE and LANE % kvh == 0
    rows_i = qi.shape[1]
    rows = q.shape[1]
    heads = rows // n_tok
    group = heads // kvh
    past = n_pages * page
    kp = past + LANE
    topk = min(TOPK_MAX, (past + n_tok) // 4)
    ch = math.gcd(SAMPLE_PAGES_PER_STEP, n_pages)
    n_chunks = n_pages // ch
    cw = ch * page * kvh
    own = (np.arange(rows)[:, None] % heads) // group == (np.arange(cw)[None, :] % kvh)
    gm = np.where(own, 0.0, NEG_BIG).astype(np.float32)

    mask = pl.pallas_call(
        functools.partial(_dsa_sample_select_kernel, n_pages=n_pages, page=page, topk=topk, n_tok=n_tok,
                          kvh=kvh),
        grid_spec=pltpu.PrefetchScalarGridSpec(
            num_scalar_prefetch=1,
            grid=(n_seq,),
            in_specs=[pl.BlockSpec((1, rows_i, IDX_DIM), lambda b, pt: (b, 0, 0)),
                      pl.BlockSpec((1, 8, rows_i), lambda b, pt: (b, 0, 0)),
                      pl.BlockSpec((1, LANE, IDX_DIM), lambda b, pt: (b, 0, 0)),
                      pl.BlockSpec(memory_space=pl.ANY)],
            out_specs=pl.BlockSpec((1, 8, kp * kvh), lambda b, pt: (b, 0, 0)),
            scratch_shapes=[pltpu.VMEM((2, past, IDX_DIM), f32), pltpu.VMEM((8, kp), f32),
                            pltpu.VMEM((8, kp), i32), pltpu.VMEM((8, kp), f32), pltpu.SemaphoreType.DMA((2,))]),
        out_shape=jax.ShapeDtypeStruct((n_seq, 8, kp * kvh), f32),
        compiler_params=_params(("arbitrary",)),
        name="dsa_sample_select",
    )(page_table, qi, wsel, knew_idx, cache_idx_k)

    return pl.pallas_call(
        functools.partial(_dsa_sample_attn_kernel, n_chunks=n_chunks, ch=ch, n_tok=n_tok, kvh=kvh, group=group,
                          dh=dh, page=page),
        grid_spec=pltpu.PrefetchScalarGridSpec(
            num_scalar_prefetch=1,
            grid=(n_seq, n_chunks),
            in_specs=[pl.BlockSpec((1, rows, dh), lambda b, c, pt: (b, 0, 0)),
                      pl.BlockSpec((1, 8, cw), lambda b, c, pt: (b, 0, c)),
                      pl.BlockSpec((1, 8, LANE * kvh), lambda b, c, pt: (b, 0, past // LANE)),
                      pl.BlockSpec((1, LANE * kvh, dh), lambda b, c, pt: (b, 0, 0)),
                      pl.BlockSpec((1, LANE * kvh, dh), lambda b, c, pt: (b, 0, 0)),
                      pl.BlockSpec(gm.shape, lambda b, c, pt: (0, 0)),
                      pl.BlockSpec(memory_space=pl.ANY),
                      pl.BlockSpec(memory_space=pl.ANY)],
            out_specs=pl.BlockSpec((1, rows, dh), lambda b, c, pt: (b, 0, 0)),
            scratch_shapes=[pltpu.VMEM((2, ch, page, kvh, dh), f32), pltpu.VMEM((2, ch, page, kvh, dh), f32),
                            pltpu.VMEM((rows, 1), f32), pltpu.VMEM((rows, 1), f32),
                            pltpu.VMEM((rows, dh), f32), pltpu.SemaphoreType.DMA((2, 2))]),
        out_shape=jax.ShapeDtypeStruct((n_seq, rows, dh), f32),
        compiler_params=_params(("arbitrary", "arbitrary")),
        name="dsa_sample_attn",
    )(page_table, q, mask, mask, k_new, v_new, jnp.asarray(gm), cache_k, cache_v)


def _merge_kernel(ga_ref, gb_ref, oa_ref, ob_ref, o_ref):
    o_ref[...] = (_sigmoid(ga_ref[...]) * oa_ref[...] + _sigmoid(gb_ref[...]) * ob_ref[...]).astype(o_ref.dtype)


def _merge(gates, oa, ob):
    n, d = oa.shape
    tn = _pick_tile(d, 1024, 128)
    nj = d // tn
    return pl.pallas_call(
        _merge_kernel,
        grid=(n // ROW_BLOCK, nj),
        in_specs=[pl.BlockSpec((ROW_BLOCK, tn), lambda i, j: (i, j)),
                  pl.BlockSpec((ROW_BLOCK, tn), lambda i, j: (i, nj + j)),
                  pl.BlockSpec((ROW_BLOCK, tn), lambda i, j: (i, j)),
                  pl.BlockSpec((ROW_BLOCK, tn), lambda i, j: (i, j))],
        out_specs=pl.BlockSpec((ROW_BLOCK, tn), lambda i, j: (i, j)),
        out_shape=jax.ShapeDtypeStruct((n, d), bf16),
        compiler_params=_params(("parallel", "parallel")),
        name="merge_branches",
    )(gates, gates, oa, ob)


def _router_kernel(x_ref, g_ref, w_ref, b_ref, route_ref, cnt_ref, carry_ref):
    i = pl.program_id(0)

    @pl.when(i == 0)
    def _():
        carry_ref[...] = jnp.zeros_like(carry_ref)

    x = x_ref[...]
    ms = jnp.mean(x * x, axis=-1, keepdims=True)
    h = x * lax.rsqrt(ms + NORM_EPS) * g_ref[...]
    logits = jnp.dot(h, w_ref[...], precision=HIGHEST, preferred_element_type=f32) + b_ref[...]
    lane = lax.broadcasted_iota(i32, logits.shape, 1)
    big = jnp.int32(1 << 20)
    ng, epg = N_GROUPS, EXPERTS_PER_GROUP

    def first_max(vals):
        mx = jnp.max(vals, axis=-1, keepdims=True)
        idx = jnp.min(jnp.where(vals == mx, lane, big), axis=-1, keepdims=True)
        return mx, idx

    lg = jnp.where(lane < ng, logits, -jnp.inf)
    mg, g_sel = first_max(lg)
    p_g = 1.0 / jnp.sum(jnp.exp(lg - mg), axis=-1, keepdims=True)
    lo = ng + g_sel * epg
    le = jnp.where(jnp.logical_and(lane >= lo, lane < lo + epg), logits, -jnp.inf)
    m1, i1 = first_max(le)
    se = jnp.sum(jnp.exp(le - m1), axis=-1, keepdims=True)
    m2, i2 = first_max(jnp.where(lane == i1, -jnp.inf, le))
    p1 = 1.0 / se
    p2 = jnp.exp(m2 - m1) / se
    w1 = p_g * p1 / (p1 + p2)
    w2 = p_g * p2 / (p1 + p2)

    hot = jnp.logical_or(lane == i1, lane == i2)
    hot_b = jnp.where(hot, 1.0, 0.0).astype(bf16)
    n = x.shape[0]
    strict = (lax.broadcasted_iota(i32, (n, n), 0) > lax.broadcasted_iota(i32, (n, n), 1))
    before = jnp.dot(jnp.where(strict, 1.0, 0.0).astype(bf16), hot_b, preferred_element_type=f32)
    before = before + carry_ref[...]
    r1 = jnp.sum(jnp.where(lane == i1, before, 0.0), axis=-1, keepdims=True)
    r2 = jnp.sum(jnp.where(lane == i2, before, 0.0), axis=-1, keepdims=True)
    carry_ref[...] = carry_ref[...] + jnp.sum(jnp.where(hot, 1.0, 0.0), axis=0, keepdims=True)

    out = jnp.where(lane == 0, (i1 - ng).astype(f32), 0.0)
    out = jnp.where(lane == 1, (i2 - ng).astype(f32), out)
    out = jnp.where(lane == 2, w1, out)
    out = jnp.where(lane == 3, w2, out)
    out = jnp.where(lane == 4, r1, out)
    out = jnp.where(lane == 5, r2, out)
    route_ref[...] = out
    cnt_ref[...] = carry_ref[...]


def _router(x1, g, w_r, b_r):
    n, d = x1.shape
    return pl.pallas_call(
        _router_kernel,
        grid=(n // ROW_BLOCK,),
        in_specs=[pl.BlockSpec((ROW_BLOCK, d), lambda i: (i, 0)),
                  pl.BlockSpec((1, d), lambda i: (0, 0)),
                  pl.BlockSpec((d, LANE), lambda i: (0, 0)),
                  pl.BlockSpec((1, LANE), lambda i: (0, 0))],
        out_specs=[pl.BlockSpec((ROW_BLOCK, LANE), lambda i: (i, 0)),
                   pl.BlockSpec((1, LANE), lambda i: (0, 0))],
        out_shape=[jax.ShapeDtypeStruct((n, LANE), f32), jax.ShapeDtypeStruct((1, LANE), f32)],
        scratch_shapes=[pltpu.VMEM((1, LANE), f32)],
        compiler_params=_params(("arbitrary",)),
        name="moe_router",
    )(x1, g.reshape(1, d), w_r, b_r)


def _row_copy(src_hbm, dst_vmem, sem, src_row, dst_row):
    return pltpu.make_async_copy(src_hbm.at[pl.ds(src_row, 1), :], dst_vmem.at[pl.ds(dst_row, 1), :], sem)


def _dispatch_kernel(tok_ref, x_hbm, g_ref, o_ref, buf_ref, sem):
    i = pl.program_id(0)
    n = pl.num_programs(0)

    def for_real_rows(blk, fn):
        def body(r8, c):
            for u in range(DMA_ISSUE_UNROLL):
                r = r8 * DMA_ISSUE_UNROLL + u
                tok = tok_ref[blk * MOE_BLOCK + r]

                @pl.when(tok >= 0)
                def _(tok=tok, r=r, u=u):
                    fn(tok, r, u)
            return c
        lax.fori_loop(0, MOE_BLOCK // DMA_ISSUE_UNROLL, body, 0)

    def start(blk, slot):
        for_real_rows(blk, lambda tok, r, u: _row_copy(x_hbm, buf_ref.at[slot], sem.at[slot], tok, r).start(
            priority=u % 2))

    def wait(blk, slot):
        for_real_rows(blk, lambda tok, r, u: _row_copy(x_hbm, buf_ref.at[slot], sem.at[slot], tok, r).wait())

    @pl.when(i == 0)
    def _():
        buf_ref[...] = jnp.zeros_like(buf_ref)
        start(0, 0)

    slot = i % 2

    @pl.when(i + 1 < n)
    def _():
        start(i + 1, 1 - slot)

    wait(i, slot)
    x = buf_ref[slot]
    ms = jnp.mean(x * x, axis=-1, keepdims=True)
    o_ref[...] = (x * lax.rsqrt(ms + NORM_EPS) * g_ref[...]).astype(o_ref.dtype)


def _dispatch(row_tok, x1, g):
    n, d = x1.shape
    rows = row_tok.shape[0]
    return pl.pallas_call(
        _dispatch_kernel,
        grid_spec=pltpu.PrefetchScalarGridSpec(
            num_scalar_prefetch=1,
            grid=(rows // MOE_BLOCK,),
            in_specs=[pl.BlockSpec(memory_space=pl.ANY),
                      pl.BlockSpec((1, d), lambda i, tok: (0, 0))],
            out_specs=pl.BlockSpec((MOE_BLOCK, d), lambda i, tok: (i, 0)),
            scratch_shapes=[pltpu.VMEM((2, MOE_BLOCK, d), f32), pltpu.SemaphoreType.DMA((2,))]),
        out_shape=jax.ShapeDtypeStruct((rows, d), bf16),
        compiler_params=_params(("arbitrary",)),
        name="moe_dispatch",
    )(row_tok, x1, g.reshape(1, d))


def _expert_ffn_kernel(ie_ref, ir_ref, in_ref, nit_ref, xs_hbm, wg_ref, wu_ref, wd_ref, y_hbm,
                       x_ref, act_ref, acc_ref, sem_in, sem_out, *, n_ff, n_out):
    it = pl.program_id(0)
    st = pl.program_id(1)
    nblk = in_ref[it]
    row0 = pl.multiple_of(ir_ref[it] * MOE_BLOCK, MOE_BLOCK)
    span = MOE_ITEM_BLOCKS * MOE_BLOCK
    tf = wg_ref.shape[2]
    to = wd_ref.shape[2]

    @pl.when(nblk > 0)
    def _():
        @pl.when(st == 0)
        def _():
            cp = pltpu.make_async_copy(xs_hbm.at[pl.ds(row0, span), :], x_ref, sem_in)
            cp.start()
            cp.wait()

        @pl.when(st < n_ff)
        def _():
            x = x_ref[...]
            gate = jnp.dot(x, wg_ref[0].astype(bf16), preferred_element_type=f32)
            up = jnp.dot(x, wu_ref[0].astype(bf16), preferred_element_type=f32)
            act_ref[:, pl.ds(pl.multiple_of(st * tf, tf), tf)] = (gate * _sigmoid(gate) * up).astype(bf16)

        @pl.when(st >= n_ff)
        def _():
            c0 = pl.multiple_of((st - n_ff) * to, to)
            acc_ref[:, pl.ds(c0, to)] = jnp.dot(act_ref[...], wd_ref[0].astype(bf16), preferred_element_type=f32)

        @pl.when(st == n_ff + n_out - 1)
        def _():
            for sb in range(MOE_ITEM_BLOCKS):
                @pl.when(sb < nblk)
                def _(sb=sb):
                    rs = pl.ds(sb * MOE_BLOCK, MOE_BLOCK)
                    cp = pltpu.make_async_copy(acc_ref.at[rs, :],
                                               y_hbm.at[pl.ds(row0 + sb * MOE_BLOCK, MOE_BLOCK), :], sem_out)
                    cp.start()
                    cp.wait()

    @pl.when(jnp.logical_and(it == pl.num_programs(0) - 1, st == n_ff + n_out - 1))
    def _():
        acc_ref[0:MOE_BLOCK, :] = jnp.zeros((MOE_BLOCK, acc_ref.shape[1]), f32)

        def fill(blk, c):
            cp = pltpu.make_async_copy(acc_ref.at[pl.ds(0, MOE_BLOCK), :],
                                       y_hbm.at[pl.ds(pl.multiple_of(blk * MOE_BLOCK, MOE_BLOCK), MOE_BLOCK), :],
                                       sem_out)
            cp.start()
            cp.wait()
            return c

        lax.fori_loop(nit_ref[1], y_hbm.shape[0] // MOE_BLOCK, fill, 0)


def _expert_ffn(item_e, item_r, item_n, n_items, xs, w_gate, w_up, w_down, max_items):
    rows, d = xs.shape
    dff = w_gate.shape[3]
    tf = _pick_tile(dff, MOE_FF_TILE, LANE)
    to = _pick_tile(d, MOE_OUT_TILE, LANE)
    n_ff = dff // tf
    n_out = d // to
    span = MOE_ITEM_BLOCKS * MOE_BLOCK

    def ff_idx(i, s, nit):
        return jnp.where(i < nit[0], jnp.minimum(s, n_ff - 1), n_ff - 1)

    def out_idx(i, s, nit):
        return jnp.where(i < nit[0], jnp.maximum(s - n_ff, 0), n_out - 1)

    return pl.pallas_call(
        functools.partial(_expert_ffn_kernel, n_ff=n_ff, n_out=n_out),
        grid_spec=pltpu.PrefetchScalarGridSpec(
            num_scalar_prefetch=4,
            grid=(max_items, n_ff + n_out),
            in_specs=[pl.BlockSpec(memory_space=pl.ANY),
                      pl.BlockSpec((None, 1, d, tf), lambda i, s, ie, ir, inn, nit: (0, ie[i], 0, ff_idx(i, s, nit))),
                      pl.BlockSpec((None, 1, d, tf), lambda i, s, ie, ir, inn, nit: (0, ie[i], 0, ff_idx(i, s, nit))),
                      pl.BlockSpec((None, 1, dff, to), lambda i, s, ie, ir, inn, nit: (0, ie[i], 0, out_idx(i, s, nit)))],
            out_specs=pl.BlockSpec(memory_space=pl.ANY),
            scratch_shapes=[pltpu.VMEM((span, d), bf16), pltpu.VMEM((span, dff), bf16), pltpu.VMEM((span, d), f32),
                            pltpu.SemaphoreType.DMA(()), pltpu.SemaphoreType.DMA(())]),
        out_shape=jax.ShapeDtypeStruct((rows, d), f32),
        compiler_params=_params(("arbitrary", "arbitrary")),
        name="moe_expert_ffn",
    )(item_e, item_r, item_n, n_items, xs, w_gate, w_up, w_down)


def _combine_kernel(d1_ref, d2_ref, y_hbm, x_ref, route_ref, g_ref, o_ref, buf_ref, sem):
    i = pl.program_id(0)
    n = pl.num_programs(0)

    def start(blk, slot):
        def body(r8, c):
            for u in range(DMA_ISSUE_UNROLL):
                r = r8 * DMA_ISSUE_UNROLL + u
                _row_copy(y_hbm, buf_ref.at[slot, 0], sem.at[slot], d1_ref[blk * ROW_BLOCK + r], r).start(priority=0)
                _row_copy(y_hbm, buf_ref.at[slot, 1], sem.at[slot], d2_ref[blk * ROW_BLOCK + r], r).start(priority=1)
            return c
        lax.fori_loop(0, ROW_BLOCK // DMA_ISSUE_UNROLL, body, 0)

    def wait(slot):
        for r in range(ROW_BLOCK):
            _row_copy(y_hbm, buf_ref.at[slot, 0], sem.at[slot], 0, r).wait()
            _row_copy(y_hbm, buf_ref.at[slot, 1], sem.at[slot], 0, r).wait()

    @pl.when(i == 0)
    def _():
        start(0, 0)

    slot = i % 2

    @pl.when(i + 1 < n)
    def _():
        start(i + 1, 1 - slot)

    wait(slot)
    route = route_ref[...]
    x = x_ref[...] + route[:, 2:3] * buf_ref[slot, 0] + route[:, 3:4] * buf_ref[slot, 1]
    ms = jnp.mean(x * x, axis=-1, keepdims=True)
    o_ref[...] = x * lax.rsqrt(ms + NORM_EPS) * g_ref[...]


def _combine(d1, d2, yb, x1, route, g):
    n, d = x1.shape
    return pl.pallas_call(
        _combine_kernel,
        grid_spec=pltpu.PrefetchScalarGridSpec(
            num_scalar_prefetch=2,
            grid=(n // ROW_BLOCK,),
            in_specs=[pl.BlockSpec(memory_space=pl.ANY),
                      pl.BlockSpec((ROW_BLOCK, d), lambda i, a, b: (i, 0)),
                      pl.BlockSpec((ROW_BLOCK, LANE), lambda i, a, b: (i, 0)),
                      pl.BlockSpec((1, d), lambda i, a, b: (0, 0))],
            out_specs=pl.BlockSpec((ROW_BLOCK, d), lambda i, a, b: (i, 0)),
            scratch_shapes=[pltpu.VMEM((2, 2, ROW_BLOCK, d), f32), pltpu.SemaphoreType.DMA((2,))]),
        out_shape=jax.ShapeDtypeStruct((n, d), f32),
        compiler_params=_params(("arbitrary",)),
        name="moe_combine",
    )(d1, d2, yb, x1, route, g.reshape(1, d))


def _moe(x1, norm_ffn, w_rg, b_rg, w_re, b_re, w_gate, w_up, w_down, norm_final):
    n, d = x1.shape
    n_exp = w_gate.shape[1]
    ng = w_rg.shape[1]
    w_r = jnp.zeros((d, LANE), f32).at[:, :ng].set(w_rg).at[:, ng:ng + n_exp].set(w_re)
    b_r = jnp.zeros((1, LANE), f32).at[0, :ng].set(b_rg).at[0, ng:ng + n_exp].set(b_re)
    route, cnt = _router(x1, norm_ffn, w_r, b_r)

    e1 = route[:, 0].astype(i32)
    e2 = route[:, 1].astype(i32)
    counts = cnt[0, ng:ng + n_exp].astype(i32)
    nblk_e = (counts + MOE_BLOCK - 1) // MOE_BLOCK
    blk_start = jnp.cumsum(nblk_e) - nblk_e
    d1 = blk_start[e1] * MOE_BLOCK + route[:, 4].astype(i32)
    d2 = blk_start[e2] * MOE_BLOCK + route[:, 5].astype(i32)
    max_blocks = -(-2 * n // MOE_BLOCK) + n_exp
    rows = (max_blocks + MOE_ITEM_BLOCKS) * MOE_BLOCK
    tok = jnp.arange(n, dtype=i32)
    row_tok = jnp.full((rows,), -1, i32).at[d1].set(tok).at[d2].set(tok)
    items_e = (nblk_e + MOE_ITEM_BLOCKS - 1) // MOE_ITEM_BLOCKS
    item_end = jnp.cumsum(items_e)
    n_items = item_end[-1]
    max_items = -(-max_blocks // MOE_ITEM_BLOCKS) + n_exp
    it = jnp.arange(max_items, dtype=i32)
    it_c = jnp.minimum(it, n_items - 1)
    item_e = jnp.minimum(jnp.searchsorted(item_end, it_c, side="right"), n_exp - 1).astype(i32)
    local = it_c - (item_end - items_e)[item_e]
    item_r = (blk_start[item_e] + local * MOE_ITEM_BLOCKS).astype(i32)
    item_n = jnp.where(it < n_items, jnp.minimum(MOE_ITEM_BLOCKS, nblk_e[item_e] - local * MOE_ITEM_BLOCKS), 0)

    xs = _dispatch(row_tok, x1, norm_ffn)
    meta = jnp.stack([n_items, jnp.sum(nblk_e)]).astype(i32)
    yb = _expert_ffn(item_e, item_r, item_n.astype(i32), meta, xs, w_gate, w_up, w_down, max_items)
    return _combine(d1, d2, yb, x1, route, norm_final)


def kernel(x_prompt, x_sample, cache_k, cache_v, cache_idx_k, state_hgrn, page_table, norm_mix, w_in,
           lb_logits, norm_hgrn_out, w_out, norm_ffn, router_group, router_group_bias, router_expert,
           router_expert_bias, expert_w_gate, expert_w_up, expert_w_down, norm_final):
    n_b, seq, d = x_prompt.shape
    n_s, n_tok, _ = x_sample.shape
    depth = norm_mix.shape[0]
    assert depth == 1
    kd = A_HEADS * A_EXPAND
    dh = d // B_HEADS
    kvd = B_KV_HEADS * dh
    group = B_HEADS // B_KV_HEADS
    np_rows = n_b * seq
    ns_rows = n_s * n_tok
    n_all = np_rows + ns_rows
    n_pad = -(-n_all // ROW_BLOCK) * ROW_BLOCK
    assert np_rows % ROW_BLOCK == 0

    lower = jnp.cumsum(jax.nn.softmax(lb_logits.astype(f32), axis=0), axis=0)[0]
    x_all = jnp.concatenate([x_prompt.reshape(np_rows, d), x_sample.reshape(ns_rows, d),
                             jnp.zeros((n_pad - n_all, d), f32)], axis=0)

    w = w_in[0]
    o_bq = 4 * kd
    o_bk = o_bq + d
    o_bqi = o_bk + 2 * kvd
    o_sm = o_bqi + IDX_HEADS * IDX_DIM
    o_g = o_sm + IDX_DIM + IDX_HEADS
    small_w = 2 * LANE
    h = _rmsnorm_bf16(x_all, norm_mix[0])
    wb = w.astype(bf16)
    z_a = _matmul(h, wb, name="proj_hgrn", col0=0, n=o_bq)
    z_bq = _matmul(h, wb, name="proj_q", col0=o_bq, n=o_bk - o_bq)
    z_kv = _matmul(h, wb, name="proj_kv", col0=o_bk, n=o_bqi - o_bk)
    z_qi = _matmul_heads(h, wb, "proj_qidx", o_bqi, o_sm - o_bqi)
    w_small = jnp.zeros((d, small_w), bf16).at[:, :o_g - o_sm].set(wb[:, o_sm:o_g])
    z_sm = _matmul(h, w_small, name="proj_small")
    z_g = _matmul(h, wb[:, o_g:], name="proj_gates")

    rb = min(HGRN_ROWS, seq)
    oa_p, s_p = _hgrn(z_a, lower, norm_hgrn_out[0], n_b, seq, rb, min(HGRN_CHUNK, rb), rb,
                      min(HGRN_PROMPT_HEADS, A_HEADS))
    sp = HGRN_SAMPLE_ROWS
    zs = z_a[np_rows:n_all].reshape(n_s, n_tok, 4 * kd)
    zs = jnp.concatenate([zs, jnp.zeros((n_s, sp - n_tok, 4 * kd), f32)], axis=1).reshape(n_s * sp, 4 * kd)
    oa_sp, s_s = _hgrn(zs, lower, norm_hgrn_out[0], n_s, sp, sp, sp, n_tok, min(HGRN_SAMPLE_HEADS, A_HEADS),
                       s0=state_hgrn[0])
    oa_s = oa_sp.reshape(n_s, sp, d)[:, :n_tok].reshape(ns_rows, d)

    ob_p = _dsa_prompt(z_qi, z_sm, z_bq, z_kv, n_b, seq)
    sm_s = z_sm[np_rows:n_all]
    w_s = sm_s[:, IDX_DIM:IDX_DIM + IDX_HEADS].reshape(n_s, n_tok, IDX_HEADS) * (IDX_HEADS ** -0.5 * IDX_DIM ** -0.5)
    eye = jnp.eye(8, n_tok, dtype=f32)
    wsel = (eye[None, :, :, None] * w_s[:, None, :, :]).reshape(n_s, 8, n_tok * IDX_HEADS)
    qi_s = z_qi[np_rows // ROW_BLOCK:].transpose(0, 2, 1, 3).reshape(n_pad - np_rows, IDX_HEADS, IDX_DIM)
    qi_s = qi_s[:ns_rows].reshape(n_s, n_tok * IDX_HEADS, IDX_DIM)
    q_s = z_bq[np_rows:n_all].reshape(n_s, n_tok * B_HEADS, dh)

    def pad_rows_to_lane(a):
        return jnp.concatenate([a, jnp.zeros((n_s, LANE - n_tok) + a.shape[2:], f32)], axis=1)

    def new_kv(a):
        return pad_rows_to_lane(a.reshape(n_s, n_tok, B_KV_HEADS, dh)).reshape(n_s, LANE * B_KV_HEADS, dh)

    kv_s = z_kv[np_rows:n_all]
    ob_s = _dsa_sample(qi_s, wsel, q_s, pad_rows_to_lane(sm_s[:, :IDX_DIM].reshape(n_s, n_tok, IDX_DIM)),
                       new_kv(kv_s[:, :kvd]), new_kv(kv_s[:, kvd:]),
                       cache_k, cache_v, cache_idx_k, page_table, n_tok)
    ob_s = ob_s.reshape(ns_rows, d)

    pad_rows = jnp.zeros((n_pad - n_all, d), f32)
    oa = jnp.concatenate([oa_p, oa_s, pad_rows], axis=0)
    ob = jnp.concatenate([ob_p, ob_s, pad_rows], axis=0)
    m = _merge(z_g, oa, ob)
    x1 = _matmul(m, w_out[0].astype(bf16), res=x_all, name="out_proj")

    y = _moe(x1, norm_ffn[0], router_group[0], router_group_bias[0], router_expert[0], router_expert_bias[0],
             expert_w_gate, expert_w_up, expert_w_down, norm_final)

    kp = z_kv[:np_rows]
    ks = z_kv[np_rows:n_all]
    return (y[:np_rows].reshape(n_b, seq, d),
            y[np_rows:n_all].reshape(n_s, n_tok, d),
            kp[:, :kvd].reshape(1, n_b, seq, B_KV_HEADS, dh),
            kp[:, kvd:].reshape(1, n_b, seq, B_KV_HEADS, dh),
            z_sm[:np_rows, :IDX_DIM].reshape(1, n_b, seq, IDX_DIM),
            s_p[None],
            ks[:, :kvd].reshape(1, n_s, n_tok, B_KV_HEADS, dh),
            ks[:, kvd:].reshape(1, n_s, n_tok, B_KV_HEADS, dh),
            z_sm[np_rows:n_all, :IDX_DIM].reshape(1, n_s, n_tok, IDX_DIM),
            s_s[None])
```

```python
import functools
import math

import jax
import jax.numpy as jnp
import numpy as np
from jax import lax
from jax.experimental import pallas as pl
from jax.experimental.pallas import tpu as pltpu

A_HEADS = 32
A_EXPAND = 128
B_HEADS = 32
B_KV_HEADS = 8
IDX_HEADS = 32
IDX_DIM = 128
TOPK_MAX = 256
N_GROUPS = 8
EXPERTS_PER_GROUP = 8
NORM_EPS = 1e-6

LANE = 128
ROW_BLOCK = 128
PROJ_ROW_TILE = 1664
PROJ_COL_TILE = 512
MOE_BLOCK = 128
MOE_ITEM_BLOCKS = 4
MOE_FF_TILE = 256
HGRN_CHUNK = 64
HGRN_ROWS = 512
HGRN_SAMPLE_ROWS = 16
HGRN_SAMPLE_HEADS = 8
HGRN_PROMPT_HEADS = 4
HGRN_SAFE_DECAY = 60.0
SAMPLE_PAGES_PER_STEP = 8
DMA_ISSUE_UNROLL = 8
COUNT_CHAINS = 5
DSA_KEY_STEP = 256
NEG_BIG = -1e30
LOG2E = 1.4426950408889634
VMEM_LIMIT = 56 * 1024 * 1024

f32 = jnp.float32
bf16 = jnp.bfloat16
i32 = jnp.int32
HIGHEST = lax.Precision.HIGHEST
NT_DIMS = (((1,), (1,)), ((), ()))
TN_DIMS = (((0,), (0,)), ((), ()))


def _pick_tile(n, cap, align):
    t = (min(cap, n) // align) * align
    while t > align and n % t:
        t -= align
    assert t > 0 and n % t == 0, (n, cap, align)
    return t


def _params(sem, limit=VMEM_LIMIT):
    return pltpu.CompilerParams(dimension_semantics=sem, vmem_limit_bytes=limit)


def _sigmoid(x):
    return 1.0 / (1.0 + jnp.exp(-x))


def _log2(n):
    s = int(math.log2(n))
    assert 1 << s == n, n
    return s


def _rmsnorm_kernel(x_ref, g_ref, o_ref):
    x = x_ref[...]
    ms = jnp.mean(x * x, axis=-1, keepdims=True)
    o_ref[...] = (x * lax.rsqrt(ms + NORM_EPS) * g_ref[...]).astype(o_ref.dtype)


def _rmsnorm_bf16(x, g):
    n, d = x.shape
    return pl.pallas_call(
        _rmsnorm_kernel,
        grid=(n // ROW_BLOCK,),
        in_specs=[pl.BlockSpec((ROW_BLOCK, d), lambda i: (i, 0)),
                  pl.BlockSpec((1, d), lambda i: (0, 0))],
        out_specs=pl.BlockSpec((ROW_BLOCK, d), lambda i: (i, 0)),
        out_shape=jax.ShapeDtypeStruct((n, d), bf16),
        compiler_params=_params(("parallel",)),
        name="rmsnorm_bf16",
    )(x, g.reshape(1, d))


def _mm_kernel(a_ref, b_ref, o_ref):
    o_ref[...] = jnp.dot(a_ref[...], b_ref[...], preferred_element_type=f32)


def _mm_res_kernel(a_ref, b_ref, r_ref, o_ref):
    o_ref[...] = r_ref[...] + jnp.dot(a_ref[...], b_ref[...], preferred_element_type=f32)


def _matmul(a, b, res=None, name="matmul", col0=0, n=None):
    m, k = a.shape
    n = b.shape[1] if n is None else n
    tm = _pick_tile(m, PROJ_ROW_TILE, ROW_BLOCK)
    tn = _pick_tile(math.gcd(n, col0) if col0 else n, PROJ_COL_TILE, LANE)
    cb = col0 // tn
    in_specs = [pl.BlockSpec((tm, k), lambda j, i: (i, 0)),
                pl.BlockSpec((k, tn), lambda j, i: (0, cb + j))]
    args = [a, b]
    kern = _mm_kernel
    if res is not None:
        in_specs.append(pl.BlockSpec((tm, tn), lambda j, i: (i, j)))
        args.append(res)
        kern = _mm_res_kernel
    return pl.pallas_call(
        kern,
        grid=(n // tn, m // tm),
        in_specs=in_specs,
        out_specs=pl.BlockSpec((tm, tn), lambda j, i: (i, j)),
        out_shape=jax.ShapeDtypeStruct((m, n), f32),
        compiler_params=_params(("parallel", "parallel")),
        name=name,
    )(*args)


def _mm_heads_kernel(a_ref, b_ref, o_ref):
    acc = jnp.dot(a_ref[...], b_ref[...], preferred_element_type=f32)
    for r in range(o_ref.shape[0]):
        for hh in range(o_ref.shape[1]):
            o_ref[r, hh] = acc[r * ROW_BLOCK:(r + 1) * ROW_BLOCK, hh * LANE:(hh + 1) * LANE]


def _matmul_heads(a, b, name, col0, n):
    m, k = a.shape
    tm = _pick_tile(m, PROJ_ROW_TILE, ROW_BLOCK)
    tn = _pick_tile(math.gcd(n, col0) if col0 else n, PROJ_COL_TILE, LANE)
    cb = col0 // tn
    return pl.pallas_call(
        _mm_heads_kernel,
        grid=(n // tn, m // tm),
        in_specs=[pl.BlockSpec((tm, k), lambda j, i: (i, 0)),
                  pl.BlockSpec((k, tn), lambda j, i: (0, cb + j))],
        out_specs=pl.BlockSpec((tm // ROW_BLOCK, tn // LANE, ROW_BLOCK, LANE), lambda j, i: (i, j, 0, 0)),
        out_shape=jax.ShapeDtypeStruct((m // ROW_BLOCK, n // LANE, ROW_BLOCK, LANE), f32),
        compiler_params=_params(("parallel", "parallel")),
        name=name,
    )(a, b)


def _hgrn_static(c):
    levels = _log2(c)
    t = np.arange(c)
    cmat = np.zeros((2 * c, c), np.float32)
    cmat[:c] = t[None, :] <= t[:, None]
    cmat[c:] = 1.0
    lmat = np.zeros((levels * c, c), np.float32)
    lv = np.full((c, c), -1, np.int32)
    for l in range(levels):
        blk = c >> l
        half = blk // 2
        mid = (t // blk) * blk + half
        lmat[l * c:(l + 1) * c] = t[None, :] <= (mid - 1)[:, None]
        same = (t[:, None] // blk) == (t[None, :] // blk)
        up = ((t // half) % 2 == 1)[:, None]
        lo = ((t // half) % 2 == 0)[None, :]
        lv[same & up & lo] = l
    lv[t, t] = levels
    return cmat, lmat, lv, levels


def _split3(x):
    hi = x.astype(bf16)
    r = x - hi.astype(f32)
    mid = r.astype(bf16)
    lo = (r - mid.astype(f32)).astype(bf16)
    return jnp.concatenate([hi, mid, lo], axis=1)


def _sum3(y):
    w = y.shape[1] // 3
    return y[:, :w] + y[:, w:2 * w] + y[:, 2 * w:]


def _hgrn_kernel(*refs, rows, chunk, levels, n_valid, heads, has_init):
    if has_init:
        (zq_ref, zf_ref, zi_ref, zg_ref, lb_ref, gain_ref, cmat_ref, lmat_ref, lv_ref, s0_ref,
         oa_ref, sfin_ref, st_ref, a_ref, q_sc, k_sc, b_sc, p_sc) = refs
    else:
        (zq_ref, zf_ref, zi_ref, zg_ref, lb_ref, gain_ref, cmat_ref, lmat_ref, lv_ref,
         oa_ref, sfin_ref, st_ref, a_ref, q_sc, k_sc, b_sc, p_sc) = refs
        s0_ref = None
    tb = pl.program_id(2)
    c = chunk
    nc = rows // c
    dk = A_EXPAND

    @pl.when(tb == 0)
    def _():
        for hh in range(heads):
            if has_init:
                st_ref[hh] = s0_ref[0, hh].T
            else:
                st_ref[hh] = jnp.zeros(st_ref.shape[1:], f32)

    lv = lv_ref[...]
    cmat = cmat_ref[...]
    row_c = lax.broadcasted_iota(i32, (c, dk), 0)

    pre = []
    unsafe = None
    for hh in range(heads):
        ls = slice(hh * dk, (hh + 1) * dk)
        zq = zq_ref[:, ls]
        lb = lb_ref[:, ls]
        f = lb + (1.0 - lb) * _sigmoid(zf_ref[:, ls])
        logf = jnp.log(f)
        k = 1.0 - f
        q = zq * _sigmoid(zq)
        if n_valid < rows:
            live = lax.broadcasted_iota(i32, (rows, dk), 0) < n_valid
            logf = jnp.where(live, logf, 0.0)
            k = jnp.where(live, k, 0.0)
        parts = _split3(logf)
        bs, bls = [], []
        for ci in range(nc):
            y = _sum3(jnp.dot(cmat, parts[ci * c:(ci + 1) * c], preferred_element_type=f32))
            bs.append(y[:c])
            bls.append(y[c:])
        b = jnp.concatenate(bs, axis=0) if nc > 1 else bs[0]
        bl = jnp.concatenate(bls, axis=0) if nc > 1 else bls[0]
        head_unsafe = jnp.min(bl) < -HGRN_SAFE_DECAY
        unsafe = head_unsafe if unsafe is None else jnp.logical_or(unsafe, head_unsafe)
        pre.append(dict(q=q, k=k, b=b, bl=bl, parts=parts, qh_b=(q * jnp.exp(b)).astype(bf16),
                        kd=(k * jnp.exp(bl - b)).astype(bf16), v_b=zi_ref[:, ls].astype(bf16)))

    @pl.when(jnp.logical_not(unsafe))
    def _():
        for hh, hv in enumerate(pre):
            kt = (hv["k"] * jnp.exp(-hv["b"])).astype(bf16)
            for ci in range(nc):
                rs = slice(ci * c, (ci + 1) * c)
                al = lax.dot_general(hv["qh_b"][rs], kt[rs], NT_DIMS, preferred_element_type=f32)
                a_ref[hh * nc + ci] = jnp.where(lv >= 0, al, 0.0)

    @pl.when(unsafe)
    def _():
        for hh, hv in enumerate(pre):
            hr = slice(hh * rows, (hh + 1) * rows)
            q_sc[hr, :] = hv["q"]
            k_sc[hr, :] = hv["k"]
            b_sc[hr, :] = hv["b"]
            p_sc[hr, :] = hv["parts"]

        def safe_chunk(ci, carry):
            r0 = pl.multiple_of(ci * c, c)
            q_c = q_sc[pl.ds(r0, c), :]
            k_c = k_sc[pl.ds(r0, c), :]
            b_c = b_sc[pl.ds(r0, c), :]
            refs_b = _sum3(jnp.dot(lmat_ref[...], p_sc[pl.ds(r0, c), :], preferred_element_type=f32))
            a = jnp.where(lv == levels, jnp.sum(q_c * k_c, axis=-1, keepdims=True), 0.0)
            for l in range(levels):
                e = jnp.exp(-jnp.abs(b_c - refs_b[l * c:(l + 1) * c]))
                upper = ((row_c >> (levels - l - 1)) & 1) == 1
                qt = jnp.where(upper, q_c * e, 0.0).astype(bf16)
                kt = jnp.where(upper, 0.0, k_c * e).astype(bf16)
                al = lax.dot_general(qt, kt, NT_DIMS, preferred_element_type=f32)
                a = jnp.where(lv == l, al, a)
            a_ref[ci] = a
            return carry

        lax.fori_loop(0, heads * nc, safe_chunk, 0)

    for hh, hv in enumerate(pre):
        ls = slice(hh * dk, (hh + 1) * dk)
        gain = gain_ref[:, ls]
        zg = zg_ref[:, ls]
        st = st_ref[hh]
        for ci in range(nc):
            rs = slice(ci * c, (ci + 1) * c)
            o = (jnp.dot(a_ref[hh * nc + ci].astype(bf16), hv["v_b"][rs], preferred_element_type=f32)
                 + lax.dot_general(hv["qh_b"][rs], st.astype(bf16), NT_DIMS, preferred_element_type=f32))
            ms = jnp.mean(o * o, axis=-1, keepdims=True)
            zg_c = zg[rs]
            oa_ref[rs, ls] = o * lax.rsqrt(ms + NORM_EPS) * gain * (zg_c * _sigmoid(zg_c))
            st = (st * jnp.exp(hv["bl"][ci * c:ci * c + 1])
                  + lax.dot_general(hv["v_b"][rs], hv["kd"][rs], TN_DIMS, preferred_element_type=f32))
        st_ref[hh] = st

    @pl.when(tb == pl.num_programs(2) - 1)
    def _():
        for hh in range(heads):
            sfin_ref[0, hh] = st_ref[hh].T


def _hgrn(z_a, lb, gain, n_seq, seq_rows, row_block, chunk, n_valid, heads, s0=None):
    kd = A_HEADS * A_EXPAND
    dv = LANE
    assert z_a.shape[1] == 2 * kd + 2 * A_HEADS * dv and A_EXPAND == LANE and A_HEADS % heads == 0
    cmat, lmat, lv, levels = _hgrn_static(chunk)
    nrb = seq_rows // row_block
    hb = A_HEADS // heads
    wd = heads * LANE

    def zspec(seg):
        return pl.BlockSpec((row_block, wd), lambda b, h, t, seg=seg: (b * nrb + t, seg * hb + h))

    const = lambda b, h, t: (0, 0)
    in_specs = [zspec(0), zspec(1), zspec(2), zspec(3),
                pl.BlockSpec((1, wd), lambda b, h, t: (0, h)),
                pl.BlockSpec((1, wd), lambda b, h, t: (0, h)),
                pl.BlockSpec(cmat.shape, const), pl.BlockSpec(lmat.shape, const), pl.BlockSpec(lv.shape, const)]
    args = [z_a, z_a, z_a, z_a, lb.reshape(1, kd), gain.reshape(1, -1),
            jnp.asarray(cmat, bf16), jnp.asarray(lmat, bf16), jnp.asarray(lv)]
    if s0 is not None:
        in_specs.append(pl.BlockSpec((1, heads, A_EXPAND, dv), lambda b, h, t: (b, h, 0, 0)))
        args.append(s0)
    kern = functools.partial(_hgrn_kernel, rows=row_block, chunk=chunk, levels=levels, n_valid=n_valid,
                             heads=heads, has_init=s0 is not None)
    return pl.pallas_call(
        kern,
        grid=(n_seq, hb, nrb),
        in_specs=in_specs,
        out_specs=[pl.BlockSpec((row_block, wd), lambda b, h, t: (b * nrb + t, h)),
                   pl.BlockSpec((1, heads, A_EXPAND, dv), lambda b, h, t: (b, h, 0, 0))],
        out_shape=[jax.ShapeDtypeStruct((n_seq * seq_rows, A_HEADS * dv), f32),
                   jax.ShapeDtypeStruct((n_seq, A_HEADS, A_EXPAND, dv), f32)],
        scratch_shapes=[pltpu.VMEM((heads, dv, A_EXPAND), f32),
                        pltpu.VMEM((heads * row_block // chunk, chunk, chunk), f32),
                        pltpu.VMEM((heads * row_block, LANE), f32), pltpu.VMEM((heads * row_block, LANE), f32),
                        pltpu.VMEM((heads * row_block, LANE), f32),
                        pltpu.VMEM((heads * row_block, 3 * LANE), bf16)],
        compiler_params=_params(("parallel", "parallel", "arbitrary")),
        name="hgrn2_prompt" if s0 is None else "hgrn2_sample",
    )(*args)


def _sort_key(x):
    b = pltpu.bitcast(x + 0.0, i32)
    return b ^ ((b >> 31) & 0x7FFFFFFF)


def _kth_largest_key(key_ref, k, axis):
    shape = (1, key_ref.shape[1]) if axis == 0 else (key_ref.shape[0], 1)
    int_min = jnp.int32(-2 ** 31)

    n = key_ref.shape[axis]
    parts = max(p for p in range(1, COUNT_CHAINS + 1) if (n // LANE) % p == 0)
    seg = n // parts

    def step(i, lo):
        cand = lo ^ lax.shift_left(jnp.int32(1), 31 - i)
        cnt = None
        for pi in range(parts):
            sl = slice(pi * seg, (pi + 1) * seg)
            keys = key_ref[sl, :] if axis == 0 else key_ref[:, sl]
            c = jnp.sum((keys >= cand).astype(i32), axis=axis, keepdims=True)
            cnt = c if cnt is None else cnt + c
        return jnp.where(cnt >= k, cand, lo)

    return lax.fori_loop(0, 32, step, jnp.full(shape, int_min, i32))


def _dsa_prompt_kernel(qi_ref, smq_ref, smk_ref, q_ref, k_ref, v_ref, o_ref, mask_ref, key_ref, qs_ref,
                       *, kext, topk, qblk, group, qb0):
    qb = qb0 + pl.program_id(1)
    g = pl.program_id(2)
    n_tiles = kext // qblk

    def causal(r0):
        kpos = r0 + lax.broadcasted_iota(i32, (qblk, qblk), 0)
        qpos = qb * qblk + lax.broadcasted_iota(i32, (qblk, qblk), 1)
        return kpos <= qpos

    @pl.when(g == 0)
    def _():
        qs_ref[...] = qi_ref[0].reshape(IDX_HEADS * qblk, IDX_DIM).astype(bf16)
        w_t = smq_ref[...].T * (IDX_HEADS ** -0.5 * IDX_DIM ** -0.5)

        def score_tile(ti, carry):
            r0 = pl.multiple_of(ti * qblk, qblk)
            kid = smk_ref[pl.ds(r0, qblk), 0:IDX_DIM].astype(bf16)
            s_all = lax.dot_general(kid, qs_ref[...], NT_DIMS, preferred_element_type=f32)
            acc = jnp.zeros((qblk, qblk), f32)
            for h in range(IDX_HEADS):
                acc = acc + (jnp.maximum(s_all[:, h * qblk:(h + 1) * qblk], 0.0)
                             * w_t[IDX_DIM + h:IDX_DIM + h + 1, :])
            key_ref[pl.ds(r0, qblk), :] = _sort_key(jnp.where(causal(r0), acc, -jnp.inf))
            return carry

        lax.fori_loop(0, n_tiles, score_tile, 0)
        thr = _kth_largest_key(key_ref, topk, axis=0)
        keys = key_ref[...]
        kpos = lax.broadcasted_iota(i32, (kext, qblk), 0)
        qpos = qb * qblk + lax.broadcasted_iota(i32, (kext, qblk), 1)
        mask_ref[...] = jnp.where(jnp.logical_and(kpos <= qpos, keys >= thr), 0.0, -jnp.inf)
        need = topk - jnp.sum((keys > thr).astype(i32), axis=0, keepdims=True)
        n_tie = jnp.sum((keys == thr).astype(i32), axis=0, keepdims=True)

        @pl.when(jnp.max(n_tie - need) > 0)
        def _():
            strict = (lax.broadcasted_iota(i32, (qblk, qblk), 0) > lax.broadcasted_iota(i32, (qblk, qblk), 1))
            strict = jnp.where(strict, 1.0, 0.0).astype(bf16)
            need_f = need.astype(f32)

            def tile(ti, seen):
                r0 = pl.multiple_of(ti * qblk, qblk)
                kt = key_ref[pl.ds(r0, qblk), :]
                tie = kt == thr
                tie_f = jnp.where(tie, 1.0, 0.0)
                before = seen + jnp.dot(strict, tie_f.astype(bf16), preferred_element_type=f32)
                keep = jnp.logical_or(kt > thr, jnp.logical_and(tie, before < need_f))
                keep = jnp.logical_and(keep, causal(r0))
                mask_ref[pl.ds(r0, qblk), :] = jnp.where(keep, 0.0, -jnp.inf)
                return seen + jnp.sum(tie_f, axis=0, keepdims=True)

            lax.fori_loop(0, n_tiles, tile, jnp.zeros((1, qblk), f32))

    kk = k_ref[0:kext, :].astype(bf16)
    vt = v_ref[0:kext, :].T.astype(bf16)
    dh = kk.shape[1]
    for hq in range(group):
        qh = (q_ref[:, hq * dh:(hq + 1) * dh] * (dh ** -0.5 * LOG2E)).astype(bf16)
        s = lax.dot_general(kk, qh, NT_DIMS, preferred_element_type=f32) + mask_ref[...]
        m = jnp.max(s, axis=0, keepdims=True)
        p = jnp.exp2(s - m)
        den = jnp.sum(p, axis=0, keepdims=True)
        o_t = jnp.dot(vt, p.astype(bf16), preferred_element_type=f32) / den
        o_ref[:, hq * dh:(hq + 1) * dh] = o_t.T


def _dsa_prompt(bqi, small, bq, bkv, n_batch, seq):
    dh = bq.shape[1] // B_HEADS
    group = B_HEADS // B_KV_HEADS
    qblk = LANE
    nqb = seq // qblk
    topk = min(TOPK_MAX, seq // 4)
    parts = []
    qb0 = 0
    count = max(DSA_KEY_STEP // qblk, 1)
    assert nqb % count == 0
    while qb0 < nqb:
        kext = (qb0 + count) * qblk
        kern = functools.partial(_dsa_prompt_kernel, kext=kext, topk=topk, qblk=qblk, group=group, qb0=qb0)
        part = pl.pallas_call(
            kern,
            grid=(n_batch, count, B_KV_HEADS),
            in_specs=[pl.BlockSpec((1, IDX_HEADS, qblk, IDX_DIM), lambda b, j, g, q0=qb0: (b * nqb + q0 + j, 0, 0, 0)),
                      pl.BlockSpec((qblk, small.shape[1]), lambda b, j, g, q0=qb0: (b * nqb + q0 + j, 0)),
                      pl.BlockSpec((seq, small.shape[1]), lambda b, j, g: (b, 0)),
                      pl.BlockSpec((qblk, group * dh), lambda b, j, g, q0=qb0: (b * nqb + q0 + j, g)),
                      pl.BlockSpec((seq, dh), lambda b, j, g: (b, g)),
                      pl.BlockSpec((seq, dh), lambda b, j, g: (b, B_KV_HEADS + g))],
            out_specs=pl.BlockSpec((qblk, group * dh), lambda b, j, g, cnt=count: (b * cnt + j, g)),
            out_shape=jax.ShapeDtypeStruct((n_batch * count * qblk, B_HEADS * dh), f32),
            scratch_shapes=[pltpu.VMEM((kext, qblk), f32), pltpu.VMEM((kext, qblk), i32),
                            pltpu.VMEM((IDX_HEADS * qblk, IDX_DIM), bf16)],
            compiler_params=_params(("parallel", "parallel", "arbitrary")),
            name=f"dsa_prompt_k{kext}",
        )(bqi, small, small, bq, bkv, bkv)
        parts.append(part.reshape(n_batch, count * qblk, B_HEADS * dh))
        qb0 += count
    return jnp.concatenate(parts, axis=1).reshape(n_batch * seq, B_HEADS * dh)


def _dsa_sample_select_kernel(pt_ref, qi_ref, wsel_ref, knew_ref, cik_hbm, mask_ref,
                              slab_ref, sc_ref, key_ref, sel_ref, sem, *, n_pages, page, topk, n_tok, kvh):
    b = pl.program_id(0)
    n_seq = pl.num_programs(0)
    past = n_pages * page

    def page_copy(seq, slot, j):
        return pltpu.make_async_copy(cik_hbm.at[0, pt_ref[seq, j]],
                                     slab_ref.at[slot, pl.ds(pl.multiple_of(j * page, page), page), :],
                                     sem.at[slot])

    def start(seq, slot):
        lax.fori_loop(0, n_pages, lambda j, c: (page_copy(seq, slot, j).start(), c)[1], 0)

    def wait(seq, slot):
        lax.fori_loop(0, n_pages, lambda j, c: (page_copy(seq, slot, j).wait(), c)[1], 0)

    @pl.when(b == 0)
    def _():
        start(0, 0)

    slot = b % 2

    @pl.when(b + 1 < n_seq)
    def _():
        start(b + 1, 1 - slot)

    wait(b, slot)
    qi = qi_ref[0].astype(bf16)
    wsel = wsel_ref[0]

    def tile_scores(keys):
        s = lax.dot_general(qi, keys.astype(bf16), NT_DIMS, preferred_element_type=f32)
        return jnp.dot(wsel, jnp.maximum(s, 0.0), precision=HIGHEST, preferred_element_type=f32)

    sc_ref[:, 0:past] = tile_scores(slab_ref[slot])
    new = tile_scores(knew_ref[0])
    r = lax.broadcasted_iota(i32, new.shape, 0)
    cidx = lax.broadcasted_iota(i32, new.shape, 1)
    sc_ref[:, past:past + LANE] = jnp.where(jnp.logical_and(cidx <= r, cidx < n_tok), new, -jnp.inf)
    sc = sc_ref[...]
    key_ref[...] = _sort_key(sc)
    thr = _kth_largest_key(key_ref, topk, axis=1)
    keys = key_ref[...]
    sel = jnp.logical_and(keys >= thr, sc > -jnp.inf)
    sel_ref[...] = jnp.where(sel, 1.0, 0.0)
    need = topk - jnp.sum((keys > thr).astype(i32), axis=1, keepdims=True)
    n_tie = jnp.sum((keys == thr).astype(i32), axis=1, keepdims=True)

    @pl.when(jnp.max(n_tie - need) > 0)
    def _():
        strict = (lax.broadcasted_iota(i32, (LANE, LANE), 0) < lax.broadcasted_iota(i32, (LANE, LANE), 1))
        strict = jnp.where(strict, 1.0, 0.0).astype(bf16)
        need_f = need.astype(f32)

        def tile(ti, seen):
            c1 = pl.multiple_of(ti * LANE, LANE)
            kt = key_ref[:, pl.ds(c1, LANE)]
            tie = kt == thr
            tie_f = jnp.where(tie, 1.0, 0.0)
            before = seen + jnp.dot(tie_f.astype(bf16), strict, preferred_element_type=f32)
            keep = jnp.logical_or(kt > thr, jnp.logical_and(tie, before < need_f))
            keep = jnp.logical_and(keep, sc_ref[:, pl.ds(c1, LANE)] > -jnp.inf)
            sel_ref[:, pl.ds(c1, LANE)] = jnp.where(keep, 1.0, 0.0)
            return seen + jnp.sum(tie_f, axis=1, keepdims=True)

        lax.fori_loop(0, sc_ref.shape[1] // LANE, tile, jnp.zeros((8, 1), f32))

    lane = lax.broadcasted_iota(i32, (8, LANE), 1)

    def expand(ti, carry):
        c1 = pl.multiple_of(ti * LANE, LANE)
        hit = sel_ref[:, pl.ds(c1, LANE)]
        for j in range(kvh):
            src = j * (LANE // kvh) + (lane >> _log2(kvh))
            piece = jnp.take_along_axis(hit, src, axis=1)
            c2 = pl.multiple_of(ti * (LANE * kvh) + j * LANE, LANE)
            mask_ref[0, :, pl.ds(c2, LANE)] = jnp.where(piece > 0.5, 0.0, NEG_BIG)
        return carry

    lax.fori_loop(0, sc_ref.shape[1] // LANE, expand, 0)


def _dsa_sample_attn_kernel(pt_ref, q_ref, mask_ref, masknew_ref, knew_ref, vnew_ref, gm_ref, ck_hbm, cv_hbm,
                            o_ref, kslab, vslab, m_ref, l_ref, acc_ref, sem,
                            *, n_chunks, ch, n_tok, kvh, group, dh, page):
    b = pl.program_id(0)
    c = pl.program_id(1)
    t = b * n_chunks + c
    total = pl.num_programs(0) * n_chunks
    scale = dh ** -0.5

    def copies(seq, chunk, slot, j):
        pid = pt_ref[seq, chunk * ch + j]
        return (pltpu.make_async_copy(ck_hbm.at[0, pid], kslab.at[slot, j], sem.at[slot, 0]),
                pltpu.make_async_copy(cv_hbm.at[0, pid], vslab.at[slot, j], sem.at[slot, 1]))

    def start(seq, chunk, slot):
        for j in range(ch):
            ck, cv = copies(seq, chunk, slot, j)
            ck.start()
            cv.start()

    def wait(seq, chunk, slot):
        for j in range(ch):
            ck, cv = copies(seq, chunk, slot, j)
            ck.wait()
            cv.wait()

    @pl.when(t == 0)
    def _():
        start(0, 0, 0)

    slot = t % 2

    @pl.when(t + 1 < total)
    def _():
        nxt = t + 1
        start(nxt // n_chunks, nxt % n_chunks, 1 - slot)

    wait(b, c, slot)

    @pl.when(c == 0)
    def _():
        m_ref[...] = jnp.full_like(m_ref, NEG_BIG)
        l_ref[...] = jnp.zeros_like(l_ref)
        acc_ref[...] = jnp.zeros_like(acc_ref)

    qb = (q_ref[0] * scale).astype(bf16)
    heads = kvh * group

    def attend(keys, vals, mask8):
        n_cols = keys.shape[0]
        tok_mask = jnp.concatenate([jnp.broadcast_to(mask8[tk:tk + 1], (heads, n_cols))
                                    for tk in range(n_tok)], axis=0)
        madd = tok_mask + gm_ref[:, 0:n_cols]
        s = lax.dot_general(qb, keys, NT_DIMS, preferred_element_type=f32) + madd
        m_old = m_ref[...]
        m_new = jnp.maximum(m_old, jnp.max(s, axis=-1, keepdims=True))
        pr = jnp.where(madd == 0.0, jnp.exp(s - m_new), 0.0)
        alpha = jnp.exp(m_old - m_new)
        l_ref[...] = alpha * l_ref[...] + jnp.sum(pr, axis=-1, keepdims=True)
        acc_ref[...] = alpha * acc_ref[...] + jnp.dot(pr.astype(bf16), vals, preferred_element_type=f32)
        m_ref[...] = m_new

    attend(kslab[slot].reshape(ch * page * kvh, dh).astype(bf16),
           vslab[slot].reshape(ch * page * kvh, dh).astype(bf16), mask_ref[0])

    @pl.when(c == n_chunks - 1)
    def _():
        attend(knew_ref[0].astype(bf16), vnew_ref[0].astype(bf16), masknew_ref[0])
        o_ref[0] = acc_ref[...] / l_ref[...]


def _dsa_sample(qi, wsel, q, knew_idx, k_new, v_new, cache_k, cache_v, cache_idx_k, page_table, n_tok):
    n_seq, n_pages = page_table.shape
    page, kvh, dh = cache_k.shape[2], cache_k.shape[3], cache_k.shape[4]
    assert page == LANE and LANE % kvh == 0
    rows_i = qi.shape[1]
    rows = q.shape[1]
    heads = rows // n_tok
    group = heads // kvh
    past = n_pages * page
    kp = past + LANE
    topk = min(TOPK_MAX, (past + n_tok) // 4)
    ch = math.gcd(SAMPLE_PAGES_PER_STEP, n_pages)
    n_chunks = n_pages // ch
    cw = ch * page * kvh
    own = (np.arange(rows)[:, None] % heads) // group == (np.arange(cw)[None, :] % kvh)
    gm = np.where(own, 0.0, NEG_BIG).astype(np.float32)

    mask = pl.pallas_call(
        functools.partial(_dsa_sample_select_kernel, n_pages=n_pages, page=page, topk=topk, n_tok=n_tok,
                          kvh=kvh),
        grid_spec=pltpu.PrefetchScalarGridSpec(
            num_scalar_prefetch=1,
            grid=(n_seq,),
            in_specs=[pl.BlockSpec((1, rows_i, IDX_DIM), lambda b, pt: (b, 0, 0)),
                      pl.BlockSpec((1, 8, rows_i), lambda b, pt: (b, 0, 0)),
                      pl.BlockSpec((1, LANE, IDX_DIM), lambda b, pt: (b, 0, 0)),
                      pl.BlockSpec(memory_space=pl.ANY)],
            out_specs=pl.BlockSpec((1, 8, kp * kvh), lambda b, pt: (b, 0, 0)),
            scratch_shapes=[pltpu.VMEM((2, past, IDX_DIM), f32), pltpu.VMEM((8, kp), f32),
                            pltpu.VMEM((8, kp), i32), pltpu.VMEM((8, kp), f32), pltpu.SemaphoreType.DMA((2,))]),
        out_shape=jax.ShapeDtypeStruct((n_seq, 8, kp * kvh), f32),
        compiler_params=_params(("arbitrary",)),
        name="dsa_sample_select",
    )(page_table, qi, wsel, knew_idx, cache_idx_k)

    return pl.pallas_call(
        functools.partial(_dsa_sample_attn_kernel, n_chunks=n_chunks, ch=ch, n_tok=n_tok, kvh=kvh, group=group,
                          dh=dh, page=page),
        grid_spec=pltpu.PrefetchScalarGridSpec(
            num_scalar_prefetch=1,
            grid=(n_seq, n_chunks),
            in_specs=[pl.BlockSpec((1, rows, dh), lambda b, c, pt: (b, 0, 0)),
                      pl.BlockSpec((1, 8, cw), lambda b, c, pt: (b, 0, c)),
                      pl.BlockSpec((1, 8, LANE * kvh), lambda b, c, pt: (b, 0, past // LANE)),
                      pl.BlockSpec((1, LANE * kvh, dh), lambda b, c, pt: (b, 0, 0)),
                      pl.BlockSpec((1, LANE * kvh, dh), lambda b, c, pt: (b, 0, 0)),
                      pl.BlockSpec(gm.shape, lambda b, c, pt: (0, 0)),
                      pl.BlockSpec(memory_space=pl.ANY),
                      pl.BlockSpec(memory_space=pl.ANY)],
            out_specs=pl.BlockSpec((1, rows, dh), lambda b, c, pt: (b, 0, 0)),
            scratch_shapes=[pltpu.VMEM((2, ch, page, kvh, dh), f32), pltpu.VMEM((2, ch, page, kvh, dh), f32),
                            pltpu.VMEM((rows, 1), f32), pltpu.VMEM((rows, 1), f32),
                            pltpu.VMEM((rows, dh), f32), pltpu.SemaphoreType.DMA((2, 2))]),
        out_shape=jax.ShapeDtypeStruct((n_seq, rows, dh), f32),
        compiler_params=_params(("arbitrary", "arbitrary")),
        name="dsa_sample_attn",
    )(page_table, q, mask, mask, k_new, v_new, jnp.asarray(gm), cache_k, cache_v)


def _merge_kernel(ga_ref, gb_ref, oa_ref, ob_ref, o_ref):
    o_ref[...] = (_sigmoid(ga_ref[...]) * oa_ref[...] + _sigmoid(gb_ref[...]) * ob_ref[...]).astype(o_ref.dtype)


def _merge(gates, oa, ob):
    n, d = oa.shape
    tn = _pick_tile(d, 1024, 128)
    nj = d // tn
    return pl.pallas_call(
        _merge_kernel,
        grid=(n // ROW_BLOCK, nj),
        in_specs=[pl.BlockSpec((ROW_BLOCK, tn), lambda i, j: (i, j)),
                  pl.BlockSpec((ROW_BLOCK, tn), lambda i, j: (i, nj + j)),
                  pl.BlockSpec((ROW_BLOCK, tn), lambda i, j: (i, j)),
                  pl.BlockSpec((ROW_BLOCK, tn), lambda i, j: (i, j))],
        out_specs=pl.BlockSpec((ROW_BLOCK, tn), lambda i, j: (i, j)),
        out_shape=jax.ShapeDtypeStruct((n, d), bf16),
        compiler_params=_params(("parallel", "parallel")),
        name="merge_branches",
    )(gates, gates, oa, ob)


def _router_kernel(x_ref, g_ref, w_ref, b_ref, route_ref, cnt_ref, carry_ref):
    i = pl.program_id(0)

    @pl.when(i == 0)
    def _():
        carry_ref[...] = jnp.zeros_like(carry_ref)

    x = x_ref[...]
    ms = jnp.mean(x * x, axis=-1, keepdims=True)
    h = x * lax.rsqrt(ms + NORM_EPS) * g_ref[...]
    logits = jnp.dot(h, w_ref[...], precision=HIGHEST, preferred_element_type=f32) + b_ref[...]
    lane = lax.broadcasted_iota(i32, logits.shape, 1)
    big = jnp.int32(1 << 20)
    ng, epg = N_GROUPS, EXPERTS_PER_GROUP

    def first_max(vals):
        mx = jnp.max(vals, axis=-1, keepdims=True)
        idx = jnp.min(jnp.where(vals == mx, lane, big), axis=-1, keepdims=True)
        return mx, idx

    lg = jnp.where(lane < ng, logits, -jnp.inf)
    mg, g_sel = first_max(lg)
    p_g = 1.0 / jnp.sum(jnp.exp(lg - mg), axis=-1, keepdims=True)
    lo = ng + g_sel * epg
    le = jnp.where(jnp.logical_and(lane >= lo, lane < lo + epg), logits, -jnp.inf)
    m1, i1 = first_max(le)
    se = jnp.sum(jnp.exp(le - m1), axis=-1, keepdims=True)
    m2, i2 = first_max(jnp.where(lane == i1, -jnp.inf, le))
    p1 = 1.0 / se
    p2 = jnp.exp(m2 - m1) / se
    w1 = p_g * p1 / (p1 + p2)
    w2 = p_g * p2 / (p1 + p2)

    hot = jnp.logical_or(lane == i1, lane == i2)
    hot_b = jnp.where(hot, 1.0, 0.0).astype(bf16)
    n = x.shape[0]
    strict = (lax.broadcasted_iota(i32, (n, n), 0) > lax.broadcasted_iota(i32, (n, n), 1))
    before = jnp.dot(jnp.where(strict, 1.0, 0.0).astype(bf16), hot_b, preferred_element_type=f32)
    before = before + carry_ref[...]
    r1 = jnp.sum(jnp.where(lane == i1, before, 0.0), axis=-1, keepdims=True)
    r2 = jnp.sum(jnp.where(lane == i2, before, 0.0), axis=-1, keepdims=True)
    carry_ref[...] = carry_ref[...] + jnp.sum(jnp.where(hot, 1.0, 0.0), axis=0, keepdims=True)

    out = jnp.where(lane == 0, (i1 - ng).astype(f32), 0.0)
    out = jnp.where(lane == 1, (i2 - ng).astype(f32), out)
    out = jnp.where(lane == 2, w1, out)
    out = jnp.where(lane == 3, w2, out)
    out = jnp.where(lane == 4, r1, out)
    out = jnp.where(lane == 5, r2, out)
    route_ref[...] = out
    cnt_ref[...] = carry_ref[...]


def _router(x1, g, w_r, b_r):
    n, d = x1.shape
    return pl.pallas_call(
        _router_kernel,
        grid=(n // ROW_BLOCK,),
        in_specs=[pl.BlockSpec((ROW_BLOCK, d), lambda i: (i, 0)),
                  pl.BlockSpec((1, d), lambda i: (0, 0)),
                  pl.BlockSpec((d, LANE), lambda i: (0, 0)),
                  pl.BlockSpec((1, LANE), lambda i: (0, 0))],
        out_specs=[pl.BlockSpec((ROW_BLOCK, LANE), lambda i: (i, 0)),
                   pl.BlockSpec((1, LANE), lambda i: (0, 0))],
        out_shape=[jax.ShapeDtypeStruct((n, LANE), f32), jax.ShapeDtypeStruct((1, LANE), f32)],
        scratch_shapes=[pltpu.VMEM((1, LANE), f32)],
        compiler_params=_params(("arbitrary",)),
        name="moe_router",
    )(x1, g.reshape(1, d), w_r, b_r)


def _row_copy(src_hbm, dst_vmem, sem, src_row, dst_row):
    return pltpu.make_async_copy(src_hbm.at[pl.ds(src_row, 1), :], dst_vmem.at[pl.ds(dst_row, 1), :], sem)


def _dispatch_kernel(tok_ref, x_hbm, g_ref, o_ref, buf_ref, sem):
    i = pl.program_id(0)
    n = pl.num_programs(0)

    def for_real_rows(blk, fn):
        def body(r8, c):
            for u in range(DMA_ISSUE_UNROLL):
                r = r8 * DMA_ISSUE_UNROLL + u
                tok = tok_ref[blk * MOE_BLOCK + r]

                @pl.when(tok >= 0)
                def _(tok=tok, r=r, u=u):
                    fn(tok, r, u)
            return c
        lax.fori_loop(0, MOE_BLOCK // DMA_ISSUE_UNROLL, body, 0)

    def start(blk, slot):
        for_real_rows(blk, lambda tok, r, u: _row_copy(x_hbm, buf_ref.at[slot], sem.at[slot], tok, r).start(
            priority=u % 2))

    def wait(blk, slot):
        for_real_rows(blk, lambda tok, r, u: _row_copy(x_hbm, buf_ref.at[slot], sem.at[slot], tok, r).wait())

    @pl.when(i == 0)
    def _():
        buf_ref[...] = jnp.zeros_like(buf_ref)
        start(0, 0)

    slot = i % 2

    @pl.when(i + 1 < n)
    def _():
        start(i + 1, 1 - slot)

    wait(i, slot)
    x = buf_ref[slot]
    ms = jnp.mean(x * x, axis=-1, keepdims=True)
    o_ref[...] = (x * lax.rsqrt(ms + NORM_EPS) * g_ref[...]).astype(o_ref.dtype)


def _dispatch(row_tok, x1, g):
    n, d = x1.shape
    rows = row_tok.shape[0]
    return pl.pallas_call(
        _dispatch_kernel,
        grid_spec=pltpu.PrefetchScalarGridSpec(
            num_scalar_prefetch=1,
            grid=(rows // MOE_BLOCK,),
            in_specs=[pl.BlockSpec(memory_space=pl.ANY),
                      pl.BlockSpec((1, d), lambda i, tok: (0, 0))],
            out_specs=pl.BlockSpec((MOE_BLOCK, d), lambda i, tok: (i, 0)),
            scratch_shapes=[pltpu.VMEM((2, MOE_BLOCK, d), f32), pltpu.SemaphoreType.DMA((2,))]),
        out_shape=jax.ShapeDtypeStruct((rows, d), bf16),
        compiler_params=_params(("arbitrary",)),
        name="moe_dispatch",
    )(row_tok, x1, g.reshape(1, d))


def _expert_ffn_kernel(ie_ref, ir_ref, in_ref, nit_ref, xs_hbm, wg_ref, wu_ref, wd_ref, y_hbm,
                       x_ref, acc_ref, sem_in, sem_out, *, n_ff):
    it = pl.program_id(0)
    ff = pl.program_id(1)
    nblk = in_ref[it]
    row0 = pl.multiple_of(ir_ref[it] * MOE_BLOCK, MOE_BLOCK)
    span = MOE_ITEM_BLOCKS * MOE_BLOCK

    @pl.when(nblk > 0)
    def _():
        @pl.when(ff == 0)
        def _():
            cp = pltpu.make_async_copy(xs_hbm.at[pl.ds(row0, span), :], x_ref, sem_in)
            cp.start()
            acc_ref[...] = jnp.zeros_like(acc_ref)
            cp.wait()

        x = x_ref[...]
        gate = jnp.dot(x, wg_ref[0].astype(bf16), preferred_element_type=f32)
        up = jnp.dot(x, wu_ref[0].astype(bf16), preferred_element_type=f32)
        act = (gate * _sigmoid(gate) * up).astype(bf16)
        acc_ref[...] += jnp.dot(act, wd_ref[0].astype(bf16), preferred_element_type=f32)

        @pl.when(ff == n_ff - 1)
        def _():
            for sb in range(MOE_ITEM_BLOCKS):
                @pl.when(sb < nblk)
                def _(sb=sb):
                    rs = pl.ds(sb * MOE_BLOCK, MOE_BLOCK)
                    cp = pltpu.make_async_copy(acc_ref.at[rs, :],
                                               y_hbm.at[pl.ds(row0 + sb * MOE_BLOCK, MOE_BLOCK), :], sem_out)
                    cp.start()
                    cp.wait()

    @pl.when(jnp.logical_and(it == pl.num_programs(0) - 1, ff == n_ff - 1))
    def _():
        acc_ref[0:MOE_BLOCK, :] = jnp.zeros((MOE_BLOCK, acc_ref.shape[1]), f32)

        def fill(blk, c):
            cp = pltpu.make_async_copy(acc_ref.at[pl.ds(0, MOE_BLOCK), :],
                                       y_hbm.at[pl.ds(pl.multiple_of(blk * MOE_BLOCK, MOE_BLOCK), MOE_BLOCK), :],
                                       sem_out)
            cp.start()
            cp.wait()
            return c

        lax.fori_loop(nit_ref[1], y_hbm.shape[0] // MOE_BLOCK, fill, 0)


def _expert_ffn(item_e, item_r, item_n, n_items, xs, w_gate, w_up, w_down, max_items):
    rows, d = xs.shape
    dff = w_gate.shape[3]
    tf = _pick_tile(dff, MOE_FF_TILE, LANE)
    n_ff = dff // tf
    span = MOE_ITEM_BLOCKS * MOE_BLOCK

    def f_eff(i, f, nit):
        return jnp.where(i < nit[0], f, n_ff - 1)

    return pl.pallas_call(
        functools.partial(_expert_ffn_kernel, n_ff=n_ff),
        grid_spec=pltpu.PrefetchScalarGridSpec(
            num_scalar_prefetch=4,
            grid=(max_items, n_ff),
            in_specs=[pl.BlockSpec(memory_space=pl.ANY),
                      pl.BlockSpec((None, 1, d, tf), lambda i, f, ie, ir, inn, nit: (0, ie[i], 0, f_eff(i, f, nit))),
                      pl.BlockSpec((None, 1, d, tf), lambda i, f, ie, ir, inn, nit: (0, ie[i], 0, f_eff(i, f, nit))),
                      pl.BlockSpec((None, 1, tf, d), lambda i, f, ie, ir, inn, nit: (0, ie[i], f_eff(i, f, nit), 0))],
            out_specs=pl.BlockSpec(memory_space=pl.ANY),
            scratch_shapes=[pltpu.VMEM((span, d), bf16), pltpu.VMEM((span, d), f32),
                            pltpu.SemaphoreType.DMA(()), pltpu.SemaphoreType.DMA(())]),
        out_shape=jax.ShapeDtypeStruct((rows, d), f32),
        compiler_params=_params(("arbitrary", "arbitrary")),
        name="moe_expert_ffn",
    )(item_e, item_r, item_n, n_items, xs, w_gate, w_up, w_down)


def _combine_kernel(d1_ref, d2_ref, y_hbm, x_ref, route_ref, g_ref, o_ref, buf_ref, sem):
    i = pl.program_id(0)
    n = pl.num_programs(0)

    def start(blk, slot):
        def body(r8, c):
            for u in range(DMA_ISSUE_UNROLL):
                r = r8 * DMA_ISSUE_UNROLL + u
                _row_copy(y_hbm, buf_ref.at[slot, 0], sem.at[slot], d1_ref[blk * ROW_BLOCK + r], r).start(priority=0)
                _row_copy(y_hbm, buf_ref.at[slot, 1], sem.at[slot], d2_ref[blk * ROW_BLOCK + r], r).start(priority=1)
            return c
        lax.fori_loop(0, ROW_BLOCK // DMA_ISSUE_UNROLL, body, 0)

    def wait(slot):
        for r in range(ROW_BLOCK):
            _row_copy(y_hbm, buf_ref.at[slot, 0], sem.at[slot], 0, r).wait()
            _row_copy(y_hbm, buf_ref.at[slot, 1], sem.at[slot], 0, r).wait()

    @pl.when(i == 0)
    def _():
        start(0, 0)

    slot = i % 2

    @pl.when(i + 1 < n)
    def _():
        start(i + 1, 1 - slot)

    wait(slot)
    route = route_ref[...]
    x = x_ref[...] + route[:, 2:3] * buf_ref[slot, 0] + route[:, 3:4] * buf_ref[slot, 1]
    ms = jnp.mean(x * x, axis=-1, keepdims=True)
    o_ref[...] = x * lax.rsqrt(ms + NORM_EPS) * g_ref[...]


def _combine(d1, d2, yb, x1, route, g):
    n, d = x1.shape
    return pl.pallas_call(
        _combine_kernel,
        grid_spec=pltpu.PrefetchScalarGridSpec(
            num_scalar_prefetch=2,
            grid=(n // ROW_BLOCK,),
            in_specs=[pl.BlockSpec(memory_space=pl.ANY),
                      pl.BlockSpec((ROW_BLOCK, d), lambda i, a, b: (i, 0)),
                      pl.BlockSpec((ROW_BLOCK, LANE), lambda i, a, b: (i, 0)),
                      pl.BlockSpec((1, d), lambda i, a, b: (0, 0))],
            out_specs=pl.BlockSpec((ROW_BLOCK, d), lambda i, a, b: (i, 0)),
            scratch_shapes=[pltpu.VMEM((2, 2, ROW_BLOCK, d), f32), pltpu.SemaphoreType.DMA((2,))]),
        out_shape=jax.ShapeDtypeStruct((n, d), f32),
        compiler_params=_params(("arbitrary",)),
        name="moe_combine",
    )(d1, d2, yb, x1, route, g.reshape(1, d))


def _moe(x1, norm_ffn, w_rg, b_rg, w_re, b_re, w_gate, w_up, w_down, norm_final):
    n, d = x1.shape
    n_exp = w_gate.shape[1]
    ng = w_rg.shape[1]
    w_r = jnp.zeros((d, LANE), f32).at[:, :ng].set(w_rg).at[:, ng:ng + n_exp].set(w_re)
    b_r = jnp.zeros((1, LANE), f32).at[0, :ng].set(b_rg).at[0, ng:ng + n_exp].set(b_re)
    route, cnt = _router(x1, norm_ffn, w_r, b_r)

    e1 = route[:, 0].astype(i32)
    e2 = route[:, 1].astype(i32)
    counts = cnt[0, ng:ng + n_exp].astype(i32)
    nblk_e = (counts + MOE_BLOCK - 1) // MOE_BLOCK
    blk_start = jnp.cumsum(nblk_e) - nblk_e
    d1 = blk_start[e1] * MOE_BLOCK + route[:, 4].astype(i32)
    d2 = blk_start[e2] * MOE_BLOCK + route[:, 5].astype(i32)
    max_blocks = -(-2 * n // MOE_BLOCK) + n_exp
    rows = (max_blocks + MOE_ITEM_BLOCKS) * MOE_BLOCK
    tok = jnp.arange(n, dtype=i32)
    row_tok = jnp.full((rows,), -1, i32).at[jnp.concatenate([d1, d2])].set(
        jnp.concatenate([tok, tok]), unique_indices=True)
    items_e = (nblk_e + MOE_ITEM_BLOCKS - 1) // MOE_ITEM_BLOCKS
    item_end = jnp.cumsum(items_e)
    n_items = item_end[-1]
    max_items = -(-max_blocks // MOE_ITEM_BLOCKS) + n_exp
    it = jnp.arange(max_items, dtype=i32)
    it_c = jnp.minimum(it, n_items - 1)
    item_e = jnp.minimum(jnp.searchsorted(item_end, it_c, side="right"), n_exp - 1).astype(i32)
    local = it_c - (item_end - items_e)[item_e]
    item_r = (blk_start[item_e] + local * MOE_ITEM_BLOCKS).astype(i32)
    item_n = jnp.where(it < n_items, jnp.minimum(MOE_ITEM_BLOCKS, nblk_e[item_e] - local * MOE_ITEM_BLOCKS), 0)

    xs = _dispatch(row_tok, x1, norm_ffn)
    meta = jnp.stack([n_items, jnp.sum(nblk_e)]).astype(i32)
    yb = _expert_ffn(item_e, item_r, item_n.astype(i32), meta, xs, w_gate, w_up, w_down, max_items)
    return _combine(d1, d2, yb, x1, route, norm_final)


def kernel(x_prompt, x_sample, cache_k, cache_v, cache_idx_k, state_hgrn, page_table, norm_mix, w_in,
           lb_logits, norm_hgrn_out, w_out, norm_ffn, router_group, router_group_bias, router_expert,
           router_expert_bias, expert_w_gate, expert_w_up, expert_w_down, norm_final):
    n_b, seq, d = x_prompt.shape
    n_s, n_tok, _ = x_sample.shape
    depth = norm_mix.shape[0]
    assert depth == 1
    kd = A_HEADS * A_EXPAND
    dh = d // B_HEADS
    kvd = B_KV_HEADS * dh
    group = B_HEADS // B_KV_HEADS
    np_rows = n_b * seq
    ns_rows = n_s * n_tok
    n_all = np_rows + ns_rows
    n_pad = -(-n_all // ROW_BLOCK) * ROW_BLOCK
    assert np_rows % ROW_BLOCK == 0

    lower = jnp.cumsum(jax.nn.softmax(lb_logits.astype(f32), axis=0), axis=0)[0]
    x_all = jnp.concatenate([x_prompt.reshape(np_rows, d), x_sample.reshape(ns_rows, d),
                             jnp.zeros((n_pad - n_all, d), f32)], axis=0)

    w = w_in[0]
    o_bq = 4 * kd
    o_bk = o_bq + d
    o_bqi = o_bk + 2 * kvd
    o_sm = o_bqi + IDX_HEADS * IDX_DIM
    o_g = o_sm + IDX_DIM + IDX_HEADS
    small_w = 2 * LANE
    h = _rmsnorm_bf16(x_all, norm_mix[0])
    wb = w.astype(bf16)
    z_a = _matmul(h, wb, name="proj_hgrn", col0=0, n=o_bq)
    z_bq = _matmul(h, wb, name="proj_q", col0=o_bq, n=o_bk - o_bq)
    z_kv = _matmul(h, wb, name="proj_kv", col0=o_bk, n=o_bqi - o_bk)
    z_qi = _matmul_heads(h, wb, "proj_qidx", o_bqi, o_sm - o_bqi)
    w_small = jnp.zeros((d, small_w), bf16).at[:, :o_g - o_sm].set(wb[:, o_sm:o_g])
    z_sm = _matmul(h, w_small, name="proj_small")
    z_g = _matmul(h, wb[:, o_g:], name="proj_gates")

    rb = min(HGRN_ROWS, seq)
    oa_p, s_p = _hgrn(z_a, lower, norm_hgrn_out[0], n_b, seq, rb, min(HGRN_CHUNK, rb), rb,
                      min(HGRN_PROMPT_HEADS, A_HEADS))
    sp = HGRN_SAMPLE_ROWS
    zs = z_a[np_rows:n_all].reshape(n_s, n_tok, 4 * kd)
    zs = jnp.concatenate([zs, jnp.zeros((n_s, sp - n_tok, 4 * kd), f32)], axis=1).reshape(n_s * sp, 4 * kd)
    oa_sp, s_s = _hgrn(zs, lower, norm_hgrn_out[0], n_s, sp, sp, sp, n_tok, min(HGRN_SAMPLE_HEADS, A_HEADS),
                       s0=state_hgrn[0])
    oa_s = oa_sp.reshape(n_s, sp, d)[:, :n_tok].reshape(ns_rows, d)

    ob_p = _dsa_prompt(z_qi, z_sm, z_bq, z_kv, n_b, seq)
    sm_s = z_sm[np_rows:n_all]
    w_s = sm_s[:, IDX_DIM:IDX_DIM + IDX_HEADS].reshape(n_s, n_tok, IDX_HEADS) * (IDX_HEADS ** -0.5 * IDX_DIM ** -0.5)
    eye = jnp.eye(8, n_tok, dtype=f32)
    wsel = (eye[None, :, :, None] * w_s[:, None, :, :]).reshape(n_s, 8, n_tok * IDX_HEADS)
    qi_s = z_qi[np_rows // ROW_BLOCK:].transpose(0, 2, 1, 3).reshape(n_pad - np_rows, IDX_HEADS, IDX_DIM)
    qi_s = qi_s[:ns_rows].reshape(n_s, n_tok * IDX_HEADS, IDX_DIM)
    q_s = z_bq[np_rows:n_all].reshape(n_s, n_tok * B_HEADS, dh)

    def pad_rows_to_lane(a):
        return jnp.concatenate([a, jnp.zeros((n_s, LANE - n_tok) + a.shape[2:], f32)], axis=1)

    def new_kv(a):
        return pad_rows_to_lane(a.reshape(n_s, n_tok, B_KV_HEADS, dh)).reshape(n_s, LANE * B_KV_HEADS, dh)

    kv_s = z_kv[np_rows:n_all]
    ob_s = _dsa_sample(qi_s, wsel, q_s, pad_rows_to_lane(sm_s[:, :IDX_DIM].reshape(n_s, n_tok, IDX_DIM)),
                       new_kv(kv_s[:, :kvd]), new_kv(kv_s[:, kvd:]),
                       cache_k, cache_v, cache_idx_k, page_table, n_tok)
    ob_s = ob_s.reshape(ns_rows, d)

    pad_rows = jnp.zeros((n_pad - n_all, d), f32)
    oa = jnp.concatenate([oa_p, oa_s, pad_rows], axis=0)
    ob = jnp.concatenate([ob_p, ob_s, pad_rows], axis=0)
    m = _merge(z_g, oa, ob)
    x1 = _matmul(m, w_out[0].astype(bf16), res=x_all, name="out_proj")

    y = _moe(x1, norm_ffn[0], router_group[0], router_group_bias[0], router_expert[0], router_expert_bias[0],
             expert_w_gate, expert_w_up, expert_w_down, norm_final)

    kp = z_kv[:np_rows]
    ks = z_kv[np_rows:n_all]
    return (y[:np_rows].reshape(n_b, seq, d),
            y[np_rows:n_all].reshape(n_s, n_tok, d),
            kp[:, :kvd].reshape(1, n_b, seq, B_KV_HEADS, dh),
            kp[:, kvd:].reshape(1, n_b, seq, B_KV_HEADS, dh),
            z_sm[:np_rows, :IDX_DIM].reshape(1, n_b, seq, IDX_DIM),
            s_p[None],
            ks[:, :kvd].reshape(1, n_s, n_tok, B_KV_HEADS, dh),
            ks[:, kvd:].reshape(1, n_s, n_tok, B_KV_HEADS, dh),
            z_sm[np_rows:n_all, :IDX_DIM].reshape(1, n_s, n_tok, IDX_DIM),
            s_s[None])
```

```python
import functools
import math

import jax
import jax.numpy as jnp
import numpy as np
from jax import lax
from jax.experimental import pallas as pl
from jax.experimental.pallas import tpu as pltpu

A_HEADS = 32
A_EXPAND = 128
B_HEADS = 32
B_KV_HEADS = 8
IDX_HEADS = 32
IDX_DIM = 128
TOPK_MAX = 256
N_GROUPS = 8
EXPERTS_PER_GROUP = 8
NORM_EPS = 1e-6

LANE = 128
ROW_BLOCK = 128
PROJ_ROW_TILE = 1664
PROJ_COL_TILE = 512
MOE_BLOCK = 128
MOE_ITEM_BLOCKS = 4
MOE_FF_TILE = 256
HGRN_CHUNK = 64
HGRN_ROWS = 512
HGRN_SAMPLE_ROWS = 16
HGRN_SAMPLE_HEADS = 8
HGRN_PROMPT_HEADS = 4
HGRN_SAFE_DECAY = 60.0
SAMPLE_PAGES_PER_STEP = 8
DMA_ISSUE_UNROLL = 8
COUNT_CHAINS = 5
DSA_KEY_STEP = 256
NEG_BIG = -1e30
LOG2E = 1.4426950408889634
VMEM_LIMIT = 56 * 1024 * 1024

f32 = jnp.float32
bf16 = jnp.bfloat16
i32 = jnp.int32
HIGHEST = lax.Precision.HIGHEST
NT_DIMS = (((1,), (1,)), ((), ()))
TN_DIMS = (((0,), (0,)), ((), ()))


def _pick_tile(n, cap, align):
    t = (min(cap, n) // align) * align
    while t > align and n % t:
        t -= align
    assert t > 0 and n % t == 0, (n, cap, align)
    return t


def _params(sem, limit=VMEM_LIMIT):
    return pltpu.CompilerParams(dimension_semantics=sem, vmem_limit_bytes=limit)


def _sigmoid(x):
    return 1.0 / (1.0 + jnp.exp(-x))


def _log2(n):
    s = int(math.log2(n))
    assert 1 << s == n, n
    return s


def _rmsnorm_kernel(x_ref, g_ref, o_ref):
    x = x_ref[...]
    ms = jnp.mean(x * x, axis=-1, keepdims=True)
    o_ref[...] = (x * lax.rsqrt(ms + NORM_EPS) * g_ref[...]).astype(o_ref.dtype)


def _rmsnorm_bf16(x, g):
    n, d = x.shape
    return pl.pallas_call(
        _rmsnorm_kernel,
        grid=(n // ROW_BLOCK,),
        in_specs=[pl.BlockSpec((ROW_BLOCK, d), lambda i: (i, 0)),
                  pl.BlockSpec((1, d), lambda i: (0, 0))],
        out_specs=pl.BlockSpec((ROW_BLOCK, d), lambda i: (i, 0)),
        out_shape=jax.ShapeDtypeStruct((n, d), bf16),
        compiler_params=_params(("parallel",)),
        name="rmsnorm_bf16",
    )(x, g.reshape(1, d))


def _mm_kernel(a_ref, b_ref, o_ref):
    o_ref[...] = jnp.dot(a_ref[...], b_ref[...], preferred_element_type=f32)


def _mm_res_kernel(a_ref, b_ref, r_ref, o_ref):
    o_ref[...] = r_ref[...] + jnp.dot(a_ref[...], b_ref[...], preferred_element_type=f32)


def _matmul(a, b, res=None, name="matmul", col0=0, n=None):
    m, k = a.shape
    n = b.shape[1] if n is None else n
    tm = _pick_tile(m, PROJ_ROW_TILE, ROW_BLOCK)
    tn = _pick_tile(math.gcd(n, col0) if col0 else n, PROJ_COL_TILE, LANE)
    cb = col0 // tn
    in_specs = [pl.BlockSpec((tm, k), lambda j, i: (i, 0)),
                pl.BlockSpec((k, tn), lambda j, i: (0, cb + j))]
    args = [a, b]
    kern = _mm_kernel
    if res is not None:
        in_specs.append(pl.BlockSpec((tm, tn), lambda j, i: (i, j)))
        args.append(res)
        kern = _mm_res_kernel
    return pl.pallas_call(
        kern,
        grid=(n // tn, m // tm),
        in_specs=in_specs,
        out_specs=pl.BlockSpec((tm, tn), lambda j, i: (i, j)),
        out_shape=jax.ShapeDtypeStruct((m, n), f32),
        compiler_params=_params(("parallel", "parallel")),
        name=name,
    )(*args)


def _mm_heads_kernel(a_ref, b_ref, o_ref):
    acc = jnp.dot(a_ref[...], b_ref[...], preferred_element_type=f32)
    for r in range(o_ref.shape[0]):
        for hh in range(o_ref.shape[1]):
            o_ref[r, hh] = acc[r * ROW_BLOCK:(r + 1) * ROW_BLOCK, hh * LANE:(hh + 1) * LANE]


def _matmul_heads(a, b, name, col0, n):
    m, k = a.shape
    tm = _pick_tile(m, PROJ_ROW_TILE, ROW_BLOCK)
    tn = _pick_tile(math.gcd(n, col0) if col0 else n, PROJ_COL_TILE, LANE)
    cb = col0 // tn
    return pl.pallas_call(
        _mm_heads_kernel,
        grid=(n // tn, m // tm),
        in_specs=[pl.BlockSpec((tm, k), lambda j, i: (i, 0)),
                  pl.BlockSpec((k, tn), lambda j, i: (0, cb + j))],
        out_specs=pl.BlockSpec((tm // ROW_BLOCK, tn // LANE, ROW_BLOCK, LANE), lambda j, i: (i, j, 0, 0)),
        out_shape=jax.ShapeDtypeStruct((m // ROW_BLOCK, n // LANE, ROW_BLOCK, LANE), f32),
        compiler_params=_params(("parallel", "parallel")),
        name=name,
    )(a, b)


def _hgrn_static(c):
    levels = _log2(c)
    t = np.arange(c)
    cmat = np.zeros((2 * c, c), np.float32)
    cmat[:c] = t[None, :] <= t[:, None]
    cmat[c:] = 1.0
    lmat = np.zeros((levels * c, c), np.float32)
    lv = np.full((c, c), -1, np.int32)
    for l in range(levels):
        blk = c >> l
        half = blk // 2
        mid = (t // blk) * blk + half
        lmat[l * c:(l + 1) * c] = t[None, :] <= (mid - 1)[:, None]
        same = (t[:, None] // blk) == (t[None, :] // blk)
        up = ((t // half) % 2 == 1)[:, None]
        lo = ((t // half) % 2 == 0)[None, :]
        lv[same & up & lo] = l
    lv[t, t] = levels
    return cmat, lmat, lv, levels


def _split3(x):
    hi = x.astype(bf16)
    r = x - hi.astype(f32)
    mid = r.astype(bf16)
    lo = (r - mid.astype(f32)).astype(bf16)
    return jnp.concatenate([hi, mid, lo], axis=1)


def _sum3(y):
    w = y.shape[1] // 3
    return y[:, :w] + y[:, w:2 * w] + y[:, 2 * w:]


def _hgrn_kernel(*refs, rows, chunk, levels, n_valid, heads, has_init):
    if has_init:
        (zq_ref, zf_ref, zi_ref, zg_ref, lb_ref, gain_ref, cmat_ref, lmat_ref, lv_ref, s0_ref,
         oa_ref, sfin_ref, st_ref, a_ref, q_sc, k_sc, b_sc, p_sc) = refs
    else:
        (zq_ref, zf_ref, zi_ref, zg_ref, lb_ref, gain_ref, cmat_ref, lmat_ref, lv_ref,
         oa_ref, sfin_ref, st_ref, a_ref, q_sc, k_sc, b_sc, p_sc) = refs
        s0_ref = None
    tb = pl.program_id(2)
    c = chunk
    nc = rows // c
    dk = A_EXPAND

    @pl.when(tb == 0)
    def _():
        for hh in range(heads):
            if has_init:
                st_ref[hh] = s0_ref[0, hh].T
            else:
                st_ref[hh] = jnp.zeros(st_ref.shape[1:], f32)

    lv = lv_ref[...]
    cmat = cmat_ref[...]
    row_c = lax.broadcasted_iota(i32, (c, dk), 0)

    pre = []
    unsafe = None
    for hh in range(heads):
        ls = slice(hh * dk, (hh + 1) * dk)
        zq = zq_ref[:, ls]
        lb = lb_ref[:, ls]
        f = lb + (1.0 - lb) * _sigmoid(zf_ref[:, ls])
        logf = jnp.log(f)
        k = 1.0 - f
        q = zq * _sigmoid(zq)
        if n_valid < rows:
            live = lax.broadcasted_iota(i32, (rows, dk), 0) < n_valid
            logf = jnp.where(live, logf, 0.0)
            k = jnp.where(live, k, 0.0)
        parts = _split3(logf)
        bs, bls = [], []
        for ci in range(nc):
            y = _sum3(jnp.dot(cmat, parts[ci * c:(ci + 1) * c], preferred_element_type=f32))
            bs.append(y[:c])
            bls.append(y[c:])
        b = jnp.concatenate(bs, axis=0) if nc > 1 else bs[0]
        bl = jnp.concatenate(bls, axis=0) if nc > 1 else bls[0]
        head_unsafe = jnp.min(bl) < -HGRN_SAFE_DECAY
        unsafe = head_unsafe if unsafe is None else jnp.logical_or(unsafe, head_unsafe)
        pre.append(dict(q=q, k=k, b=b, bl=bl, parts=parts, qh_b=(q * jnp.exp(b)).astype(bf16),
                        kd=(k * jnp.exp(bl - b)).astype(bf16), v_b=zi_ref[:, ls].astype(bf16)))

    @pl.when(jnp.logical_not(unsafe))
    def _():
        for hh, hv in enumerate(pre):
            kt = (hv["k"] * jnp.exp(-hv["b"])).astype(bf16)
            for ci in range(nc):
                rs = slice(ci * c, (ci + 1) * c)
                al = lax.dot_general(hv["qh_b"][rs], kt[rs], NT_DIMS, preferred_element_type=f32)
                a_ref[hh * nc + ci] = jnp.where(lv >= 0, al, 0.0)

    @pl.when(unsafe)
    def _():
        for hh, hv in enumerate(pre):
            hr = slice(hh * rows, (hh + 1) * rows)
            q_sc[hr, :] = hv["q"]
            k_sc[hr, :] = hv["k"]
            b_sc[hr, :] = hv["b"]
            p_sc[hr, :] = hv["parts"]

        def safe_chunk(ci, carry):
            r0 = pl.multiple_of(ci * c, c)
            q_c = q_sc[pl.ds(r0, c), :]
            k_c = k_sc[pl.ds(r0, c), :]
            b_c = b_sc[pl.ds(r0, c), :]
            refs_b = _sum3(jnp.dot(lmat_ref[...], p_sc[pl.ds(r0, c), :], preferred_element_type=f32))
            a = jnp.where(lv == levels, jnp.sum(q_c * k_c, axis=-1, keepdims=True), 0.0)
            for l in range(levels):
                e = jnp.exp(-jnp.abs(b_c - refs_b[l * c:(l + 1) * c]))
                upper = ((row_c >> (levels - l - 1)) & 1) == 1
                qt = jnp.where(upper, q_c * e, 0.0).astype(bf16)
                kt = jnp.where(upper, 0.0, k_c * e).astype(bf16)
                al = lax.dot_general(qt, kt, NT_DIMS, preferred_element_type=f32)
                a = jnp.where(lv == l, al, a)
            a_ref[ci] = a
            return carry

        lax.fori_loop(0, heads * nc, safe_chunk, 0)

    for hh, hv in enumerate(pre):
        ls = slice(hh * dk, (hh + 1) * dk)
        gain = gain_ref[:, ls]
        zg = zg_ref[:, ls]
        st = st_ref[hh]
        for ci in range(nc):
            rs = slice(ci * c, (ci + 1) * c)
            o = (jnp.dot(a_ref[hh * nc + ci].astype(bf16), hv["v_b"][rs], preferred_element_type=f32)
                 + lax.dot_general(hv["qh_b"][rs], st.astype(bf16), NT_DIMS, preferred_element_type=f32))
            ms = jnp.mean(o * o, axis=-1, keepdims=True)
            zg_c = zg[rs]
            oa_ref[rs, ls] = o * lax.rsqrt(ms + NORM_EPS) * gain * (zg_c * _sigmoid(zg_c))
            st = (st * jnp.exp(hv["bl"][ci * c:ci * c + 1])
                  + lax.dot_general(hv["v_b"][rs], hv["kd"][rs], TN_DIMS, preferred_element_type=f32))
        st_ref[hh] = st

    @pl.when(tb == pl.num_programs(2) - 1)
    def _():
        for hh in range(heads):
            sfin_ref[0, hh] = st_ref[hh].T


def _hgrn(z_a, lb, gain, n_seq, seq_rows, row_block, chunk, n_valid, heads, s0=None):
    kd = A_HEADS * A_EXPAND
    dv = LANE
    assert z_a.shape[1] == 2 * kd + 2 * A_HEADS * dv and A_EXPAND == LANE and A_HEADS % heads == 0
    cmat, lmat, lv, levels = _hgrn_static(chunk)
    nrb = seq_rows // row_block
    hb = A_HEADS // heads
    wd = heads * LANE

    def zspec(seg):
        return pl.BlockSpec((row_block, wd), lambda b, h, t, seg=seg: (b * nrb + t, seg * hb + h))

    const = lambda b, h, t: (0, 0)
    in_specs = [zspec(0), zspec(1), zspec(2), zspec(3),
                pl.BlockSpec((1, wd), lambda b, h, t: (0, h)),
                pl.BlockSpec((1, wd), lambda b, h, t: (0, h)),
                pl.BlockSpec(cmat.shape, const), pl.BlockSpec(lmat.shape, const), pl.BlockSpec(lv.shape, const)]
    args = [z_a, z_a, z_a, z_a, lb.reshape(1, kd), gain.reshape(1, -1),
            jnp.asarray(cmat, bf16), jnp.asarray(lmat, bf16), jnp.asarray(lv)]
    if s0 is not None:
        in_specs.append(pl.BlockSpec((1, heads, A_EXPAND, dv), lambda b, h, t: (b, h, 0, 0)))
        args.append(s0)
    kern = functools.partial(_hgrn_kernel, rows=row_block, chunk=chunk, levels=levels, n_valid=n_valid,
                             heads=heads, has_init=s0 is not None)
    return pl.pallas_call(
        kern,
        grid=(n_seq, hb, nrb),
        in_specs=in_specs,
        out_specs=[pl.BlockSpec((row_block, wd), lambda b, h, t: (b * nrb + t, h)),
                   pl.BlockSpec((1, heads, A_EXPAND, dv), lambda b, h, t: (b, h, 0, 0))],
        out_shape=[jax.ShapeDtypeStruct((n_seq * seq_rows, A_HEADS * dv), f32),
                   jax.ShapeDtypeStruct((n_seq, A_HEADS, A_EXPAND, dv), f32)],
        scratch_shapes=[pltpu.VMEM((heads, dv, A_EXPAND), f32),
                        pltpu.VMEM((heads * row_block // chunk, chunk, chunk), f32),
                        pltpu.VMEM((heads * row_block, LANE), f32), pltpu.VMEM((heads * row_block, LANE), f32),
                        pltpu.VMEM((heads * row_block, LANE), f32),
                        pltpu.VMEM((heads * row_block, 3 * LANE), bf16)],
        compiler_params=_params(("parallel", "parallel", "arbitrary")),
        name="hgrn2_prompt" if s0 is None else "hgrn2_sample",
    )(*args)


def _sort_key(x):
    b = pltpu.bitcast(x + 0.0, i32)
    return b ^ ((b >> 31) & 0x7FFFFFFF)


def _kth_largest_key(key_ref, k, axis):
    shape = (1, key_ref.shape[1]) if axis == 0 else (key_ref.shape[0], 1)
    int_min = jnp.int32(-2 ** 31)

    n = key_ref.shape[axis]
    parts = max(p for p in range(1, COUNT_CHAINS + 1) if (n // LANE) % p == 0)
    seg = n // parts

    def step(i, lo):
        cand = lo ^ lax.shift_left(jnp.int32(1), 31 - i)
        cnt = None
        for pi in range(parts):
            sl = slice(pi * seg, (pi + 1) * seg)
            keys = key_ref[sl, :] if axis == 0 else key_ref[:, sl]
            c = jnp.sum((keys >= cand).astype(i32), axis=axis, keepdims=True)
            cnt = c if cnt is None else cnt + c
        return jnp.where(cnt >= k, cand, lo)

    return lax.fori_loop(0, 32, step, jnp.full(shape, int_min, i32))


def _dsa_prompt_kernel(qi_ref, smq_ref, smk_ref, q_ref, k_ref, v_ref, o_ref, mask_ref, key_ref, qs_ref,
                       *, kext, topk, qblk, group, qb0):
    qb = qb0 + pl.program_id(1)
    g = pl.program_id(2)
    n_tiles = kext // qblk

    def causal(r0):
        kpos = r0 + lax.broadcasted_iota(i32, (qblk, qblk), 0)
        qpos = qb * qblk + lax.broadcasted_iota(i32, (qblk, qblk), 1)
        return kpos <= qpos

    @pl.when(g == 0)
    def _():
        qs_ref[...] = qi_ref[0].reshape(IDX_HEADS * qblk, IDX_DIM).T.astype(bf16)
        w_t = smq_ref[...].T * (IDX_HEADS ** -0.5 * IDX_DIM ** -0.5)

        def score_tile(ti, carry):
            r0 = pl.multiple_of(ti * qblk, qblk)
            kid = smk_ref[pl.ds(r0, qblk), 0:IDX_DIM].astype(bf16)
            s_all = jnp.dot(kid, qs_ref[...], preferred_element_type=f32)
            acc = jnp.zeros((qblk, qblk), f32)
            for h in range(IDX_HEADS):
                acc = acc + (jnp.maximum(s_all[:, h * qblk:(h + 1) * qblk], 0.0)
                             * w_t[IDX_DIM + h:IDX_DIM + h + 1, :])
            key_ref[pl.ds(r0, qblk), :] = _sort_key(jnp.where(causal(r0), acc, -jnp.inf))
            return carry

        lax.fori_loop(0, n_tiles, score_tile, 0)
        thr = _kth_largest_key(key_ref, topk, axis=0)
        keys = key_ref[...]
        kpos = lax.broadcasted_iota(i32, (kext, qblk), 0)
        qpos = qb * qblk + lax.broadcasted_iota(i32, (kext, qblk), 1)
        mask_ref[...] = jnp.where(jnp.logical_and(kpos <= qpos, keys >= thr), 0.0, -jnp.inf)
        need = topk - jnp.sum((keys > thr).astype(i32), axis=0, keepdims=True)
        n_tie = jnp.sum((keys == thr).astype(i32), axis=0, keepdims=True)

        @pl.when(jnp.max(n_tie - need) > 0)
        def _():
            strict = (lax.broadcasted_iota(i32, (qblk, qblk), 0) > lax.broadcasted_iota(i32, (qblk, qblk), 1))
            strict = jnp.where(strict, 1.0, 0.0).astype(bf16)
            need_f = need.astype(f32)

            def tile(ti, seen):
                r0 = pl.multiple_of(ti * qblk, qblk)
                kt = key_ref[pl.ds(r0, qblk), :]
                tie = kt == thr
                tie_f = jnp.where(tie, 1.0, 0.0)
                before = seen + jnp.dot(strict, tie_f.astype(bf16), preferred_element_type=f32)
                keep = jnp.logical_or(kt > thr, jnp.logical_and(tie, before < need_f))
                keep = jnp.logical_and(keep, causal(r0))
                mask_ref[pl.ds(r0, qblk), :] = jnp.where(keep, 0.0, -jnp.inf)
                return seen + jnp.sum(tie_f, axis=0, keepdims=True)

            lax.fori_loop(0, n_tiles, tile, jnp.zeros((1, qblk), f32))

    kk = k_ref[0:kext, :].astype(bf16)
    vt = v_ref[0:kext, :].T.astype(bf16)
    dh = kk.shape[1]
    for hq in range(group):
        qh = (q_ref[:, hq * dh:(hq + 1) * dh] * (dh ** -0.5 * LOG2E)).astype(bf16)
        s = lax.dot_general(kk, qh, NT_DIMS, preferred_element_type=f32) + mask_ref[...]
        m = jnp.max(s, axis=0, keepdims=True)
        p = jnp.exp2(s - m)
        den = jnp.sum(p, axis=0, keepdims=True)
        o_t = jnp.dot(vt, p.astype(bf16), preferred_element_type=f32) / den
        o_ref[:, hq * dh:(hq + 1) * dh] = o_t.T


def _dsa_prompt(bqi, small, bq, bkv, n_batch, seq):
    dh = bq.shape[1] // B_HEADS
    group = B_HEADS // B_KV_HEADS
    qblk = LANE
    nqb = seq // qblk
    topk = min(TOPK_MAX, seq // 4)
    parts = []
    qb0 = 0
    count = max(DSA_KEY_STEP // qblk, 1)
    assert nqb % count == 0
    while qb0 < nqb:
        kext = (qb0 + count) * qblk
        kern = functools.partial(_dsa_prompt_kernel, kext=kext, topk=topk, qblk=qblk, group=group, qb0=qb0)
        part = pl.pallas_call(
            kern,
            grid=(n_batch, count, B_KV_HEADS),
            in_specs=[pl.BlockSpec((1, IDX_HEADS, qblk, IDX_DIM), lambda b, j, g, q0=qb0: (b * nqb + q0 + j, 0, 0, 0)),
                      pl.BlockSpec((qblk, small.shape[1]), lambda b, j, g, q0=qb0: (b * nqb + q0 + j, 0)),
                      pl.BlockSpec((seq, small.shape[1]), lambda b, j, g: (b, 0)),
                      pl.BlockSpec((qblk, group * dh), lambda b, j, g, q0=qb0: (b * nqb + q0 + j, g)),
                      pl.BlockSpec((seq, dh), lambda b, j, g: (b, g)),
                      pl.BlockSpec((seq, dh), lambda b, j, g: (b, B_KV_HEADS + g))],
            out_specs=pl.BlockSpec((qblk, group * dh), lambda b, j, g, cnt=count: (b * cnt + j, g)),
            out_shape=jax.ShapeDtypeStruct((n_batch * count * qblk, B_HEADS * dh), f32),
            scratch_shapes=[pltpu.VMEM((kext, qblk), f32), pltpu.VMEM((kext, qblk), i32),
                            pltpu.VMEM((IDX_DIM, IDX_HEADS * qblk), bf16)],
            compiler_params=_params(("parallel", "parallel", "arbitrary")),
            name=f"dsa_prompt_k{kext}",
        )(bqi, small, small, bq, bkv, bkv)
        parts.append(part.reshape(n_batch, count * qblk, B_HEADS * dh))
        qb0 += count
    return jnp.concatenate(parts, axis=1).reshape(n_batch * seq, B_HEADS * dh)


def _dsa_sample_select_kernel(pt_ref, qi_ref, wsel_ref, knew_ref, cik_hbm, mask_ref,
                              slab_ref, sc_ref, key_ref, sel_ref, sem, *, n_pages, page, topk, n_tok, kvh):
    b = pl.program_id(0)
    n_seq = pl.num_programs(0)
    past = n_pages * page

    def page_copy(seq, slot, j):
        return pltpu.make_async_copy(cik_hbm.at[0, pt_ref[seq, j]],
                                     slab_ref.at[slot, pl.ds(pl.multiple_of(j * page, page), page), :],
                                     sem.at[slot])

    def start(seq, slot):
        lax.fori_loop(0, n_pages, lambda j, c: (page_copy(seq, slot, j).start(), c)[1], 0)

    def wait(seq, slot):
        lax.fori_loop(0, n_pages, lambda j, c: (page_copy(seq, slot, j).wait(), c)[1], 0)

    @pl.when(b == 0)
    def _():
        start(0, 0)

    slot = b % 2

    @pl.when(b + 1 < n_seq)
    def _():
        start(b + 1, 1 - slot)

    wait(b, slot)
    qi = qi_ref[0].astype(bf16)
    wsel = wsel_ref[0]

    def tile_scores(keys):
        s = jnp.dot(qi, keys.T.astype(bf16), preferred_element_type=f32)
        return jnp.dot(wsel, jnp.maximum(s, 0.0), precision=HIGHEST, preferred_element_type=f32)

    sc_ref[:, 0:past] = tile_scores(slab_ref[slot])
    new = tile_scores(knew_ref[0])
    r = lax.broadcasted_iota(i32, new.shape, 0)
    cidx = lax.broadcasted_iota(i32, new.shape, 1)
    sc_ref[:, past:past + LANE] = jnp.where(jnp.logical_and(cidx <= r, cidx < n_tok), new, -jnp.inf)
    sc = sc_ref[...]
    key_ref[...] = _sort_key(sc)
    thr = _kth_largest_key(key_ref, topk, axis=1)
    keys = key_ref[...]
    sel = jnp.logical_and(keys >= thr, sc > -jnp.inf)
    sel_ref[...] = jnp.where(sel, 1.0, 0.0)
    need = topk - jnp.sum((keys > thr).astype(i32), axis=1, keepdims=True)
    n_tie = jnp.sum((keys == thr).astype(i32), axis=1, keepdims=True)

    @pl.when(jnp.max(n_tie - need) > 0)
    def _():
        strict = (lax.broadcasted_iota(i32, (LANE, LANE), 0) < lax.broadcasted_iota(i32, (LANE, LANE), 1))
        strict = jnp.where(strict, 1.0, 0.0).astype(bf16)
        need_f = need.astype(f32)

        def tile(ti, seen):
            c1 = pl.multiple_of(ti * LANE, LANE)
            kt = key_ref[:, pl.ds(c1, LANE)]
            tie = kt == thr
            tie_f = jnp.where(tie, 1.0, 0.0)
            before = seen + jnp.dot(tie_f.astype(bf16), strict, preferred_element_type=f32)
            keep = jnp.logical_or(kt > thr, jnp.logical_and(tie, before < need_f))
            keep = jnp.logical_and(keep, sc_ref[:, pl.ds(c1, LANE)] > -jnp.inf)
            sel_ref[:, pl.ds(c1, LANE)] = jnp.where(keep, 1.0, 0.0)
            return seen + jnp.sum(tie_f, axis=1, keepdims=True)

        lax.fori_loop(0, sc_ref.shape[1] // LANE, tile, jnp.zeros((8, 1), f32))

    lane = lax.broadcasted_iota(i32, (8, LANE), 1)

    def expand(ti, carry):
        c1 = pl.multiple_of(ti * LANE, LANE)
        hit = sel_ref[:, pl.ds(c1, LANE)]
        for j in range(kvh):
            src = j * (LANE // kvh) + (lane >> _log2(kvh))
            piece = jnp.take_along_axis(hit, src, axis=1)
            c2 = pl.multiple_of(ti * (LANE * kvh) + j * LANE, LANE)
            mask_ref[0, :, pl.ds(c2, LANE)] = jnp.where(piece > 0.5, 0.0, NEG_BIG)
        return carry

    lax.fori_loop(0, sc_ref.shape[1] // LANE, expand, 0)


def _dsa_sample_attn_kernel(pt_ref, q_ref, mask_ref, masknew_ref, knew_ref, vnew_ref, gm_ref, ck_hbm, cv_hbm,
                            o_ref, kslab, vslab, m_ref, l_ref, acc_ref, sem,
                            *, n_chunks, ch, n_tok, kvh, group, dh, page):
    b = pl.program_id(0)
    c = pl.program_id(1)
    t = b * n_chunks + c
    total = pl.num_programs(0) * n_chunks
    scale = dh ** -0.5

    def copies(seq, chunk, slot, j):
        pid = pt_ref[seq, chunk * ch + j]
        return (pltpu.make_async_copy(ck_hbm.at[0, pid], kslab.at[slot, j], sem.at[slot, 0]),
                pltpu.make_async_copy(cv_hbm.at[0, pid], vslab.at[slot, j], sem.at[slot, 1]))

    def start(seq, chunk, slot):
        for j in range(ch):
            ck, cv = copies(seq, chunk, slot, j)
            ck.start()
            cv.start()

    def wait(seq, chunk, slot):
        for j in range(ch):
            ck, cv = copies(seq, chunk, slot, j)
            ck.wait()
            cv.wait()

    @pl.when(t == 0)
    def _():
        start(0, 0, 0)

    slot = t % 2

    @pl.when(t + 1 < total)
    def _():
        nxt = t + 1
        start(nxt // n_chunks, nxt % n_chunks, 1 - slot)

    wait(b, c, slot)

    @pl.when(c == 0)
    def _():
        m_ref[...] = jnp.full_like(m_ref, NEG_BIG)
        l_ref[...] = jnp.zeros_like(l_ref)
        acc_ref[...] = jnp.zeros_like(acc_ref)

    qb = (q_ref[0] * scale).astype(bf16)
    heads = kvh * group

    def attend(keys_t, vals, mask8):
        n_cols = keys_t.shape[1]
        tok_mask = jnp.concatenate([jnp.broadcast_to(mask8[tk:tk + 1], (heads, n_cols))
                                    for tk in range(n_tok)], axis=0)
        madd = tok_mask + gm_ref[:, 0:n_cols]
        s = jnp.dot(qb, keys_t, preferred_element_type=f32) + madd
        m_old = m_ref[...]
        m_new = jnp.maximum(m_old, jnp.max(s, axis=-1, keepdims=True))
        pr = jnp.where(madd == 0.0, jnp.exp(s - m_new), 0.0)
        alpha = jnp.exp(m_old - m_new)
        l_ref[...] = alpha * l_ref[...] + jnp.sum(pr, axis=-1, keepdims=True)
        acc_ref[...] = alpha * acc_ref[...] + jnp.dot(pr.astype(bf16), vals, preferred_element_type=f32)
        m_ref[...] = m_new

    attend(kslab[slot].reshape(ch * page * kvh, dh).T.astype(bf16),
           vslab[slot].reshape(ch * page * kvh, dh).astype(bf16), mask_ref[0])

    @pl.when(c == n_chunks - 1)
    def _():
        attend(knew_ref[0].T.astype(bf16), vnew_ref[0].astype(bf16), masknew_ref[0])
        o_ref[0] = acc_ref[...] / l_ref[...]


def _dsa_sample(qi, wsel, q, knew_idx, k_new, v_new, cache_k, cache_v, cache_idx_k, page_table, n_tok):
    n_seq, n_pages = page_table.shape
    page, kvh, dh = cache_k.shape[2], cache_k.shape[3], cache_k.shape[4]
    assert page == LANE and LANE % kvh == 0
    rows_i = qi.shape[1]
    rows = q.shape[1]
    heads = rows // n_tok
    group = heads // kvh
    past = n_pages * page
    kp = past + LANE
    topk = min(TOPK_MAX, (past + n_tok) // 4)
    ch = math.gcd(SAMPLE_PAGES_PER_STEP, n_pages)
    n_chunks = n_pages // ch
    cw = ch * page * kvh
    own = (np.arange(rows)[:, None] % heads) // group == (np.arange(cw)[None, :] % kvh)
    gm = np.where(own, 0.0, NEG_BIG).astype(np.float32)

    mask = pl.pallas_call(
        functools.partial(_dsa_sample_select_kernel, n_pages=n_pages, page=page, topk=topk, n_tok=n_tok,
                          kvh=kvh),
        grid_spec=pltpu.PrefetchScalarGridSpec(
            num_scalar_prefetch=1,
            grid=(n_seq,),
            in_specs=[pl.BlockSpec((1, rows_i, IDX_DIM), lambda b, pt: (b, 0, 0)),
                      pl.BlockSpec((1, 8, rows_i), lambda b, pt: (b, 0, 0)),
                      pl.BlockSpec((1, LANE, IDX_DIM), lambda b, pt: (b, 0, 0)),
                      pl.BlockSpec(memory_space=pl.ANY)],
            out_specs=pl.BlockSpec((1, 8, kp * kvh), lambda b, pt: (b, 0, 0)),
            scratch_shapes=[pltpu.VMEM((2, past, IDX_DIM), f32), pltpu.VMEM((8, kp), f32),
                            pltpu.VMEM((8, kp), i32), pltpu.VMEM((8, kp), f32), pltpu.SemaphoreType.DMA((2,))]),
        out_shape=jax.ShapeDtypeStruct((n_seq, 8, kp * kvh), f32),
        compiler_params=_params(("arbitrary",)),
        name="dsa_sample_select",
    )(page_table, qi, wsel, knew_idx, cache_idx_k)

    return pl.pallas_call(
        functools.partial(_dsa_sample_attn_kernel, n_chunks=n_chunks, ch=ch, n_tok=n_tok, kvh=kvh, group=group,
                          dh=dh, page=page),
        grid_spec=pltpu.PrefetchScalarGridSpec(
            num_scalar_prefetch=1,
            grid=(n_seq, n_chunks),
            in_specs=[pl.BlockSpec((1, rows, dh), lambda b, c, pt: (b, 0, 0)),
                      pl.BlockSpec((1, 8, cw), lambda b, c, pt: (b, 0, c)),
                      pl.BlockSpec((1, 8, LANE * kvh), lambda b, c, pt: (b, 0, past // LANE)),
                      pl.BlockSpec((1, LANE * kvh, dh), lambda b, c, pt: (b, 0, 0)),
                      pl.BlockSpec((1, LANE * kvh, dh), lambda b, c, pt: (b, 0, 0)),
                      pl.BlockSpec(gm.shape, lambda b, c, pt: (0, 0)),
                      pl.BlockSpec(memory_space=pl.ANY),
                      pl.BlockSpec(memory_space=pl.ANY)],
            out_specs=pl.BlockSpec((1, rows, dh), lambda b, c, pt: (b, 0, 0)),
            scratch_shapes=[pltpu.VMEM((2, ch, page, kvh, dh), f32), pltpu.VMEM((2, ch, page, kvh, dh), f32),
                            pltpu.VMEM((rows, 1), f32), pltpu.VMEM((rows, 1), f32),
                            pltpu.VMEM((rows, dh), f32), pltpu.SemaphoreType.DMA((2, 2))]),
        out_shape=jax.ShapeDtypeStruct((n_seq, rows, dh), f32),
        compiler_params=_params(("arbitrary", "arbitrary")),
        name="dsa_sample_attn",
    )(page_table, q, mask, mask, k_new, v_new, jnp.asarray(gm), cache_k, cache_v)


def _merge_kernel(ga_ref, gb_ref, oap_ref, obp_ref, oas_ref, obs_ref, o_ref, *, prompt_blocks):
    is_prompt = pl.program_id(0) < prompt_blocks
    oa = jnp.where(is_prompt, oap_ref[...], oas_ref[...])
    ob = jnp.where(is_prompt, obp_ref[...], obs_ref[...])
    o_ref[...] = (_sigmoid(ga_ref[...]) * oa + _sigmoid(gb_ref[...]) * ob).astype(o_ref.dtype)


def _merge(gates, oa_p, ob_p, oa_s, ob_s):
    n = gates.shape[0]
    d = oa_p.shape[1]
    npb = oa_p.shape[0] // ROW_BLOCK
    assert oa_p.shape[0] + oa_s.shape[0] == n and oa_s.shape[0] % ROW_BLOCK == 0
    tn = _pick_tile(d, 1024, 128)
    nj = d // tn
    p_spec = pl.BlockSpec((ROW_BLOCK, tn), lambda i, j: (jnp.minimum(i, npb - 1), jnp.where(i < npb, j, nj - 1)))
    s_spec = pl.BlockSpec((ROW_BLOCK, tn), lambda i, j: (jnp.maximum(i - npb, 0), jnp.where(i < npb, 0, j)))
    return pl.pallas_call(
        functools.partial(_merge_kernel, prompt_blocks=npb),
        grid=(n // ROW_BLOCK, nj),
        in_specs=[pl.BlockSpec((ROW_BLOCK, tn), lambda i, j: (i, j)),
                  pl.BlockSpec((ROW_BLOCK, tn), lambda i, j: (i, nj + j)),
                  p_spec, p_spec, s_spec, s_spec],
        out_specs=pl.BlockSpec((ROW_BLOCK, tn), lambda i, j: (i, j)),
        out_shape=jax.ShapeDtypeStruct((n, d), bf16),
        compiler_params=_params(("arbitrary", "arbitrary")),
        name="merge_branches",
    )(gates, gates, oa_p, ob_p, oa_s, ob_s)


def _router_kernel(x_ref, g_ref, w_ref, b_ref, route_ref, cnt_ref, carry_ref):
    i = pl.program_id(0)

    @pl.when(i == 0)
    def _():
        carry_ref[...] = jnp.zeros_like(carry_ref)

    x = x_ref[...]
    ms = jnp.mean(x * x, axis=-1, keepdims=True)
    h = x * lax.rsqrt(ms + NORM_EPS) * g_ref[...]
    logits = jnp.dot(h, w_ref[...], precision=HIGHEST, preferred_element_type=f32) + b_ref[...]
    lane = lax.broadcasted_iota(i32, logits.shape, 1)
    big = jnp.int32(1 << 20)
    ng, epg = N_GROUPS, EXPERTS_PER_GROUP

    def first_max(vals):
        mx = jnp.max(vals, axis=-1, keepdims=True)
        idx = jnp.min(jnp.where(vals == mx, lane, big), axis=-1, keepdims=True)
        return mx, idx

    lg = jnp.where(lane < ng, logits, -jnp.inf)
    mg, g_sel = first_max(lg)
    p_g = 1.0 / jnp.sum(jnp.exp(lg - mg), axis=-1, keepdims=True)
    lo = ng + g_sel * epg
    le = jnp.where(jnp.logical_and(lane >= lo, lane < lo + epg), logits, -jnp.inf)
    m1, i1 = first_max(le)
    se = jnp.sum(jnp.exp(le - m1), axis=-1, keepdims=True)
    m2, i2 = first_max(jnp.where(lane == i1, -jnp.inf, le))
    p1 = 1.0 / se
    p2 = jnp.exp(m2 - m1) / se
    w1 = p_g * p1 / (p1 + p2)
    w2 = p_g * p2 / (p1 + p2)

    hot = jnp.logical_or(lane == i1, lane == i2)
    hot_b = jnp.where(hot, 1.0, 0.0).astype(bf16)
    n = x.shape[0]
    strict = (lax.broadcasted_iota(i32, (n, n), 0) > lax.broadcasted_iota(i32, (n, n), 1))
    before = jnp.dot(jnp.where(strict, 1.0, 0.0).astype(bf16), hot_b, preferred_element_type=f32)
    before = before + carry_ref[...]
    r1 = jnp.sum(jnp.where(lane == i1, before, 0.0), axis=-1, keepdims=True)
    r2 = jnp.sum(jnp.where(lane == i2, before, 0.0), axis=-1, keepdims=True)
    carry_ref[...] = carry_ref[...] + jnp.sum(jnp.where(hot, 1.0, 0.0), axis=0, keepdims=True)

    out = jnp.where(lane == 0, (i1 - ng).astype(f32), 0.0)
    out = jnp.where(lane == 1, (i2 - ng).astype(f32), out)
    out = jnp.where(lane == 2, w1, out)
    out = jnp.where(lane == 3, w2, out)
    out = jnp.where(lane == 4, r1, out)
    out = jnp.where(lane == 5, r2, out)
    route_ref[...] = out
    cnt_ref[...] = carry_ref[...]


def _router(x1, g, w_r, b_r):
    n, d = x1.shape
    return pl.pallas_call(
        _router_kernel,
        grid=(n // ROW_BLOCK,),
        in_specs=[pl.BlockSpec((ROW_BLOCK, d), lambda i: (i, 0)),
                  pl.BlockSpec((1, d), lambda i: (0, 0)),
                  pl.BlockSpec((d, LANE), lambda i: (0, 0)),
                  pl.BlockSpec((1, LANE), lambda i: (0, 0))],
        out_specs=[pl.BlockSpec((ROW_BLOCK, LANE), lambda i: (i, 0)),
                   pl.BlockSpec((1, LANE), lambda i: (0, 0))],
        out_shape=[jax.ShapeDtypeStruct((n, LANE), f32), jax.ShapeDtypeStruct((1, LANE), f32)],
        scratch_shapes=[pltpu.VMEM((1, LANE), f32)],
        compiler_params=_params(("arbitrary",)),
        name="moe_router",
    )(x1, g.reshape(1, d), w_r, b_r)


def _row_copy(src_hbm, dst_vmem, sem, src_row, dst_row):
    return pltpu.make_async_copy(src_hbm.at[pl.ds(src_row, 1), :], dst_vmem.at[pl.ds(dst_row, 1), :], sem)


def _dispatch_kernel(tok_ref, x_hbm, g_ref, o_ref, buf_ref, sem):
    i = pl.program_id(0)
    n = pl.num_programs(0)

    def for_real_rows(blk, fn):
        def body(r8, c):
            for u in range(DMA_ISSUE_UNROLL):
                r = r8 * DMA_ISSUE_UNROLL + u
                tok = tok_ref[blk * MOE_BLOCK + r]

                @pl.when(tok >= 0)
                def _(tok=tok, r=r, u=u):
                    fn(tok, r, u)
            return c
        lax.fori_loop(0, MOE_BLOCK // DMA_ISSUE_UNROLL, body, 0)

    def start(blk, slot):
        for_real_rows(blk, lambda tok, r, u: _row_copy(x_hbm, buf_ref.at[slot], sem.at[slot], tok, r).start(
            priority=u % 2))

    def wait(blk, slot):
        for_real_rows(blk, lambda tok, r, u: _row_copy(x_hbm, buf_ref.at[slot], sem.at[slot], tok, r).wait())

    @pl.when(i == 0)
    def _():
        buf_ref[...] = jnp.zeros_like(buf_ref)
        start(0, 0)

    slot = i % 2

    @pl.when(i + 1 < n)
    def _():
        start(i + 1, 1 - slot)

    wait(i, slot)
    x = buf_ref[slot]
    ms = jnp.mean(x * x, axis=-1, keepdims=True)
    o_ref[...] = (x * lax.rsqrt(ms + NORM_EPS) * g_ref[...]).astype(o_ref.dtype)


def _dispatch(row_tok, x1, g):
    n, d = x1.shape
    rows = row_tok.shape[0]
    return pl.pallas_call(
        _dispatch_kernel,
        grid_spec=pltpu.PrefetchScalarGridSpec(
            num_scalar_prefetch=1,
            grid=(rows // MOE_BLOCK,),
            in_specs=[pl.BlockSpec(memory_space=pl.ANY),
                      pl.BlockSpec((1, d), lambda i, tok: (0, 0))],
            out_specs=pl.BlockSpec((MOE_BLOCK, d), lambda i, tok: (i, 0)),
            scratch_shapes=[pltpu.VMEM((2, MOE_BLOCK, d), f32), pltpu.SemaphoreType.DMA((2,))]),
        out_shape=jax.ShapeDtypeStruct((rows, d), bf16),
        compiler_params=_params(("arbitrary",)),
        name="moe_dispatch",
    )(row_tok, x1, g.reshape(1, d))


def _expert_ffn_kernel(ie_ref, ir_ref, in_ref, nit_ref, xs_hbm, wg_ref, wu_ref, wd_ref, y_hbm,
                       x_ref, acc_ref, sem_in, sem_out, *, n_ff):
    it = pl.program_id(0)
    ff = pl.program_id(1)
    nblk = in_ref[it]
    row0 = pl.multiple_of(ir_ref[it] * MOE_BLOCK, MOE_BLOCK)
    span = MOE_ITEM_BLOCKS * MOE_BLOCK

    @pl.when(nblk > 0)
    def _():
        @pl.when(ff == 0)
        def _():
            cp = pltpu.make_async_copy(xs_hbm.at[pl.ds(row0, span), :], x_ref, sem_in)
            cp.start()
            acc_ref[...] = jnp.zeros_like(acc_ref)
            cp.wait()

        x = x_ref[...]
        gate = jnp.dot(x, wg_ref[0].astype(bf16), preferred_element_type=f32)
        up = jnp.dot(x, wu_ref[0].astype(bf16), preferred_element_type=f32)
        act = (gate * _sigmoid(gate) * up).astype(bf16)
        acc_ref[...] += jnp.dot(act, wd_ref[0].astype(bf16), preferred_element_type=f32)

        @pl.when(ff == n_ff - 1)
        def _():
            for sb in range(MOE_ITEM_BLOCKS):
                @pl.when(sb < nblk)
                def _(sb=sb):
                    rs = pl.ds(sb * MOE_BLOCK, MOE_BLOCK)
                    cp = pltpu.make_async_copy(acc_ref.at[rs, :],
                                               y_hbm.at[pl.ds(row0 + sb * MOE_BLOCK, MOE_BLOCK), :], sem_out)
                    cp.start()
                    cp.wait()

    @pl.when(jnp.logical_and(it == pl.num_programs(0) - 1, ff == n_ff - 1))
    def _():
        acc_ref[0:MOE_BLOCK, :] = jnp.zeros((MOE_BLOCK, acc_ref.shape[1]), f32)

        def fill(blk, c):
            cp = pltpu.make_async_copy(acc_ref.at[pl.ds(0, MOE_BLOCK), :],
                                       y_hbm.at[pl.ds(pl.multiple_of(blk * MOE_BLOCK, MOE_BLOCK), MOE_BLOCK), :],
                                       sem_out)
            cp.start()
            cp.wait()
            return c

        lax.fori_loop(nit_ref[1], y_hbm.shape[0] // MOE_BLOCK, fill, 0)


def _expert_ffn(item_e, item_r, item_n, n_items, xs, w_gate, w_up, w_down, max_items):
    rows, d = xs.shape
    dff = w_gate.shape[3]
    tf = _pick_tile(dff, MOE_FF_TILE, LANE)
    n_ff = dff // tf
    span = MOE_ITEM_BLOCKS * MOE_BLOCK

    def f_eff(i, f, nit):
        return jnp.where(i < nit[0], f, n_ff - 1)

    return pl.pallas_call(
        functools.partial(_expert_ffn_kernel, n_ff=n_ff),
        grid_spec=pltpu.PrefetchScalarGridSpec(
            num_scalar_prefetch=4,
            grid=(max_items, n_ff),
            in_specs=[pl.BlockSpec(memory_space=pl.ANY),
                      pl.BlockSpec((None, 1, d, tf), lambda i, f, ie, ir, inn, nit: (0, ie[i], 0, f_eff(i, f, nit))),
                      pl.BlockSpec((None, 1, d, tf), lambda i, f, ie, ir, inn, nit: (0, ie[i], 0, f_eff(i, f, nit))),
                      pl.BlockSpec((None, 1, tf, d), lambda i, f, ie, ir, inn, nit: (0, ie[i], f_eff(i, f, nit), 0))],
            out_specs=pl.BlockSpec(memory_space=pl.ANY),
            scratch_shapes=[pltpu.VMEM((span, d), bf16), pltpu.VMEM((span, d), f32),
                            pltpu.SemaphoreType.DMA(()), pltpu.SemaphoreType.DMA(())]),
        out_shape=jax.ShapeDtypeStruct((rows, d), f32),
        compiler_params=_params(("arbitrary", "arbitrary")),
        name="moe_expert_ffn",
    )(item_e, item_r, item_n, n_items, xs, w_gate, w_up, w_down)


def _combine_kernel(d1_ref, d2_ref, y_hbm, x_ref, route_ref, g_ref, o_ref, buf_ref, sem):
    i = pl.program_id(0)
    n = pl.num_programs(0)

    def start(blk, slot):
        def body(r8, c):
            for u in range(DMA_ISSUE_UNROLL):
                r = r8 * DMA_ISSUE_UNROLL + u
                _row_copy(y_hbm, buf_ref.at[slot, 0], sem.at[slot], d1_ref[blk * ROW_BLOCK + r], r).start(priority=0)
                _row_copy(y_hbm, buf_ref.at[slot, 1], sem.at[slot], d2_ref[blk * ROW_BLOCK + r], r).start(priority=1)
            return c
        lax.fori_loop(0, ROW_BLOCK // DMA_ISSUE_UNROLL, body, 0)

    def wait(slot):
        for r in range(ROW_BLOCK):
            _row_copy(y_hbm, buf_ref.at[slot, 0], sem.at[slot], 0, r).wait()
            _row_copy(y_hbm, buf_ref.at[slot, 1], sem.at[slot], 0, r).wait()

    @pl.when(i == 0)
    def _():
        start(0, 0)

    slot = i % 2

    @pl.when(i + 1 < n)
    def _():
        start(i + 1, 1 - slot)

    wait(slot)
    route = route_ref[...]
    x = x_ref[...] + route[:, 2:3] * buf_ref[slot, 0] + route[:, 3:4] * buf_ref[slot, 1]
    ms = jnp.mean(x * x, axis=-1, keepdims=True)
    o_ref[...] = x * lax.rsqrt(ms + NORM_EPS) * g_ref[...]


def _combine(d1, d2, yb, x1, route, g):
    n, d = x1.shape
    return pl.pallas_call(
        _combine_kernel,
        grid_spec=pltpu.PrefetchScalarGridSpec(
            num_scalar_prefetch=2,
            grid=(n // ROW_BLOCK,),
            in_specs=[pl.BlockSpec(memory_space=pl.ANY),
                      pl.BlockSpec((ROW_BLOCK, d), lambda i, a, b: (i, 0)),
                      pl.BlockSpec((ROW_BLOCK, LANE), lambda i, a, b: (i, 0)),
                      pl.BlockSpec((1, d), lambda i, a, b: (0, 0))],
            out_specs=pl.BlockSpec((ROW_BLOCK, d), lambda i, a, b: (i, 0)),
            scratch_shapes=[pltpu.VMEM((2, 2, ROW_BLOCK, d), f32), pltpu.SemaphoreType.DMA((2,))]),
        out_shape=jax.ShapeDtypeStruct((n, d), f32),
        compiler_params=_params(("arbitrary",)),
        name="moe_combine",
    )(d1, d2, yb, x1, route, g.reshape(1, d))


def _moe(x1, norm_ffn, w_rg, b_rg, w_re, b_re, w_gate, w_up, w_down, norm_final):
    n, d = x1.shape
    n_exp = w_gate.shape[1]
    ng = w_rg.shape[1]
    w_r = jnp.zeros((d, LANE), f32).at[:, :ng].set(w_rg).at[:, ng:ng + n_exp].set(w_re)
    b_r = jnp.zeros((1, LANE), f32).at[0, :ng].set(b_rg).at[0, ng:ng + n_exp].set(b_re)
    route, cnt = _router(x1, norm_ffn, w_r, b_r)

    e1 = route[:, 0].astype(i32)
    e2 = route[:, 1].astype(i32)
    counts = cnt[0, ng:ng + n_exp].astype(i32)
    nblk_e = (counts + MOE_BLOCK - 1) // MOE_BLOCK
    blk_start = jnp.cumsum(nblk_e) - nblk_e
    def first_row(e):
        hot = e[:, None] == jnp.arange(n_exp, dtype=i32)[None, :]
        return jnp.sum(jnp.where(hot, blk_start[None, :], 0), axis=1) * MOE_BLOCK

    d1 = first_row(e1) + route[:, 4].astype(i32)
    d2 = first_row(e2) + route[:, 5].astype(i32)
    max_blocks = -(-2 * n // MOE_BLOCK) + n_exp
    rows = (max_blocks + MOE_ITEM_BLOCKS) * MOE_BLOCK
    tok = jnp.arange(n, dtype=i32)
    row_tok = jnp.full((rows,), -1, i32).at[jnp.concatenate([d1, d2])].set(
        jnp.concatenate([tok, tok]), unique_indices=True)
    items_e = (nblk_e + MOE_ITEM_BLOCKS - 1) // MOE_ITEM_BLOCKS
    item_end = jnp.cumsum(items_e)
    n_items = item_end[-1]
    max_items = -(-max_blocks // MOE_ITEM_BLOCKS) + n_exp
    it = jnp.arange(max_items, dtype=i32)
    it_c = jnp.minimum(it, n_items - 1)
    item_e = jnp.minimum(jnp.searchsorted(item_end, it_c, side="right"), n_exp - 1).astype(i32)
    local = it_c - (item_end - items_e)[item_e]
    item_r = (blk_start[item_e] + local * MOE_ITEM_BLOCKS).astype(i32)
    item_n = jnp.where(it < n_items, jnp.minimum(MOE_ITEM_BLOCKS, nblk_e[item_e] - local * MOE_ITEM_BLOCKS), 0)

    xs = _dispatch(row_tok, x1, norm_ffn)
    meta = jnp.stack([n_items, jnp.sum(nblk_e)]).astype(i32)
    yb = _expert_ffn(item_e, item_r, item_n.astype(i32), meta, xs, w_gate, w_up, w_down, max_items)
    return _combine(d1, d2, yb, x1, route, norm_final)


def kernel(x_prompt, x_sample, cache_k, cache_v, cache_idx_k, state_hgrn, page_table, norm_mix, w_in,
           lb_logits, norm_hgrn_out, w_out, norm_ffn, router_group, router_group_bias, router_expert,
           router_expert_bias, expert_w_gate, expert_w_up, expert_w_down, norm_final):
    n_b, seq, d = x_prompt.shape
    n_s, n_tok, _ = x_sample.shape
    depth = norm_mix.shape[0]
    assert depth == 1
    kd = A_HEADS * A_EXPAND
    dh = d // B_HEADS
    kvd = B_KV_HEADS * dh
    group = B_HEADS // B_KV_HEADS
    np_rows = n_b * seq
    ns_rows = n_s * n_tok
    n_all = np_rows + ns_rows
    n_pad = -(-n_all // ROW_BLOCK) * ROW_BLOCK
    assert np_rows % ROW_BLOCK == 0

    lower = jnp.cumsum(jax.nn.softmax(lb_logits.astype(f32), axis=0), axis=0)[0]
    x_all = jnp.concatenate([x_prompt.reshape(np_rows, d), x_sample.reshape(ns_rows, d),
                             jnp.zeros((n_pad - n_all, d), f32)], axis=0)

    w = w_in[0]
    o_bq = 4 * kd
    o_bk = o_bq + d
    o_bqi = o_bk + 2 * kvd
    o_sm = o_bqi + IDX_HEADS * IDX_DIM
    o_g = o_sm + IDX_DIM + IDX_HEADS
    small_w = 2 * LANE
    h = _rmsnorm_bf16(x_all, norm_mix[0])
    wb = w.astype(bf16)
    z_a = _matmul(h, wb, name="proj_hgrn", col0=0, n=o_bq)
    z_bq = _matmul(h, wb, name="proj_q", col0=o_bq, n=o_bk - o_bq)
    z_kv = _matmul(h, wb, name="proj_kv", col0=o_bk, n=o_bqi - o_bk)
    z_qi = _matmul_heads(h, wb, "proj_qidx", o_bqi, o_sm - o_bqi)
    w_small = jnp.zeros((d, small_w), bf16).at[:, :o_g - o_sm].set(wb[:, o_sm:o_g])
    z_sm = _matmul(h, w_small, name="proj_small")
    z_g = _matmul(h, wb[:, o_g:], name="proj_gates")

    rb = min(HGRN_ROWS, seq)
    oa_p, s_p = _hgrn(z_a, lower, norm_hgrn_out[0], n_b, seq, rb, min(HGRN_CHUNK, rb), rb,
                      min(HGRN_PROMPT_HEADS, A_HEADS))
    sp = HGRN_SAMPLE_ROWS
    zs = z_a[np_rows:n_all].reshape(n_s, n_tok, 4 * kd)
    zs = jnp.concatenate([zs, jnp.zeros((n_s, sp - n_tok, 4 * kd), f32)], axis=1).reshape(n_s * sp, 4 * kd)
    oa_sp, s_s = _hgrn(zs, lower, norm_hgrn_out[0], n_s, sp, sp, sp, n_tok, min(HGRN_SAMPLE_HEADS, A_HEADS),
                       s0=state_hgrn[0])
    oa_s = oa_sp.reshape(n_s, sp, d)[:, :n_tok].reshape(ns_rows, d)

    ob_p = _dsa_prompt(z_qi, z_sm, z_bq, z_kv, n_b, seq)
    sm_s = z_sm[np_rows:n_all]
    w_s = sm_s[:, IDX_DIM:IDX_DIM + IDX_HEADS].reshape(n_s, n_tok, IDX_HEADS) * (IDX_HEADS ** -0.5 * IDX_DIM ** -0.5)
    eye = jnp.eye(8, n_tok, dtype=f32)
    wsel = (eye[None, :, :, None] * w_s[:, None, :, :]).reshape(n_s, 8, n_tok * IDX_HEADS)
    qi_s = z_qi[np_rows // ROW_BLOCK:].transpose(0, 2, 1, 3).reshape(n_pad - np_rows, IDX_HEADS, IDX_DIM)
    qi_s = qi_s[:ns_rows].reshape(n_s, n_tok * IDX_HEADS, IDX_DIM)
    q_s = z_bq[np_rows:n_all].reshape(n_s, n_tok * B_HEADS, dh)

    def pad_rows_to_lane(a):
        return jnp.concatenate([a, jnp.zeros((n_s, LANE - n_tok) + a.shape[2:], f32)], axis=1)

    def new_kv(a):
        return pad_rows_to_lane(a.reshape(n_s, n_tok, B_KV_HEADS, dh)).reshape(n_s, LANE * B_KV_HEADS, dh)

    kv_s = z_kv[np_rows:n_all]
    ob_s = _dsa_sample(qi_s, wsel, q_s, pad_rows_to_lane(sm_s[:, :IDX_DIM].reshape(n_s, n_tok, IDX_DIM)),
                       new_kv(kv_s[:, :kvd]), new_kv(kv_s[:, kvd:]),
                       cache_k, cache_v, cache_idx_k, page_table, n_tok)
    ob_s = ob_s.reshape(ns_rows, d)

    pad_rows = jnp.zeros((n_pad - n_all, d), f32)
    m = _merge(z_g, oa_p, ob_p, jnp.concatenate([oa_s, pad_rows], axis=0), jnp.concatenate([ob_s, pad_rows], axis=0))
    x1 = _matmul(m, w_out[0].astype(bf16), res=x_all, name="out_proj")

    y = _moe(x1, norm_ffn[0], router_group[0], router_group_bias[0], router_expert[0], router_expert_bias[0],
             expert_w_gate, expert_w_up, expert_w_down, norm_final)

    kp = z_kv[:np_rows]
    ks = z_kv[np_rows:n_all]
    return (y[:np_rows].reshape(n_b, seq, d),
            y[np_rows:n_all].reshape(n_s, n_tok, d),
            kp[:, :kvd].reshape(1, n_b, seq, B_KV_HEADS, dh),
            kp[:, kvd:].reshape(1, n_b, seq, B_KV_HEADS, dh),
            z_sm[:np_rows, :IDX_DIM].reshape(1, n_b, seq, IDX_DIM),
            s_p[None],
            ks[:, :kvd].reshape(1, n_s, n_tok, B_KV_HEADS, dh),
            ks[:, kvd:].reshape(1, n_s, n_tok, B_KV_HEADS, dh),
            z_sm[np_rows:n_all, :IDX_DIM].reshape(1, n_s, n_tok, IDX_DIM),
            s_s[None])
```

```python
import functools
import math

import jax
import jax.numpy as jnp
import numpy as np
from jax import lax
from jax.experimental import pallas as pl
from jax.experimental.pallas import tpu as pltpu

A_HEADS = 32
A_EXPAND = 128
B_HEADS = 32
B_KV_HEADS = 8
IDX_HEADS = 32
IDX_DIM = 128
TOPK_MAX = 256
N_GROUPS = 8
EXPERTS_PER_GROUP = 8
NORM_EPS = 1e-6

LANE = 128
ROW_BLOCK = 128
PROJ_ROW_TILE = 640
PROJ_COL_TILE = 1024
MOE_BLOCK = 128
MOE_ITEM_BLOCKS = 4
MOE_FF_TILE = 256
HGRN_CHUNK = 64
HGRN_ROWS = 512
HGRN_SAMPLE_ROWS = 16
HGRN_SAMPLE_HEADS = 8
HGRN_PROMPT_HEADS = 8
HGRN_SAFE_DECAY = 60.0
SAMPLE_PAGES_PER_STEP = 8
DMA_ISSUE_UNROLL = 8
COUNT_CHAINS = 5
DSA_KEY_STEP = 256
NEG_BIG = -1e30
LOG2E = 1.4426950408889634
VMEM_LIMIT = 56 * 1024 * 1024

f32 = jnp.float32
bf16 = jnp.bfloat16
i32 = jnp.int32
HIGHEST = lax.Precision.HIGHEST
NT_DIMS = (((1,), (1,)), ((), ()))
TN_DIMS = (((0,), (0,)), ((), ()))


def _pick_tile(n, cap, align):
    t = (min(cap, n) // align) * align
    while t > align and n % t:
        t -= align
    assert t > 0 and n % t == 0, (n, cap, align)
    return t


def _params(sem, limit=VMEM_LIMIT):
    return pltpu.CompilerParams(dimension_semantics=sem, vmem_limit_bytes=limit)


def _sigmoid(x):
    return 1.0 / (1.0 + jnp.exp(-x))


def _log2(n):
    s = int(math.log2(n))
    assert 1 << s == n, n
    return s


def _rmsnorm_kernel(x_ref, g_ref, o_ref):
    x = x_ref[...]
    ms = jnp.mean(x * x, axis=-1, keepdims=True)
    o_ref[...] = (x * lax.rsqrt(ms + NORM_EPS) * g_ref[...]).astype(o_ref.dtype)


def _rmsnorm_bf16(x, g):
    n, d = x.shape
    return pl.pallas_call(
        _rmsnorm_kernel,
        grid=(n // ROW_BLOCK,),
        in_specs=[pl.BlockSpec((ROW_BLOCK, d), lambda i: (i, 0)),
                  pl.BlockSpec((1, d), lambda i: (0, 0))],
        out_specs=pl.BlockSpec((ROW_BLOCK, d), lambda i: (i, 0)),
        out_shape=jax.ShapeDtypeStruct((n, d), bf16),
        compiler_params=_params(("parallel",)),
        name="rmsnorm_bf16",
    )(x, g.reshape(1, d))


def _mm_kernel(a_ref, b_ref, o_ref):
    o_ref[...] = jnp.dot(a_ref[...], b_ref[...], preferred_element_type=f32)


def _mm_res_kernel(a_ref, b_ref, r_ref, o_ref):
    o_ref[...] = r_ref[...] + jnp.dot(a_ref[...], b_ref[...], preferred_element_type=f32)


def _matmul(a, b, res=None, name="matmul", col0=0, n=None):
    m, k = a.shape
    n = b.shape[1] if n is None else n
    tm = _pick_tile(m, PROJ_ROW_TILE, ROW_BLOCK)
    tn = _pick_tile(math.gcd(n, col0) if col0 else n, PROJ_COL_TILE, LANE)
    cb = col0 // tn
    in_specs = [pl.BlockSpec((tm, k), lambda j, i: (i, 0)),
                pl.BlockSpec((k, tn), lambda j, i: (0, cb + j))]
    args = [a, b]
    kern = _mm_kernel
    if res is not None:
        in_specs.append(pl.BlockSpec((tm, tn), lambda j, i: (i, j)))
        args.append(res)
        kern = _mm_res_kernel
    return pl.pallas_call(
        kern,
        grid=(n // tn, m // tm),
        in_specs=in_specs,
        out_specs=pl.BlockSpec((tm, tn), lambda j, i: (i, j)),
        out_shape=jax.ShapeDtypeStruct((m, n), f32),
        compiler_params=_params(("parallel", "parallel")),
        name=name,
    )(*args)


def _mm_heads_kernel(a_ref, b_ref, o_ref):
    acc = jnp.dot(a_ref[...], b_ref[...], preferred_element_type=f32)
    for r in range(o_ref.shape[0]):
        for hh in range(o_ref.shape[1]):
            o_ref[r, hh] = acc[r * ROW_BLOCK:(r + 1) * ROW_BLOCK, hh * LANE:(hh + 1) * LANE]


def _matmul_heads(a, b, name, col0, n):
    m, k = a.shape
    tm = _pick_tile(m, PROJ_ROW_TILE, ROW_BLOCK)
    tn = _pick_tile(math.gcd(n, col0) if col0 else n, PROJ_COL_TILE, LANE)
    cb = col0 // tn
    return pl.pallas_call(
        _mm_heads_kernel,
        grid=(n // tn, m // tm),
        in_specs=[pl.BlockSpec((tm, k), lambda j, i: (i, 0)),
                  pl.BlockSpec((k, tn), lambda j, i: (0, cb + j))],
        out_specs=pl.BlockSpec((tm // ROW_BLOCK, tn // LANE, ROW_BLOCK, LANE), lambda j, i: (i, j, 0, 0)),
        out_shape=jax.ShapeDtypeStruct((m // ROW_BLOCK, n // LANE, ROW_BLOCK, LANE), f32),
        compiler_params=_params(("parallel", "parallel")),
        name=name,
    )(a, b)


def _hgrn_static(c):
    levels = _log2(c)
    t = np.arange(c)
    cmat = np.zeros((2 * c, c), np.float32)
    cmat[:c] = t[None, :] <= t[:, None]
    cmat[c:] = 1.0
    lmat = np.zeros((levels * c, c), np.float32)
    lv = np.full((c, c), -1, np.int32)
    for l in range(levels):
        blk = c >> l
        half = blk // 2
        mid = (t // blk) * blk + half
        lmat[l * c:(l + 1) * c] = t[None, :] <= (mid - 1)[:, None]
        same = (t[:, None] // blk) == (t[None, :] // blk)
        up = ((t // half) % 2 == 1)[:, None]
        lo = ((t // half) % 2 == 0)[None, :]
        lv[same & up & lo] = l
    lv[t, t] = levels
    return cmat, lmat, lv, levels


def _split3(x):
    hi = x.astype(bf16)
    r = x - hi.astype(f32)
    mid = r.astype(bf16)
    lo = (r - mid.astype(f32)).astype(bf16)
    return jnp.concatenate([hi, mid, lo], axis=1)


def _sum3(y):
    w = y.shape[1] // 3
    return y[:, :w] + y[:, w:2 * w] + y[:, 2 * w:]


def _hgrn_kernel(*refs, rows, chunk, levels, n_valid, heads, has_init):
    if has_init:
        (zq_ref, zf_ref, zi_ref, zg_ref, lb_ref, gain_ref, cmat_ref, lmat_ref, lv_ref, s0_ref,
         oa_ref, sfin_ref, st_ref, a_ref, q_sc, k_sc, b_sc, p_sc) = refs
    else:
        (zq_ref, zf_ref, zi_ref, zg_ref, lb_ref, gain_ref, cmat_ref, lmat_ref, lv_ref,
         oa_ref, sfin_ref, st_ref, a_ref, q_sc, k_sc, b_sc, p_sc) = refs
        s0_ref = None
    tb = pl.program_id(2)
    c = chunk
    nc = rows // c
    dk = A_EXPAND

    @pl.when(tb == 0)
    def _():
        for hh in range(heads):
            if has_init:
                st_ref[hh] = s0_ref[0, hh].T
            else:
                st_ref[hh] = jnp.zeros(st_ref.shape[1:], f32)

    lv = lv_ref[...]
    cmat = cmat_ref[...]
    row_c = lax.broadcasted_iota(i32, (c, dk), 0)

    pre = []
    unsafe = None
    for hh in range(heads):
        ls = slice(hh * dk, (hh + 1) * dk)
        zq = zq_ref[:, ls]
        lb = lb_ref[:, ls]
        f = lb + (1.0 - lb) * _sigmoid(zf_ref[:, ls])
        logf = jnp.log(f)
        k = 1.0 - f
        q = zq * _sigmoid(zq)
        if n_valid < rows:
            live = lax.broadcasted_iota(i32, (rows, dk), 0) < n_valid
            logf = jnp.where(live, logf, 0.0)
            k = jnp.where(live, k, 0.0)
        parts = _split3(logf)
        bs, bls = [], []
        for ci in range(nc):
            y = _sum3(jnp.dot(cmat, parts[ci * c:(ci + 1) * c], preferred_element_type=f32))
            bs.append(y[:c])
            bls.append(y[c:])
        b = jnp.concatenate(bs, axis=0) if nc > 1 else bs[0]
        bl = jnp.concatenate(bls, axis=0) if nc > 1 else bls[0]
        head_unsafe = jnp.min(bl) < -HGRN_SAFE_DECAY
        unsafe = head_unsafe if unsafe is None else jnp.logical_or(unsafe, head_unsafe)
        pre.append(dict(q=q, k=k, b=b, bl=bl, parts=parts, qh_b=(q * jnp.exp(b)).astype(bf16),
                        kd=(k * jnp.exp(bl - b)).astype(bf16), v_b=zi_ref[:, ls].astype(bf16)))

    @pl.when(jnp.logical_not(unsafe))
    def _():
        for hh, hv in enumerate(pre):
            kt = (hv["k"] * jnp.exp(-hv["b"])).astype(bf16)
            for ci in range(nc):
                rs = slice(ci * c, (ci + 1) * c)
                al = lax.dot_general(hv["qh_b"][rs], kt[rs], NT_DIMS, preferred_element_type=f32)
                a_ref[hh * nc + ci] = jnp.where(lv >= 0, al, 0.0)

    @pl.when(unsafe)
    def _():
        for hh, hv in enumerate(pre):
            hr = slice(hh * rows, (hh + 1) * rows)
            q_sc[hr, :] = hv["q"]
            k_sc[hr, :] = hv["k"]
            b_sc[hr, :] = hv["b"]
            p_sc[hr, :] = hv["parts"]

        def safe_chunk(ci, carry):
            r0 = pl.multiple_of(ci * c, c)
            q_c = q_sc[pl.ds(r0, c), :]
            k_c = k_sc[pl.ds(r0, c), :]
            b_c = b_sc[pl.ds(r0, c), :]
            refs_b = _sum3(jnp.dot(lmat_ref[...], p_sc[pl.ds(r0, c), :], preferred_element_type=f32))
            a = jnp.where(lv == levels, jnp.sum(q_c * k_c, axis=-1, keepdims=True), 0.0)
            for l in range(levels):
                e = jnp.exp(-jnp.abs(b_c - refs_b[l * c:(l + 1) * c]))
                upper = ((row_c >> (levels - l - 1)) & 1) == 1
                qt = jnp.where(upper, q_c * e, 0.0).astype(bf16)
                kt = jnp.where(upper, 0.0, k_c * e).astype(bf16)
                al = lax.dot_general(qt, kt, NT_DIMS, preferred_element_type=f32)
                a = jnp.where(lv == l, al, a)
            a_ref[ci] = a
            return carry

        lax.fori_loop(0, heads * nc, safe_chunk, 0)

    for hh, hv in enumerate(pre):
        ls = slice(hh * dk, (hh + 1) * dk)
        gain = gain_ref[:, ls]
        zg = zg_ref[:, ls]
        st = st_ref[hh]
        for ci in range(nc):
            rs = slice(ci * c, (ci + 1) * c)
            o = (jnp.dot(a_ref[hh * nc + ci].astype(bf16), hv["v_b"][rs], preferred_element_type=f32)
                 + lax.dot_general(hv["qh_b"][rs], st.astype(bf16), NT_DIMS, preferred_element_type=f32))
            ms = jnp.mean(o * o, axis=-1, keepdims=True)
            zg_c = zg[rs]
            oa_ref[rs, ls] = o * lax.rsqrt(ms + NORM_EPS) * gain * (zg_c * _sigmoid(zg_c))
            st = (st * jnp.exp(hv["bl"][ci * c:ci * c + 1])
                  + lax.dot_general(hv["v_b"][rs], hv["kd"][rs], TN_DIMS, preferred_element_type=f32))
        st_ref[hh] = st

    @pl.when(tb == pl.num_programs(2) - 1)
    def _():
        for hh in range(heads):
            sfin_ref[0, hh] = st_ref[hh].T


def _hgrn(z_a, lb, gain, n_seq, seq_rows, row_block, chunk, n_valid, heads, s0=None):
    kd = A_HEADS * A_EXPAND
    dv = LANE
    assert z_a.shape[1] == 2 * kd + 2 * A_HEADS * dv and A_EXPAND == LANE and A_HEADS % heads == 0
    cmat, lmat, lv, levels = _hgrn_static(chunk)
    nrb = seq_rows // row_block
    hb = A_HEADS // heads
    wd = heads * LANE

    def zspec(seg):
        return pl.BlockSpec((row_block, wd), lambda b, h, t, seg=seg: (b * nrb + t, seg * hb + h))

    const = lambda b, h, t: (0, 0)
    in_specs = [zspec(0), zspec(1), zspec(2), zspec(3),
                pl.BlockSpec((1, wd), lambda b, h, t: (0, h)),
                pl.BlockSpec((1, wd), lambda b, h, t: (0, h)),
                pl.BlockSpec(cmat.shape, const), pl.BlockSpec(lmat.shape, const), pl.BlockSpec(lv.shape, const)]
    args = [z_a, z_a, z_a, z_a, lb.reshape(1, kd), gain.reshape(1, -1),
            jnp.asarray(cmat, bf16), jnp.asarray(lmat, bf16), jnp.asarray(lv)]
    if s0 is not None:
        in_specs.append(pl.BlockSpec((1, heads, A_EXPAND, dv), lambda b, h, t: (b, h, 0, 0)))
        args.append(s0)
    kern = functools.partial(_hgrn_kernel, rows=row_block, chunk=chunk, levels=levels, n_valid=n_valid,
                             heads=heads, has_init=s0 is not None)
    return pl.pallas_call(
        kern,
        grid=(n_seq, hb, nrb),
        in_specs=in_specs,
        out_specs=[pl.BlockSpec((row_block, wd), lambda b, h, t: (b * nrb + t, h)),
                   pl.BlockSpec((1, heads, A_EXPAND, dv), lambda b, h, t: (b, h, 0, 0))],
        out_shape=[jax.ShapeDtypeStruct((n_seq * seq_rows, A_HEADS * dv), f32),
                   jax.ShapeDtypeStruct((n_seq, A_HEADS, A_EXPAND, dv), f32)],
        scratch_shapes=[pltpu.VMEM((heads, dv, A_EXPAND), f32),
                        pltpu.VMEM((heads * row_block // chunk, chunk, chunk), f32),
                        pltpu.VMEM((heads * row_block, LANE), f32), pltpu.VMEM((heads * row_block, LANE), f32),
                        pltpu.VMEM((heads * row_block, LANE), f32),
                        pltpu.VMEM((heads * row_block, 3 * LANE), bf16)],
        compiler_params=_params(("parallel", "parallel", "arbitrary")),
        name="hgrn2_prompt" if s0 is None else "hgrn2_sample",
    )(*args)


def _sort_key(x):
    b = pltpu.bitcast(x + 0.0, i32)
    return b ^ ((b >> 31) & 0x7FFFFFFF)


def _kth_largest_key(key_ref, k, axis):
    shape = (1, key_ref.shape[1]) if axis == 0 else (key_ref.shape[0], 1)
    int_min = jnp.int32(-2 ** 31)

    n = key_ref.shape[axis]
    parts = max(p for p in range(1, COUNT_CHAINS + 1) if (n // LANE) % p == 0)
    seg = n // parts

    def step(i, lo):
        cand = lo ^ lax.shift_left(jnp.int32(1), 31 - i)
        cnt = None
        for pi in range(parts):
            sl = slice(pi * seg, (pi + 1) * seg)
            keys = key_ref[sl, :] if axis == 0 else key_ref[:, sl]
            c = jnp.sum((keys >= cand).astype(i32), axis=axis, keepdims=True)
            cnt = c if cnt is None else cnt + c
        return jnp.where(cnt >= k, cand, lo)

    return lax.fori_loop(0, 32, step, jnp.full(shape, int_min, i32))


def _dsa_prompt_kernel(qi_ref, smq_ref, smk_ref, q_ref, k_ref, v_ref, o_ref, mask_ref, key_ref, qs_ref,
                       *, kext, topk, qblk, group, qb0):
    qb = qb0 + pl.program_id(1)
    g = pl.program_id(2)
    n_tiles = kext // qblk

    def causal(r0):
        kpos = r0 + lax.broadcasted_iota(i32, (qblk, qblk), 0)
        qpos = qb * qblk + lax.broadcasted_iota(i32, (qblk, qblk), 1)
        return kpos <= qpos

    @pl.when(g == 0)
    def _():
        qs_ref[...] = qi_ref[0].reshape(IDX_HEADS * qblk, IDX_DIM).T.astype(bf16)
        w_t = smq_ref[...].T * (IDX_HEADS ** -0.5 * IDX_DIM ** -0.5)

        def score_tile(ti, carry):
            r0 = pl.multiple_of(ti * qblk, qblk)
            kid = smk_ref[pl.ds(r0, qblk), 0:IDX_DIM].astype(bf16)
            s_all = jnp.dot(kid, qs_ref[...], preferred_element_type=f32)
            acc = jnp.zeros((qblk, qblk), f32)
            for h in range(IDX_HEADS):
                acc = acc + (jnp.maximum(s_all[:, h * qblk:(h + 1) * qblk], 0.0)
                             * w_t[IDX_DIM + h:IDX_DIM + h + 1, :])
            key_ref[pl.ds(r0, qblk), :] = _sort_key(jnp.where(causal(r0), acc, -jnp.inf))
            return carry

        lax.fori_loop(0, n_tiles, score_tile, 0)
        thr = _kth_largest_key(key_ref, topk, axis=0)
        keys = key_ref[...]
        kpos = lax.broadcasted_iota(i32, (kext, qblk), 0)
        qpos = qb * qblk + lax.broadcasted_iota(i32, (kext, qblk), 1)
        mask_ref[...] = jnp.where(jnp.logical_and(kpos <= qpos, keys >= thr), 0.0, -jnp.inf)
        need = topk - jnp.sum((keys > thr).astype(i32), axis=0, keepdims=True)
        n_tie = jnp.sum((keys == thr).astype(i32), axis=0, keepdims=True)

        @pl.when(jnp.max(n_tie - need) > 0)
        def _():
            strict = (lax.broadcasted_iota(i32, (qblk, qblk), 0) > lax.broadcasted_iota(i32, (qblk, qblk), 1))
            strict = jnp.where(strict, 1.0, 0.0).astype(bf16)
            need_f = need.astype(f32)

            def tile(ti, seen):
                r0 = pl.multiple_of(ti * qblk, qblk)
                kt = key_ref[pl.ds(r0, qblk), :]
                tie = kt == thr
                tie_f = jnp.where(tie, 1.0, 0.0)
                before = seen + jnp.dot(strict, tie_f.astype(bf16), preferred_element_type=f32)
                keep = jnp.logical_or(kt > thr, jnp.logical_and(tie, before < need_f))
                keep = jnp.logical_and(keep, causal(r0))
                mask_ref[pl.ds(r0, qblk), :] = jnp.where(keep, 0.0, -jnp.inf)
                return seen + jnp.sum(tie_f, axis=0, keepdims=True)

            lax.fori_loop(0, n_tiles, tile, jnp.zeros((1, qblk), f32))

    kk = k_ref[0:kext, :].astype(bf16)
    vt = v_ref[0:kext, :].T.astype(bf16)
    dh = kk.shape[1]
    for hq in range(group):
        qh = (q_ref[:, hq * dh:(hq + 1) * dh] * (dh ** -0.5 * LOG2E)).astype(bf16)
        s = lax.dot_general(kk, qh, NT_DIMS, preferred_element_type=f32) + mask_ref[...]
        m = jnp.max(s, axis=0, keepdims=True)
        p = jnp.exp2(s - m)
        den = jnp.sum(p, axis=0, keepdims=True)
        o_t = jnp.dot(vt, p.astype(bf16), preferred_element_type=f32) / den
        o_ref[:, hq * dh:(hq + 1) * dh] = o_t.T


def _dsa_prompt(bqi, small, bq, bkv, n_batch, seq):
    dh = bq.shape[1] // B_HEADS
    group = B_HEADS // B_KV_HEADS
    qblk = LANE
    nqb = seq // qblk
    topk = min(TOPK_MAX, seq // 4)
    parts = []
    qb0 = 0
    count = max(DSA_KEY_STEP // qblk, 1)
    assert nqb % count == 0
    while qb0 < nqb:
        kext = (qb0 + count) * qblk
        kern = functools.partial(_dsa_prompt_kernel, kext=kext, topk=topk, qblk=qblk, group=group, qb0=qb0)
        part = pl.pallas_call(
            kern,
            grid=(n_batch, count, B_KV_HEADS),
            in_specs=[pl.BlockSpec((1, IDX_HEADS, qblk, IDX_DIM), lambda b, j, g, q0=qb0: (b * nqb + q0 + j, 0, 0, 0)),
                      pl.BlockSpec((qblk, small.shape[1]), lambda b, j, g, q0=qb0: (b * nqb + q0 + j, 0)),
                      pl.BlockSpec((seq, small.shape[1]), lambda b, j, g: (b, 0)),
                      pl.BlockSpec((qblk, group * dh), lambda b, j, g, q0=qb0: (b * nqb + q0 + j, g)),
                      pl.BlockSpec((seq, dh), lambda b, j, g: (b, g)),
                      pl.BlockSpec((seq, dh), lambda b, j, g: (b, B_KV_HEADS + g))],
            out_specs=pl.BlockSpec((qblk, group * dh), lambda b, j, g, cnt=count: (b * cnt + j, g)),
            out_shape=jax.ShapeDtypeStruct((n_batch * count * qblk, B_HEADS * dh), f32),
            scratch_shapes=[pltpu.VMEM((kext, qblk), f32), pltpu.VMEM((kext, qblk), i32),
                            pltpu.VMEM((IDX_DIM, IDX_HEADS * qblk), bf16)],
            compiler_params=_params(("parallel", "parallel", "arbitrary")),
            name=f"dsa_prompt_k{kext}",
        )(bqi, small, small, bq, bkv, bkv)
        parts.append(part.reshape(n_batch, count * qblk, B_HEADS * dh))
        qb0 += count
    return jnp.concatenate(parts, axis=1).reshape(n_batch * seq, B_HEADS * dh)


def _dsa_sample_select_kernel(pt_ref, qi_ref, wsel_ref, knew_ref, cik_hbm, mask_ref,
                              slab_ref, sc_ref, key_ref, sel_ref, sem, *, n_pages, page, topk, n_tok, kvh):
    b = pl.program_id(0)
    n_seq = pl.num_programs(0)
    past = n_pages * page

    def page_copy(seq, slot, j):
        return pltpu.make_async_copy(cik_hbm.at[0, pt_ref[seq, j]],
                                     slab_ref.at[slot, pl.ds(pl.multiple_of(j * page, page), page), :],
                                     sem.at[slot])

    def start(seq, slot):
        lax.fori_loop(0, n_pages, lambda j, c: (page_copy(seq, slot, j).start(), c)[1], 0)

    def wait(seq, slot):
        lax.fori_loop(0, n_pages, lambda j, c: (page_copy(seq, slot, j).wait(), c)[1], 0)

    @pl.when(b == 0)
    def _():
        start(0, 0)

    slot = b % 2

    @pl.when(b + 1 < n_seq)
    def _():
        start(b + 1, 1 - slot)

    wait(b, slot)
    qi = qi_ref[0].astype(bf16)
    wsel = wsel_ref[0]

    def tile_scores(keys):
        s = lax.dot_general(qi, keys.astype(bf16), NT_DIMS, preferred_element_type=f32)
        return jnp.dot(wsel, jnp.maximum(s, 0.0), precision=HIGHEST, preferred_element_type=f32)

    sc_ref[:, 0:past] = tile_scores(slab_ref[slot])
    new = tile_scores(knew_ref[0])
    r = lax.broadcasted_iota(i32, new.shape, 0)
    cidx = lax.broadcasted_iota(i32, new.shape, 1)
    sc_ref[:, past:past + LANE] = jnp.where(jnp.logical_and(cidx <= r, cidx < n_tok), new, -jnp.inf)
    sc = sc_ref[...]
    key_ref[...] = _sort_key(sc)
    thr = _kth_largest_key(key_ref, topk, axis=1)
    keys = key_ref[...]
    sel = jnp.logical_and(keys >= thr, sc > -jnp.inf)
    sel_ref[...] = jnp.where(sel, 1.0, 0.0)
    need = topk - jnp.sum((keys > thr).astype(i32), axis=1, keepdims=True)
    n_tie = jnp.sum((keys == thr).astype(i32), axis=1, keepdims=True)

    @pl.when(jnp.max(n_tie - need) > 0)
    def _():
        strict = (lax.broadcasted_iota(i32, (LANE, LANE), 0) < lax.broadcasted_iota(i32, (LANE, LANE), 1))
        strict = jnp.where(strict, 1.0, 0.0).astype(bf16)
        need_f = need.astype(f32)

        def tile(ti, seen):
            c1 = pl.multiple_of(ti * LANE, LANE)
            kt = key_ref[:, pl.ds(c1, LANE)]
            tie = kt == thr
            tie_f = jnp.where(tie, 1.0, 0.0)
            before = seen + jnp.dot(tie_f.astype(bf16), strict, preferred_element_type=f32)
            keep = jnp.logical_or(kt > thr, jnp.logical_and(tie, before < need_f))
            keep = jnp.logical_and(keep, sc_ref[:, pl.ds(c1, LANE)] > -jnp.inf)
            sel_ref[:, pl.ds(c1, LANE)] = jnp.where(keep, 1.0, 0.0)
            return seen + jnp.sum(tie_f, axis=1, keepdims=True)

        lax.fori_loop(0, sc_ref.shape[1] // LANE, tile, jnp.zeros((8, 1), f32))

    lane = lax.broadcasted_iota(i32, (8, LANE), 1)

    def expand(ti, carry):
        c1 = pl.multiple_of(ti * LANE, LANE)
        hit = sel_ref[:, pl.ds(c1, LANE)]
        for j in range(kvh):
            src = j * (LANE // kvh) + (lane >> _log2(kvh))
            piece = jnp.take_along_axis(hit, src, axis=1)
            c2 = pl.multiple_of(ti * (LANE * kvh) + j * LANE, LANE)
            mask_ref[0, :, pl.ds(c2, LANE)] = jnp.where(piece > 0.5, 0.0, NEG_BIG)
        return carry

    lax.fori_loop(0, sc_ref.shape[1] // LANE, expand, 0)


def _dsa_sample_attn_kernel(pt_ref, q_ref, mask_ref, masknew_ref, knew_ref, vnew_ref, gm_ref, ck_hbm, cv_hbm,
                            o_ref, kslab, vslab, m_ref, l_ref, acc_ref, sem,
                            *, n_chunks, ch, n_tok, kvh, group, dh, page):
    b = pl.program_id(0)
    c = pl.program_id(1)
    t = b * n_chunks + c
    total = pl.num_programs(0) * n_chunks
    scale = dh ** -0.5

    def copies(seq, chunk, slot, j):
        pid = pt_ref[seq, chunk * ch + j]
        return (pltpu.make_async_copy(ck_hbm.at[0, pid], kslab.at[slot, j], sem.at[slot, 0]),
                pltpu.make_async_copy(cv_hbm.at[0, pid], vslab.at[slot, j], sem.at[slot, 1]))

    def start(seq, chunk, slot):
        for j in range(ch):
            ck, cv = copies(seq, chunk, slot, j)
            ck.start()
            cv.start()

    def wait(seq, chunk, slot):
        for j in range(ch):
            ck, cv = copies(seq, chunk, slot, j)
            ck.wait()
            cv.wait()

    @pl.when(t == 0)
    def _():
        start(0, 0, 0)

    slot = t % 2

    @pl.when(t + 1 < total)
    def _():
        nxt = t + 1
        start(nxt // n_chunks, nxt % n_chunks, 1 - slot)

    wait(b, c, slot)

    @pl.when(c == 0)
    def _():
        m_ref[...] = jnp.full_like(m_ref, NEG_BIG)
        l_ref[...] = jnp.zeros_like(l_ref)
        acc_ref[...] = jnp.zeros_like(acc_ref)

    qb = (q_ref[0] * scale).astype(bf16)
    heads = kvh * group

    def attend(keys_t, vals, mask8):
        n_cols = keys_t.shape[1]
        tok_mask = jnp.concatenate([jnp.broadcast_to(mask8[tk:tk + 1], (heads, n_cols))
                                    for tk in range(n_tok)], axis=0)
        madd = tok_mask + gm_ref[:, 0:n_cols]
        s = jnp.dot(qb, keys_t, preferred_element_type=f32) + madd
        m_old = m_ref[...]
        m_new = jnp.maximum(m_old, jnp.max(s, axis=-1, keepdims=True))
        pr = jnp.where(madd == 0.0, jnp.exp(s - m_new), 0.0)
        alpha = jnp.exp(m_old - m_new)
        l_ref[...] = alpha * l_ref[...] + jnp.sum(pr, axis=-1, keepdims=True)
        acc_ref[...] = alpha * acc_ref[...] + jnp.dot(pr.astype(bf16), vals, preferred_element_type=f32)
        m_ref[...] = m_new

    attend(kslab[slot].reshape(ch * page * kvh, dh).T.astype(bf16),
           vslab[slot].reshape(ch * page * kvh, dh).astype(bf16), mask_ref[0])

    @pl.when(c == n_chunks - 1)
    def _():
        attend(knew_ref[0].T.astype(bf16), vnew_ref[0].astype(bf16), masknew_ref[0])
        o_ref[0] = acc_ref[...] / l_ref[...]


def _dsa_sample(qi, wsel, q, knew_idx, k_new, v_new, cache_k, cache_v, cache_idx_k, page_table, n_tok):
    n_seq, n_pages = page_table.shape
    page, kvh, dh = cache_k.shape[2], cache_k.shape[3], cache_k.shape[4]
    assert page == LANE and LANE % kvh == 0
    rows_i = qi.shape[1]
    rows = q.shape[1]
    heads = rows // n_tok
    group = heads // kvh
    past = n_pages * page
    kp = past + LANE
    topk = min(TOPK_MAX, (past + n_tok) // 4)
    ch = math.gcd(SAMPLE_PAGES_PER_STEP, n_pages)
    n_chunks = n_pages // ch
    cw = ch * page * kvh
    own = (np.arange(rows)[:, None] % heads) // group == (np.arange(cw)[None, :] % kvh)
    gm = np.where(own, 0.0, NEG_BIG).astype(np.float32)

    mask = pl.pallas_call(
        functools.partial(_dsa_sample_select_kernel, n_pages=n_pages, page=page, topk=topk, n_tok=n_tok,
                          kvh=kvh),
        grid_spec=pltpu.PrefetchScalarGridSpec(
            num_scalar_prefetch=1,
            grid=(n_seq,),
            in_specs=[pl.BlockSpec((1, rows_i, IDX_DIM), lambda b, pt: (b, 0, 0)),
                      pl.BlockSpec((1, 8, rows_i), lambda b, pt: (b, 0, 0)),
                      pl.BlockSpec((1, LANE, IDX_DIM), lambda b, pt: (b, 0, 0)),
                      pl.BlockSpec(memory_space=pl.ANY)],
            out_specs=pl.BlockSpec((1, 8, kp * kvh), lambda b, pt: (b, 0, 0)),
            scratch_shapes=[pltpu.VMEM((2, past, IDX_DIM), f32), pltpu.VMEM((8, kp), f32),
                            pltpu.VMEM((8, kp), i32), pltpu.VMEM((8, kp), f32), pltpu.SemaphoreType.DMA((2,))]),
        out_shape=jax.ShapeDtypeStruct((n_seq, 8, kp * kvh), f32),
        compiler_params=_params(("arbitrary",)),
        name="dsa_sample_select",
    )(page_table, qi, wsel, knew_idx, cache_idx_k)

    return pl.pallas_call(
        functools.partial(_dsa_sample_attn_kernel, n_chunks=n_chunks, ch=ch, n_tok=n_tok, kvh=kvh, group=group,
                          dh=dh, page=page),
        grid_spec=pltpu.PrefetchScalarGridSpec(
            num_scalar_prefetch=1,
            grid=(n_seq, n_chunks),
            in_specs=[pl.BlockSpec((1, rows, dh), lambda b, c, pt: (b, 0, 0)),
                      pl.BlockSpec((1, 8, cw), lambda b, c, pt: (b, 0, c)),
                      pl.BlockSpec((1, 8, LANE * kvh), lambda b, c, pt: (b, 0, past // LANE)),
                      pl.BlockSpec((1, LANE * kvh, dh), lambda b, c, pt: (b, 0, 0)),
                      pl.BlockSpec((1, LANE * kvh, dh), lambda b, c, pt: (b, 0, 0)),
                      pl.BlockSpec(gm.shape, lambda b, c, pt: (0, 0)),
                      pl.BlockSpec(memory_space=pl.ANY),
                      pl.BlockSpec(memory_space=pl.ANY)],
            out_specs=pl.BlockSpec((1, rows, dh), lambda b, c, pt: (b, 0, 0)),
            scratch_shapes=[pltpu.VMEM((2, ch, page, kvh, dh), f32), pltpu.VMEM((2, ch, page, kvh, dh), f32),
                            pltpu.VMEM((rows, 1), f32), pltpu.VMEM((rows, 1), f32),
                            pltpu.VMEM((rows, dh), f32), pltpu.SemaphoreType.DMA((2, 2))]),
        out_shape=jax.ShapeDtypeStruct((n_seq, rows, dh), f32),
        compiler_params=_params(("arbitrary", "arbitrary")),
        name="dsa_sample_attn",
    )(page_table, q, mask, mask, k_new, v_new, jnp.asarray(gm), cache_k, cache_v)


def _merge_kernel(ga_ref, gb_ref, oap_ref, obp_ref, oas_ref, obs_ref, o_ref, *, prompt_blocks):
    is_prompt = pl.program_id(0) < prompt_blocks
    oa = jnp.where(is_prompt, oap_ref[...], oas_ref[...])
    ob = jnp.where(is_prompt, obp_ref[...], obs_ref[...])
    o_ref[...] = (_sigmoid(ga_ref[...]) * oa + _sigmoid(gb_ref[...]) * ob).astype(o_ref.dtype)


def _merge(gates, oa_p, ob_p, oa_s, ob_s):
    n = gates.shape[0]
    d = oa_p.shape[1]
    npb = oa_p.shape[0] // ROW_BLOCK
    assert oa_p.shape[0] + oa_s.shape[0] == n and oa_s.shape[0] % ROW_BLOCK == 0
    tn = _pick_tile(d, 1024, 128)
    nj = d // tn
    p_spec = pl.BlockSpec((ROW_BLOCK, tn), lambda i, j: (jnp.minimum(i, npb - 1), jnp.where(i < npb, j, nj - 1)))
    s_spec = pl.BlockSpec((ROW_BLOCK, tn), lambda i, j: (jnp.maximum(i - npb, 0), jnp.where(i < npb, 0, j)))
    return pl.pallas_call(
        functools.partial(_merge_kernel, prompt_blocks=npb),
        grid=(n // ROW_BLOCK, nj),
        in_specs=[pl.BlockSpec((ROW_BLOCK, tn), lambda i, j: (i, j)),
                  pl.BlockSpec((ROW_BLOCK, tn), lambda i, j: (i, nj + j)),
                  p_spec, p_spec, s_spec, s_spec],
        out_specs=pl.BlockSpec((ROW_BLOCK, tn), lambda i, j: (i, j)),
        out_shape=jax.ShapeDtypeStruct((n, d), bf16),
        compiler_params=_params(("arbitrary", "arbitrary")),
        name="merge_branches",
    )(gates, gates, oa_p, ob_p, oa_s, ob_s)


def _router_kernel(x_ref, g_ref, w_ref, b_ref, route_ref, cnt_ref, carry_ref):
    i = pl.program_id(0)

    @pl.when(i == 0)
    def _():
        carry_ref[...] = jnp.zeros_like(carry_ref)

    x = x_ref[...]
    ms = jnp.mean(x * x, axis=-1, keepdims=True)
    h = x * lax.rsqrt(ms + NORM_EPS) * g_ref[...]
    logits = jnp.dot(h, w_ref[...], precision=HIGHEST, preferred_element_type=f32) + b_ref[...]
    lane = lax.broadcasted_iota(i32, logits.shape, 1)
    big = jnp.int32(1 << 20)
    ng, epg = N_GROUPS, EXPERTS_PER_GROUP

    def first_max(vals):
        mx = jnp.max(vals, axis=-1, keepdims=True)
        idx = jnp.min(jnp.where(vals == mx, lane, big), axis=-1, keepdims=True)
        return mx, idx

    lg = jnp.where(lane < ng, logits, -jnp.inf)
    mg, g_sel = first_max(lg)
    p_g = 1.0 / jnp.sum(jnp.exp(lg - mg), axis=-1, keepdims=True)
    lo = ng + g_sel * epg
    le = jnp.where(jnp.logical_and(lane >= lo, lane < lo + epg), logits, -jnp.inf)
    m1, i1 = first_max(le)
    se = jnp.sum(jnp.exp(le - m1), axis=-1, keepdims=True)
    m2, i2 = first_max(jnp.where(lane == i1, -jnp.inf, le))
    p1 = 1.0 / se
    p2 = jnp.exp(m2 - m1) / se
    w1 = p_g * p1 / (p1 + p2)
    w2 = p_g * p2 / (p1 + p2)

    hot = jnp.logical_or(lane == i1, lane == i2)
    hot_b = jnp.where(hot, 1.0, 0.0).astype(bf16)
    n = x.shape[0]
    strict = (lax.broadcasted_iota(i32, (n, n), 0) > lax.broadcasted_iota(i32, (n, n), 1))
    before = jnp.dot(jnp.where(strict, 1.0, 0.0).astype(bf16), hot_b, preferred_element_type=f32)
    before = before + carry_ref[...]
    r1 = jnp.sum(jnp.where(lane == i1, before, 0.0), axis=-1, keepdims=True)
    r2 = jnp.sum(jnp.where(lane == i2, before, 0.0), axis=-1, keepdims=True)
    carry_ref[...] = carry_ref[...] + jnp.sum(jnp.where(hot, 1.0, 0.0), axis=0, keepdims=True)

    out = jnp.where(lane == 0, (i1 - ng).astype(f32), 0.0)
    out = jnp.where(lane == 1, (i2 - ng).astype(f32), out)
    out = jnp.where(lane == 2, w1, out)
    out = jnp.where(lane == 3, w2, out)
    out = jnp.where(lane == 4, r1, out)
    out = jnp.where(lane == 5, r2, out)
    route_ref[...] = out
    cnt_ref[...] = carry_ref[...]


def _router(x1, g, w_r, b_r):
    n, d = x1.shape
    return pl.pallas_call(
        _router_kernel,
        grid=(n // ROW_BLOCK,),
        in_specs=[pl.BlockSpec((ROW_BLOCK, d), lambda i: (i, 0)),
                  pl.BlockSpec((1, d), lambda i: (0, 0)),
                  pl.BlockSpec((d, LANE), lambda i: (0, 0)),
                  pl.BlockSpec((1, LANE), lambda i: (0, 0))],
        out_specs=[pl.BlockSpec((ROW_BLOCK, LANE), lambda i: (i, 0)),
                   pl.BlockSpec((1, LANE), lambda i: (0, 0))],
        out_shape=[jax.ShapeDtypeStruct((n, LANE), f32), jax.ShapeDtypeStruct((1, LANE), f32)],
        scratch_shapes=[pltpu.VMEM((1, LANE), f32)],
        compiler_params=_params(("arbitrary",)),
        name="moe_router",
    )(x1, g.reshape(1, d), w_r, b_r)


def _row_copy(src_hbm, dst_vmem, sem, src_row, dst_row):
    return pltpu.make_async_copy(src_hbm.at[pl.ds(src_row, 1), :], dst_vmem.at[pl.ds(dst_row, 1), :], sem)


def _dispatch_kernel(tok_ref, x_hbm, g_ref, o_ref, buf_ref, sem):
    i = pl.program_id(0)
    n = pl.num_programs(0)

    def for_real_rows(blk, fn):
        def body(r8, c):
            for u in range(DMA_ISSUE_UNROLL):
                r = r8 * DMA_ISSUE_UNROLL + u
                tok = tok_ref[blk * MOE_BLOCK + r]

                @pl.when(tok >= 0)
                def _(tok=tok, r=r, u=u):
                    fn(tok, r, u)
            return c
        lax.fori_loop(0, MOE_BLOCK // DMA_ISSUE_UNROLL, body, 0)

    def start(blk, slot):
        for_real_rows(blk, lambda tok, r, u: _row_copy(x_hbm, buf_ref.at[slot], sem.at[slot], tok, r).start(
            priority=u % 2))

    def wait(blk, slot):
        for_real_rows(blk, lambda tok, r, u: _row_copy(x_hbm, buf_ref.at[slot], sem.at[slot], tok, r).wait())

    @pl.when(i == 0)
    def _():
        buf_ref[...] = jnp.zeros_like(buf_ref)
        start(0, 0)

    slot = i % 2

    @pl.when(i + 1 < n)
    def _():
        start(i + 1, 1 - slot)

    wait(i, slot)
    x = buf_ref[slot]
    ms = jnp.mean(x * x, axis=-1, keepdims=True)
    o_ref[...] = (x * lax.rsqrt(ms + NORM_EPS) * g_ref[...]).astype(o_ref.dtype)


def _dispatch(row_tok, x1, g):
    n, d = x1.shape
    rows = row_tok.shape[0]
    return pl.pallas_call(
        _dispatch_kernel,
        grid_spec=pltpu.PrefetchScalarGridSpec(
            num_scalar_prefetch=1,
            grid=(rows // MOE_BLOCK,),
            in_specs=[pl.BlockSpec(memory_space=pl.ANY),
                      pl.BlockSpec((1, d), lambda i, tok: (0, 0))],
            out_specs=pl.BlockSpec((MOE_BLOCK, d), lambda i, tok: (i, 0)),
            scratch_shapes=[pltpu.VMEM((2, MOE_BLOCK, d), f32), pltpu.SemaphoreType.DMA((2,))]),
        out_shape=jax.ShapeDtypeStruct((rows, d), bf16),
        compiler_params=_params(("arbitrary",)),
        name="moe_dispatch",
    )(row_tok, x1, g.reshape(1, d))


def _expert_ffn_kernel(ie_ref, ir_ref, in_ref, nit_ref, xs_hbm, wg_ref, wu_ref, wd_ref, y_hbm,
                       x_ref, acc_ref, sem_in, sem_out, *, n_ff):
    it = pl.program_id(0)
    ff = pl.program_id(1)
    nblk = in_ref[it]
    row0 = pl.multiple_of(ir_ref[it] * MOE_BLOCK, MOE_BLOCK)
    span = MOE_ITEM_BLOCKS * MOE_BLOCK

    @pl.when(nblk > 0)
    def _():
        @pl.when(ff == 0)
        def _():
            cp = pltpu.make_async_copy(xs_hbm.at[pl.ds(row0, span), :], x_ref, sem_in)
            cp.start()
            acc_ref[...] = jnp.zeros_like(acc_ref)
            cp.wait()

        x = x_ref[...]
        gate = jnp.dot(x, wg_ref[0].astype(bf16), preferred_element_type=f32)
        up = jnp.dot(x, wu_ref[0].astype(bf16), preferred_element_type=f32)
        act = (gate * _sigmoid(gate) * up).astype(bf16)
        acc_ref[...] += jnp.dot(act, wd_ref[0].astype(bf16), preferred_element_type=f32)

        @pl.when(ff == n_ff - 1)
        def _():
            for sb in range(MOE_ITEM_BLOCKS):
                @pl.when(sb < nblk)
                def _(sb=sb):
                    rs = pl.ds(sb * MOE_BLOCK, MOE_BLOCK)
                    cp = pltpu.make_async_copy(acc_ref.at[rs, :],
                                               y_hbm.at[pl.ds(row0 + sb * MOE_BLOCK, MOE_BLOCK), :], sem_out)
                    cp.start()
                    cp.wait()

    @pl.when(jnp.logical_and(it == pl.num_programs(0) - 1, ff == n_ff - 1))
    def _():
        acc_ref[0:MOE_BLOCK, :] = jnp.zeros((MOE_BLOCK, acc_ref.shape[1]), f32)

        def fill(blk, c):
            cp = pltpu.make_async_copy(acc_ref.at[pl.ds(0, MOE_BLOCK), :],
                                       y_hbm.at[pl.ds(pl.multiple_of(blk * MOE_BLOCK, MOE_BLOCK), MOE_BLOCK), :],
                                       sem_out)
            cp.start()
            cp.wait()
            return c

        lax.fori_loop(nit_ref[1], y_hbm.shape[0] // MOE_BLOCK, fill, 0)


def _expert_ffn(item_e, item_r, item_n, n_items, xs, w_gate, w_up, w_down, max_items):
    rows, d = xs.shape
    dff = w_gate.shape[3]
    tf = _pick_tile(dff, MOE_FF_TILE, LANE)
    n_ff = dff // tf
    span = MOE_ITEM_BLOCKS * MOE_BLOCK

    def f_eff(i, f, nit):
        return jnp.where(i < nit[0], f, n_ff - 1)

    return pl.pallas_call(
        functools.partial(_expert_ffn_kernel, n_ff=n_ff),
        grid_spec=pltpu.PrefetchScalarGridSpec(
            num_scalar_prefetch=4,
            grid=(max_items, n_ff),
            in_specs=[pl.BlockSpec(memory_space=pl.ANY),
                      pl.BlockSpec((None, 1, d, tf), lambda i, f, ie, ir, inn, nit: (0, ie[i], 0, f_eff(i, f, nit))),
                      pl.BlockSpec((None, 1, d, tf), lambda i, f, ie, ir, inn, nit: (0, ie[i], 0, f_eff(i, f, nit))),
                      pl.BlockSpec((None, 1, tf, d), lambda i, f, ie, ir, inn, nit: (0, ie[i], f_eff(i, f, nit), 0))],
            out_specs=pl.BlockSpec(memory_space=pl.ANY),
            scratch_shapes=[pltpu.VMEM((span, d), bf16), pltpu.VMEM((span, d), f32),
                            pltpu.SemaphoreType.DMA(()), pltpu.SemaphoreType.DMA(())]),
        out_shape=jax.ShapeDtypeStruct((rows, d), f32),
        compiler_params=_params(("arbitrary", "arbitrary")),
        name="moe_expert_ffn",
    )(item_e, item_r, item_n, n_items, xs, w_gate, w_up, w_down)


def _combine_kernel(d1_ref, d2_ref, y_hbm, x_ref, route_ref, g_ref, o_ref, buf_ref, sem):
    i = pl.program_id(0)
    n = pl.num_programs(0)

    def start(blk, slot):
        def body(r8, c):
            for u in range(DMA_ISSUE_UNROLL):
                r = r8 * DMA_ISSUE_UNROLL + u
                _row_copy(y_hbm, buf_ref.at[slot, 0], sem.at[slot], d1_ref[blk * ROW_BLOCK + r], r).start(priority=0)
                _row_copy(y_hbm, buf_ref.at[slot, 1], sem.at[slot], d2_ref[blk * ROW_BLOCK + r], r).start(priority=1)
            return c
        lax.fori_loop(0, ROW_BLOCK // DMA_ISSUE_UNROLL, body, 0)

    def wait(slot):
        for r in range(ROW_BLOCK):
            _row_copy(y_hbm, buf_ref.at[slot, 0], sem.at[slot], 0, r).wait()
            _row_copy(y_hbm, buf_ref.at[slot, 1], sem.at[slot], 0, r).wait()

    @pl.when(i == 0)
    def _():
        start(0, 0)

    slot = i % 2

    @pl.when(i + 1 < n)
    def _():
        start(i + 1, 1 - slot)

    wait(slot)
    route = route_ref[...]
    x = x_ref[...] + route[:, 2:3] * buf_ref[slot, 0] + route[:, 3:4] * buf_ref[slot, 1]
    ms = jnp.mean(x * x, axis=-1, keepdims=True)
    o_ref[...] = x * lax.rsqrt(ms + NORM_EPS) * g_ref[...]


def _combine(d1, d2, yb, x1, route, g):
    n, d = x1.shape
    return pl.pallas_call(
        _combine_kernel,
        grid_spec=pltpu.PrefetchScalarGridSpec(
            num_scalar_prefetch=2,
            grid=(n // ROW_BLOCK,),
            in_specs=[pl.BlockSpec(memory_space=pl.ANY),
                      pl.BlockSpec((ROW_BLOCK, d), lambda i, a, b: (i, 0)),
                      pl.BlockSpec((ROW_BLOCK, LANE), lambda i, a, b: (i, 0)),
                      pl.BlockSpec((1, d), lambda i, a, b: (0, 0))],
            out_specs=pl.BlockSpec((ROW_BLOCK, d), lambda i, a, b: (i, 0)),
            scratch_shapes=[pltpu.VMEM((2, 2, ROW_BLOCK, d), f32), pltpu.SemaphoreType.DMA((2,))]),
        out_shape=jax.ShapeDtypeStruct((n, d), f32),
        compiler_params=_params(("arbitrary",)),
        name="moe_combine",
    )(d1, d2, yb, x1, route, g.reshape(1, d))


def _moe(x1, norm_ffn, w_rg, b_rg, w_re, b_re, w_gate, w_up, w_down, norm_final):
    n, d = x1.shape
    n_exp = w_gate.shape[1]
    ng = w_rg.shape[1]
    w_r = jnp.zeros((d, LANE), f32).at[:, :ng].set(w_rg).at[:, ng:ng + n_exp].set(w_re)
    b_r = jnp.zeros((1, LANE), f32).at[0, :ng].set(b_rg).at[0, ng:ng + n_exp].set(b_re)
    route, cnt = _router(x1, norm_ffn, w_r, b_r)

    e1 = route[:, 0].astype(i32)
    e2 = route[:, 1].astype(i32)
    counts = cnt[0, ng:ng + n_exp].astype(i32)
    nblk_e = (counts + MOE_BLOCK - 1) // MOE_BLOCK
    blk_start = jnp.cumsum(nblk_e) - nblk_e
    def first_row(e):
        hot = e[:, None] == jnp.arange(n_exp, dtype=i32)[None, :]
        return jnp.sum(jnp.where(hot, blk_start[None, :], 0), axis=1) * MOE_BLOCK

    d1 = first_row(e1) + route[:, 4].astype(i32)
    d2 = first_row(e2) + route[:, 5].astype(i32)
    max_blocks = -(-2 * n // MOE_BLOCK) + n_exp
    rows = (max_blocks + MOE_ITEM_BLOCKS) * MOE_BLOCK
    tok = jnp.arange(n, dtype=i32)
    row_tok = jnp.full((rows,), -1, i32).at[jnp.concatenate([d1, d2])].set(
        jnp.concatenate([tok, tok]), unique_indices=True)
    items_e = (nblk_e + MOE_ITEM_BLOCKS - 1) // MOE_ITEM_BLOCKS
    item_end = jnp.cumsum(items_e)
    n_items = item_end[-1]
    max_items = -(-max_blocks // MOE_ITEM_BLOCKS) + n_exp
    it = jnp.arange(max_items, dtype=i32)
    it_c = jnp.minimum(it, n_items - 1)
    item_e = jnp.minimum(jnp.searchsorted(item_end, it_c, side="right"), n_exp - 1).astype(i32)
    local = it_c - (item_end - items_e)[item_e]
    item_r = (blk_start[item_e] + local * MOE_ITEM_BLOCKS).astype(i32)
    item_n = jnp.where(it < n_items, jnp.minimum(MOE_ITEM_BLOCKS, nblk_e[item_e] - local * MOE_ITEM_BLOCKS), 0)

    xs = _dispatch(row_tok, x1, norm_ffn)
    meta = jnp.stack([n_items, jnp.sum(nblk_e)]).astype(i32)
    yb = _expert_ffn(item_e, item_r, item_n.astype(i32), meta, xs, w_gate, w_up, w_down, max_items)
    return _combine(d1, d2, yb, x1, route, norm_final)


def kernel(x_prompt, x_sample, cache_k, cache_v, cache_idx_k, state_hgrn, page_table, norm_mix, w_in,
           lb_logits, norm_hgrn_out, w_out, norm_ffn, router_group, router_group_bias, router_expert,
           router_expert_bias, expert_w_gate, expert_w_up, expert_w_down, norm_final):
    n_b, seq, d = x_prompt.shape
    n_s, n_tok, _ = x_sample.shape
    depth = norm_mix.shape[0]
    assert depth == 1
    kd = A_HEADS * A_EXPAND
    dh = d // B_HEADS
    kvd = B_KV_HEADS * dh
    group = B_HEADS // B_KV_HEADS
    np_rows = n_b * seq
    ns_rows = n_s * n_tok
    n_all = np_rows + ns_rows
    n_pad = -(-n_all // ROW_BLOCK) * ROW_BLOCK
    assert np_rows % ROW_BLOCK == 0

    lower = jnp.cumsum(jax.nn.softmax(lb_logits.astype(f32), axis=0), axis=0)[0]
    x_all = jnp.concatenate([x_prompt.reshape(np_rows, d), x_sample.reshape(ns_rows, d),
                             jnp.zeros((n_pad - n_all, d), f32)], axis=0)

    w = w_in[0]
    o_bq = 4 * kd
    o_bk = o_bq + d
    o_bqi = o_bk + 2 * kvd
    o_sm = o_bqi + IDX_HEADS * IDX_DIM
    o_g = o_sm + IDX_DIM + IDX_HEADS
    small_w = 2 * LANE
    h = _rmsnorm_bf16(x_all, norm_mix[0])
    wb = w.astype(bf16)
    z_a = _matmul(h, wb, name="proj_hgrn", col0=0, n=o_bq)
    z_bq = _matmul(h, wb, name="proj_q", col0=o_bq, n=o_bk - o_bq)
    z_kv = _matmul(h, wb, name="proj_kv", col0=o_bk, n=o_bqi - o_bk)
    z_qi = _matmul_heads(h, wb, "proj_qidx", o_bqi, o_sm - o_bqi)
    w_small = jnp.zeros((d, small_w), bf16).at[:, :o_g - o_sm].set(wb[:, o_sm:o_g])
    z_sm = _matmul(h, w_small, name="proj_small")
    z_g = _matmul(h, wb[:, o_g:], name="proj_gates")

    rb = min(HGRN_ROWS, seq)
    oa_p, s_p = _hgrn(z_a, lower, norm_hgrn_out[0], n_b, seq, rb, min(HGRN_CHUNK, rb), rb,
                      min(HGRN_PROMPT_HEADS, A_HEADS))
    sp = HGRN_SAMPLE_ROWS
    zs = z_a[np_rows:n_all].reshape(n_s, n_tok, 4 * kd)
    zs = jnp.concatenate([zs, jnp.zeros((n_s, sp - n_tok, 4 * kd), f32)], axis=1).reshape(n_s * sp, 4 * kd)
    oa_sp, s_s = _hgrn(zs, lower, norm_hgrn_out[0], n_s, sp, sp, sp, n_tok, min(HGRN_SAMPLE_HEADS, A_HEADS),
                       s0=state_hgrn[0])
    oa_s = oa_sp.reshape(n_s, sp, d)[:, :n_tok].reshape(ns_rows, d)

    ob_p = _dsa_prompt(z_qi, z_sm, z_bq, z_kv, n_b, seq)
    sm_s = z_sm[np_rows:n_all]
    w_s = sm_s[:, IDX_DIM:IDX_DIM + IDX_HEADS].reshape(n_s, n_tok, IDX_HEADS) * (IDX_HEADS ** -0.5 * IDX_DIM ** -0.5)
    eye = jnp.eye(8, n_tok, dtype=f32)
    wsel = (eye[None, :, :, None] * w_s[:, None, :, :]).reshape(n_s, 8, n_tok * IDX_HEADS)
    qi_s = z_qi[np_rows // ROW_BLOCK:].transpose(0, 2, 1, 3).reshape(n_pad - np_rows, IDX_HEADS, IDX_DIM)
    qi_s = qi_s[:ns_rows].reshape(n_s, n_tok * IDX_HEADS, IDX_DIM)
    q_s = z_bq[np_rows:n_all].reshape(n_s, n_tok * B_HEADS, dh)

    def pad_rows_to_lane(a):
        return jnp.concatenate([a, jnp.zeros((n_s, LANE - n_tok) + a.shape[2:], f32)], axis=1)

    def new_kv(a):
        return pad_rows_to_lane(a.reshape(n_s, n_tok, B_KV_HEADS, dh)).reshape(n_s, LANE * B_KV_HEADS, dh)

    kv_s = z_kv[np_rows:n_all]
    ob_s = _dsa_sample(qi_s, wsel, q_s, pad_rows_to_lane(sm_s[:, :IDX_DIM].reshape(n_s, n_tok, IDX_DIM)),
                       new_kv(kv_s[:, :kvd]), new_kv(kv_s[:, kvd:]),
                       cache_k, cache_v, cache_idx_k, page_table, n_tok)
    ob_s = ob_s.reshape(ns_rows, d)

    pad_rows = jnp.zeros((n_pad - n_all, d), f32)
    m = _merge(z_g, oa_p, ob_p, jnp.concatenate([oa_s, pad_rows], axis=0), jnp.concatenate([ob_s, pad_rows], axis=0))
    x1 = _matmul(m, w_out[0].astype(bf16), res=x_all, name="out_proj")

    y = _moe(x1, norm_ffn[0], router_group[0], router_group_bias[0], router_expert[0], router_expert_bias[0],
             expert_w_gate, expert_w_up, expert_w_down, norm_final)

    kp = z_kv[:np_rows]
    ks = z_kv[np_rows:n_all]
    return (y[:np_rows].reshape(n_b, seq, d),
            y[np_rows:n_all].reshape(n_s, n_tok, d),
            kp[:, :kvd].reshape(1, n_b, seq, B_KV_HEADS, dh),
            kp[:, kvd:].reshape(1, n_b, seq, B_KV_HEADS, dh),
            z_sm[:np_rows, :IDX_DIM].reshape(1, n_b, seq, IDX_DIM),
            s_p[None],
            ks[:, :kvd].reshape(1, n_s, n_tok, B_KV_HEADS, dh),
            ks[:, kvd:].reshape(1, n_s, n_tok, B_KV_HEADS, dh),
            z_sm[np_rows:n_all, :IDX_DIM].reshape(1, n_s, n_tok, IDX_DIM),
            s_s[None])
```
